```python
import jax, jax.numpy as jnp
from jax import lax
import numpy as np

D_MODEL = 2048
BATCH = 1
SEQ = 8192
DEPTH = 1

POOL_WIDTH = D_MODEL // 2
POOL_WINDOWS = (2, 4, 8, 16)
N_POOL_GROUPS = len(POOL_WINDOWS)
POOL_GROUP = POOL_WIDTH // N_POOL_GROUPS
N_Q_HEADS = 16
N_KV_HEADS = 4
GROUP = N_Q_HEADS // N_KV_HEADS
HEAD_DIM = D_MODEL // N_Q_HEADS
NSA_WIDTH = N_Q_HEADS * HEAD_DIM
KV_WIDTH = N_KV_HEADS * HEAD_DIM
CMP_BLOCK = 32
CMP_STRIDE = 16
CMP_HIDDEN = 2 * HEAD_DIM
SEL_BLOCK = 64
N_SEL = 16
WINDOW = 512
Q_BLOCK = 128
ROPE_THETA = 10000.0
EPS = 1e-6
NEG = -1e30
FORCE = 1e6
N_BRANCHES = 2
IN_SPLITS = (POOL_WIDTH, POOL_WIDTH, NSA_WIDTH, KV_WIDTH, KV_WIDTH, KV_WIDTH, KV_WIDTH,
             KV_WIDTH, KV_WIDTH, NSA_WIDTH, 3 * N_Q_HEADS)
D_IN = sum(IN_SPLITS)

kernel_name = "hybrid_pool_nsa_gated_block"


def rmsnorm(x, w):
    xf = x.astype(jnp.float32)
    y = xf * lax.rsqrt(jnp.mean(xf * xf, axis=-1, keepdims=True) + EPS)
    return (y * w.astype(jnp.float32)).astype(x.dtype)


def rope(x, pos):
    half = HEAD_DIM // 2
    inv = ROPE_THETA ** (-jnp.arange(half, dtype=jnp.float32) / half)
    ang = pos.astype(jnp.float32)[:, None] * inv[None, :]
    cos = jnp.cos(ang)[None, :, None, :]
    sin = jnp.sin(ang)[None, :, None, :]
    xf = x.astype(jnp.float32)
    x1, x2 = xf[..., :half], xf[..., half:]
    return jnp.concatenate([x1 * cos - x2 * sin, x2 * cos + x1 * sin], axis=-1).astype(x.dtype)


def pool_mixer(u, mix, scale):
    b, s, _ = u.shape
    ug = u.reshape(b, s, N_POOL_GROUPS, POOL_GROUP)
    uf = ug.astype(jnp.float32)
    c = jnp.pad(jnp.cumsum(uf, axis=1), ((0, 0), (1, 0), (0, 0), (0, 0)))
    t = np.arange(s)
    win = np.array(POOL_WINDOWS)
    lo = np.maximum(t[:, None] + 1 - win[None, :], 0)
    cnt = jnp.asarray(np.minimum(t[:, None] + 1, win[None, :]).astype(np.float32))
    c_lo = c[:, lo, np.arange(N_POOL_GROUPS)[None, :]]
    mean = (c[:, 1:] - c_lo) / cnt[None, :, :, None]
    pooled = (mean - uf).astype(u.dtype)
    z = jnp.einsum("bsgc,gcd->bsgd", pooled, mix).reshape(b, s, POOL_WIDTH)
    return z * scale


def compress(kraw, pe, w1, w2):
    s = kraw.shape[1]
    n_cmp = (s - CMP_BLOCK) // CMP_STRIDE + 1
    idx = np.arange(n_cmp)[:, None] * CMP_STRIDE + np.arange(CMP_BLOCK)[None, :]
    kb = kraw[:, idx] + pe[None, None, :, None, :]
    hdn = jax.nn.silu(jnp.einsum("bnlhd,ldf->bnhf", kb, w1))
    return jnp.einsum("bnhf,fe->bnhe", hdn, w2)


def nsa_attention(q, kc, vc, ks, vs, kw, vw, gates):
    b, s = q.shape[0], q.shape[1]
    n_cmp = kc.shape[1]
    n_blk = s // SEL_BLOCK
    n_sel = min(N_SEL, n_blk)
    scale = HEAD_DIM ** -0.5
    cmp_end = jnp.arange(n_cmp) * CMP_STRIDE + CMP_BLOCK - 1
    ratio = SEL_BLOCK // CMP_STRIDE
    sub = CMP_BLOCK // CMP_STRIDE
    sel_idx = (np.arange(n_blk)[:, None, None] * ratio + np.arange(ratio)[None, :, None]
               - np.arange(sub)[None, None, :]).reshape(n_blk, ratio * sub)
    sel_valid = jnp.asarray((sel_idx >= 0) & (sel_idx < n_cmp))
    sel_idx = np.clip(sel_idx, 0, n_cmp - 1)
    ks_blk = ks.reshape(b, n_blk, SEL_BLOCK, N_KV_HEADS, HEAD_DIM).transpose(0, 3, 1, 2, 4)
    vs_blk = vs.reshape(b, n_blk, SEL_BLOCK, N_KV_HEADS, HEAD_DIM).transpose(0, 3, 1, 2, 4)
    kw_pad = jnp.pad(kw, ((0, 0), (WINDOW, 0), (0, 0), (0, 0)))
    vw_pad = jnp.pad(vw, ((0, 0), (WINDOW, 0), (0, 0), (0, 0)))
    gather = jax.vmap(jax.vmap(lambda kt, ix: kt[ix]))
    jblk = jnp.arange(n_blk)

    def block(i):
        s0 = i * Q_BLOCK
        qb = lax.dynamic_slice_in_dim(q, s0, Q_BLOCK, axis=1)
        gb = lax.dynamic_slice_in_dim(gates, s0, Q_BLOCK, axis=1)
        tpos = s0 + jnp.arange(Q_BLOCK)
        sc = jnp.einsum("bqhgd,bnhd->bhgqn", qb, kc, preferred_element_type=jnp.float32) * scale
        cvalid = cmp_end[None, :] <= tpos[:, None]
        p_c = jnp.where(cvalid, jax.nn.softmax(jnp.where(cvalid, sc, NEG), axis=-1), 0.0)
        o_c = jnp.einsum("bhgqn,bnhe->bqhge", p_c.astype(vc.dtype), vc)
        imp_c = p_c.sum(axis=2)
        imp = jnp.where(sel_valid, imp_c[..., sel_idx], 0.0).sum(-1)
        jt = (tpos // SEL_BLOCK)[:, None]
        forced = (jblk[None, :] == 0) | (jblk[None, :] == jt) | (jblk[None, :] == jt - 1)
        imp = jnp.where(forced, FORCE, imp)
        imp = jnp.where(jblk[None, :] > jt, NEG, imp)
        _, top = lax.top_k(imp, n_sel)
        kg = gather(ks_blk, top)
        vg = gather(vs_blk, top)
        kpos = top[..., None] * SEL_BLOCK + jnp.arange(SEL_BLOCK)
        svalid = kpos <= tpos[None, None, :, None, None]
        ss = jnp.einsum("bqhgd,bhqnld->bhgqnl", qb, kg, preferred_element_type=jnp.float32) * scale
        ss = jnp.where(svalid[:, :, None], ss, NEG)
        p_s = jax.nn.softmax(ss.reshape(ss.shape[:4] + (-1,)), axis=-1).reshape(ss.shape)
        o_s = jnp.einsum("bhgqnl,bhqnle->bqhge", p_s.astype(vg.dtype), vg)
        kwb = lax.dynamic_slice_in_dim(kw_pad, s0, WINDOW + Q_BLOCK, axis=1)
        vwb = lax.dynamic_slice_in_dim(vw_pad, s0, WINDOW + Q_BLOCK, axis=1)
        wpos = s0 - WINDOW + jnp.arange(WINDOW + Q_BLOCK)
        diff = tpos[:, None] - wpos[None, :]
        wvalid = (diff >= 0) & (diff < WINDOW) & (wpos[None, :] >= 0)
        sw = jnp.einsum("bqhgd,bkhd->bhgqk", qb, kwb, preferred_element_type=jnp.float32) * scale
        p_w = jax.nn.softmax(jnp.where(wvalid, sw, NEG), axis=-1)
        o_w = jnp.einsum("bhgqk,bkhe->bqhge", p_w.astype(vwb.dtype), vwb)
        return gb[..., 0:1] * o_c + gb[..., 1:2] * o_s + gb[..., 2:3] * o_w

    out = lax.map(block, jnp.arange(s // Q_BLOCK))
    return out.transpose(1, 0, 2, 3, 4, 5).reshape(b, s, NSA_WIDTH)


def setup_inputs(seed: int = 0) -> dict:
    key = jax.random.key(seed)
    ks = jax.random.split(key, 20)
    f32 = jnp.float32
    nrm = lambda k, shape, fan: jax.random.normal(k, shape, f32) * (fan ** -0.5)
    L = DEPTH
    return {
        "x": jax.random.normal(ks[0], (BATCH, SEQ, D_MODEL), f32),
        "norm_w": 1.0 + 0.05 * jax.random.normal(ks[1], (L, D_MODEL), f32),
        "w_in": nrm(ks[2], (L, D_MODEL, D_IN), D_MODEL),
        "pool_mix": nrm(ks[3], (L, N_POOL_GROUPS, POOL_GROUP, POOL_GROUP), POOL_GROUP),
        "pool_scale": 1.0 + 0.1 * jax.random.normal(ks[4], (L, POOL_WIDTH), f32),
        "cmp_pe_k": 0.1 * jax.random.normal(ks[5], (L, CMP_BLOCK, HEAD_DIM), f32),
        "cmp_w1_k": nrm(ks[6], (L, CMP_BLOCK, HEAD_DIM, CMP_HIDDEN), CMP_BLOCK * HEAD_DIM),
        "cmp_w2_k": nrm(ks[7], (L, CMP_HIDDEN, HEAD_DIM), CMP_HIDDEN),
        "cmp_pe_v": 0.1 * jax.random.normal(ks[8], (L, CMP_BLOCK, HEAD_DIM), f32),
        "cmp_w1_v": nrm(ks[9], (L, CMP_BLOCK, HEAD_DIM, CMP_HIDDEN), CMP_BLOCK * HEAD_DIM),
        "cmp_w2_v": nrm(ks[10], (L, CMP_HIDDEN, HEAD_DIM), CMP_HIDDEN),
        "w_pool_out": nrm(ks[11], (L, POOL_WIDTH, D_MODEL), POOL_WIDTH),
        "w_nsa_out": nrm(ks[12], (L, NSA_WIDTH, D_MODEL), NSA_WIDTH),
        "w_merge": nrm(ks[13], (L, D_MODEL, N_BRANCHES * D_MODEL), D_MODEL),
        "b_merge": 0.01 * jax.random.normal(ks[14], (L, N_BRANCHES * D_MODEL), f32),
        "w_out": nrm(ks[15], (L, D_MODEL, D_MODEL), D_MODEL),
        "final_norm_w": 1.0 + 0.05 * jax.random.normal(ks[16], (D_MODEL,), f32),
    }


def reference(x, norm_w, w_in, pool_mix, pool_scale, cmp_pe_k, cmp_w1_k, cmp_w2_k,
              cmp_pe_v, cmp_w1_v, cmp_w2_v, w_pool_out, w_nsa_out, w_merge, b_merge,
              w_out, final_norm_w):
    b, s, _ = x.shape
    pos = jnp.arange(s)
    n_cmp = (s - CMP_BLOCK) // CMP_STRIDE + 1
    cmp_pos = jnp.arange(n_cmp) * CMP_STRIDE + CMP_BLOCK - 1
    split_at = np.cumsum(IN_SPLITS)[:-1].tolist()
    for layer in range(DEPTH):
        h = rmsnorm(x, norm_w[layer])
        proj = h @ w_in[layer]
        (u_pool, g_pool, q, kc_raw, vc_raw, k_sel, v_sel, k_win, v_win,
         g_nsa, g_br) = jnp.split(proj, split_at, axis=-1)
        y_pool = pool_mixer(u_pool, pool_mix[layer], pool_scale[layer]) * jax.nn.silu(g_pool)
        y_a = y_pool @ w_pool_out[layer]
        heads = lambda t, n: t.reshape(b, s, n, HEAD_DIM)
        qh = rope(heads(q, N_Q_HEADS), pos).reshape(b, s, N_KV_HEADS, GROUP, HEAD_DIM)
        kc = rope(compress(heads(kc_raw, N_KV_HEADS), cmp_pe_k[layer], cmp_w1_k[layer],
                           cmp_w2_k[layer]), cmp_pos)
        vc = compress(heads(vc_raw, N_KV_HEADS), cmp_pe_v[layer], cmp_w1_v[layer], cmp_w2_v[layer])
        ksh = rope(heads(k_sel, N_KV_HEADS), pos)
        kwh = rope(heads(k_win, N_KV_HEADS), pos)
        gates = jax.nn.sigmoid(g_br).reshape(b, s, N_KV_HEADS, GROUP, 3)
        o_nsa = nsa_attention(qh, kc, vc, ksh, heads(v_sel, N_KV_HEADS), kwh,
                              heads(v_win, N_KV_HEADS), gates)
        y_b = (o_nsa * jax.nn.silu(g_nsa)) @ w_nsa_out[layer]
        gm = jax.nn.sigmoid(h @ w_merge[layer] + b_merge[layer]).reshape(b, s, N_BRANCHES, D_MODEL)
        merged = gm[:, :, 0] * y_a + gm[:, :, 1] * y_b
        x = x + merged @ w_out[layer]
    return rmsnorm(x, final_norm_w)
```

```python
import functools
import math

import numpy as np
import jax
import jax.numpy as jnp
from jax import lax
from jax.experimental import pallas as pl
from jax.experimental.pallas import tpu as pltpu

F32 = jnp.float32
BF16 = jnp.bfloat16

D_MODEL = 2048
SEQ = 8192
POOL_WIDTH = 1024
POOL_WINDOWS = (2, 4, 8, 16)
POOL_GROUP = 256
POOL_HALO = 16
N_Q_HEADS = 16
N_KV_HEADS = 4
GROUP = 4
HEAD_DIM = 128
NSA_WIDTH = 2048
KV_WIDTH = 512
CMP_BLOCK = 32
CMP_STRIDE = 16
CMP_HIDDEN = 256
N_CMP = (SEQ - CMP_BLOCK) // CMP_STRIDE + 1
N_CMP_PAD = 512
SEL_BLOCK = 64
N_BLK = SEQ // SEL_BLOCK
N_SEL = 16
WINDOW = 512
Q_BLOCK = 128
N_QB = SEQ // Q_BLOCK
ROPE_THETA = 10000.0
EPS = 1e-6
NEG = -1e30
FORCE = 1e6
REMOVED = -3e38
LOG2E = 1.4426950408889634
Q_SCALE = HEAD_DIM ** -0.5 * LOG2E

COL_GM = 0
COL_U = 4096
COL_GP = 5120
COL_Q = 6144
COL_KC = 8192
COL_VC = 8704
COL_KS = 9216
COL_VS = 9728
COL_KW = 10240
COL_VW = 10752
COL_GN = 11264
N_PROJ = 13312
W_IN_MAIN = 9216

VMEM_LIMIT = 56 * 1024 * 1024

PROJ_TM = 512
PROJ_TN = 1024
N_GM_TILES = 4096 // PROJ_TN
PREP_TM = 512
SEL_TK = 512
WIN_KEYS = WINDOW + Q_BLOCK
POOL_TM = 512
OUT_TM = 256


def _nt_dot(a, b):
    return lax.dot_general(a, b, (((1,), (1,)), ((), ())), preferred_element_type=F32)


def _dot(a, b):
    return jnp.dot(a, b, preferred_element_type=F32)


def _proj_kernel(x_ref, nw_ref, w_ref, wg_ref, b_ref, o_ref, g_ref, h_scr):
    j = pl.program_id(1)

    @pl.when(j == 0)
    def _():
        x = x_ref[...]
        ms = jnp.mean(x * x, axis=-1, keepdims=True)
        hb = (x * lax.rsqrt(ms + EPS) * nw_ref[...]).astype(BF16)
        h_scr[...] = hb
        g_ref[...] = jax.nn.sigmoid(_dot(hb, wg_ref[...]))

    acc = _dot(h_scr[...], w_ref[...])

    @pl.when(j < N_GM_TILES)
    def _():
        o_ref[...] = jax.nn.sigmoid(acc + b_ref[...])

    @pl.when(j >= N_GM_TILES)
    def _():
        o_ref[...] = acc


def _proj(x2, norm_w, w_cat, w_gate, b_merge):
    grid = (SEQ // PROJ_TM, N_PROJ // PROJ_TN)
    return pl.pallas_call(
        _proj_kernel,
        grid=grid,
        in_specs=[
            pl.BlockSpec((PROJ_TM, D_MODEL), lambda i, j: (i, 0)),
            pl.BlockSpec((1, D_MODEL), lambda i, j: (0, 0)),
            pl.BlockSpec((D_MODEL, PROJ_TN), lambda i, j: (0, j)),
            pl.BlockSpec((D_MODEL, N_KV_HEADS * 128), lambda i, j: (0, 0)),
            pl.BlockSpec((1, PROJ_TN), lambda i, j: (0, jnp.minimum(j, N_GM_TILES - 1))),
        ],
        out_specs=[
            pl.BlockSpec((PROJ_TM, PROJ_TN), lambda i, j: (i, j)),
            pl.BlockSpec((PROJ_TM, N_KV_HEADS * 128), lambda i, j: (i, 0)),
        ],
        out_shape=[
            jax.ShapeDtypeStruct((SEQ, N_PROJ), F32),
            jax.ShapeDtypeStruct((SEQ, N_KV_HEADS * 128), F32),
        ],
        scratch_shapes=[pltpu.VMEM((PROJ_TM, D_MODEL), BF16)],
        compiler_params=pltpu.CompilerParams(
            dimension_semantics=("arbitrary", "arbitrary"), vmem_limit_bytes=VMEM_LIMIT),
        name="proj",
    )(x2, norm_w, w_cat, w_gate, b_merge)


def _rope(x, cos, sin_signed):
    return x * cos + pltpu.roll(x, HEAD_DIM // 2, 1) * sin_signed


def _prep_kernel(q_ref, ks_ref, vs_ref, kw_ref, vw_ref, cos_ref, sin_ref,
                 qo_ref, kso_ref, vso_ref, kwo_ref, vwo_ref):
    i = pl.program_id(0)
    cos = cos_ref[...]
    sin = sin_ref[...]
    for hq in range(N_Q_HEADS):
        sl = slice(hq * HEAD_DIM, (hq + 1) * HEAD_DIM)
        qo_ref[:, sl] = (_rope(q_ref[:, sl], cos, sin) * Q_SCALE).astype(BF16)
    blk = (i * PREP_TM + lax.broadcasted_iota(jnp.int32, (PREP_TM, N_BLK), 0)) // SEL_BLOCK
    onehot = jnp.where(blk == lax.broadcasted_iota(jnp.int32, (PREP_TM, N_BLK), 1), 1.0, 0.0).astype(BF16)
    for h in range(N_KV_HEADS):
        sl = slice(h * HEAD_DIM, (h + 1) * HEAD_DIM)
        kso_ref[h, :, 0:HEAD_DIM] = _rope(ks_ref[:, sl], cos, sin).astype(BF16)
        kso_ref[h, :, HEAD_DIM:2 * HEAD_DIM] = onehot
        kwo_ref[h] = _rope(kw_ref[:, sl], cos, sin).astype(BF16)
        vso_ref[h] = vs_ref[:, sl].T.astype(BF16)
        vwo_ref[h] = vw_ref[:, sl].T.astype(BF16)


def _prep(proj, cos_tab, sin_tab):
    tm = PREP_TM
    kv_spec = lambda col: pl.BlockSpec((tm, KV_WIDTH), lambda i: (i, col // KV_WIDTH))
    return pl.pallas_call(
        _prep_kernel,
        grid=(SEQ // tm,),
        in_specs=[
            pl.BlockSpec((tm, NSA_WIDTH), lambda i: (i, COL_Q // NSA_WIDTH)),
            kv_spec(COL_KS), kv_spec(COL_VS), kv_spec(COL_KW), kv_spec(COL_VW),
            pl.BlockSpec((tm, HEAD_DIM), lambda i: (i, 0)),
            pl.BlockSpec((tm, HEAD_DIM), lambda i: (i, 0)),
        ],
        out_specs=[
            pl.BlockSpec((tm, NSA_WIDTH), lambda i: (i, 0)),
            pl.BlockSpec((N_KV_HEADS, tm, 2 * HEAD_DIM), lambda i: (0, i, 0)),
            pl.BlockSpec((N_KV_HEADS, HEAD_DIM, tm), lambda i: (0, 0, i)),
            pl.BlockSpec((N_KV_HEADS, tm, HEAD_DIM), lambda i: (0, i, 0)),
            pl.BlockSpec((N_KV_HEADS, HEAD_DIM, tm), lambda i: (0, 0, i)),
        ],
        out_shape=[
            jax.ShapeDtypeStruct((SEQ, NSA_WIDTH), BF16),
            jax.ShapeDtypeStruct((N_KV_HEADS, SEQ, 2 * HEAD_DIM), BF16),
            jax.ShapeDtypeStruct((N_KV_HEADS, HEAD_DIM, SEQ), BF16),
            jax.ShapeDtypeStruct((N_KV_HEADS, SEQ, HEAD_DIM), BF16),
            jax.ShapeDtypeStruct((N_KV_HEADS, HEAD_DIM, SEQ), BF16),
        ],
        compiler_params=pltpu.CompilerParams(
            dimension_semantics=("arbitrary",), vmem_limit_bytes=VMEM_LIMIT),
        name="prep",
    )(proj, proj, proj, proj, proj, cos_tab, sin_tab)


def _cmp_kernel(x_ref, pe_ref, w1_ref, w2_ref, cos_ref, sin_ref, o_ref, ot_ref, hi_scr):
    half = CMP_BLOCK // 2
    acc_lo = jnp.zeros((N_CMP_PAD, CMP_HIDDEN), F32)
    acc_hi = jnp.zeros((N_CMP_PAD, CMP_HIDDEN), F32)
    for l in range(half):
        xl = x_ref[pl.ds(l, N_CMP_PAD, stride=CMP_STRIDE), :]
        acc_lo += _dot((xl + pe_ref[0, l:l + 1, :]).astype(BF16), w1_ref[0, l])
        acc_hi += _dot((xl + pe_ref[0, half + l:half + l + 1, :]).astype(BF16), w1_ref[0, half + l])
    hi_scr[0:N_CMP_PAD, :] = acc_hi
    hi_scr[N_CMP_PAD:N_CMP_PAD + 8, :] = jnp.zeros((8, CMP_HIDDEN), F32)
    pre = acc_lo + hi_scr[pl.ds(1, N_CMP_PAD), :]
    hdn = pre * jax.nn.sigmoid(pre)
    out = _dot(hdn.astype(BF16), w2_ref[0])
    out = _rope(out, cos_ref[0], sin_ref[0])
    row = lax.broadcasted_iota(jnp.int32, (N_CMP_PAD, HEAD_DIM), 0)
    out = jnp.where(row < N_CMP, out, 0.0)
    o_ref[0, 0] = out.astype(BF16)
    ot_ref[0, 0] = out.T.astype(BF16)


def _compress(proj, pe, w1, w2, cos_c, sin_c):
    return pl.pallas_call(
        _cmp_kernel,
        grid=(2, N_KV_HEADS),
        in_specs=[
            pl.BlockSpec((SEQ, HEAD_DIM), lambda k, h: (0, COL_KC // HEAD_DIM + N_KV_HEADS * k + h)),
            pl.BlockSpec((1, CMP_BLOCK, HEAD_DIM), lambda k, h: (k, 0, 0)),
            pl.BlockSpec((1, CMP_BLOCK, HEAD_DIM, CMP_HIDDEN), lambda k, h: (k, 0, 0, 0)),
            pl.BlockSpec((1, CMP_HIDDEN, HEAD_DIM), lambda k, h: (k, 0, 0)),
            pl.BlockSpec((1, N_CMP_PAD, HEAD_DIM), lambda k, h: (k, 0, 0)),
            pl.BlockSpec((1, N_CMP_PAD, HEAD_DIM), lambda k, h: (k, 0, 0)),
        ],
        out_specs=[
            pl.BlockSpec((1, 1, N_CMP_PAD, HEAD_DIM), lambda k, h: (k, h, 0, 0)),
            pl.BlockSpec((1, 1, HEAD_DIM, N_CMP_PAD), lambda k, h: (k, h, 0, 0)),
        ],
        out_shape=[
            jax.ShapeDtypeStruct((2, N_KV_HEADS, N_CMP_PAD, HEAD_DIM), BF16),
            jax.ShapeDtypeStruct((2, N_KV_HEADS, HEAD_DIM, N_CMP_PAD), BF16),
        ],
        scratch_shapes=[pltpu.VMEM((N_CMP_PAD + 8, CMP_HIDDEN), F32)],
        compiler_params=pltpu.CompilerParams(
            dimension_semantics=("arbitrary", "arbitrary"), vmem_limit_bytes=VMEM_LIMIT),
        name="compress",
    )(proj, pe, w1, w2, cos_c, sin_c)


IMP_PAD = 8


def _attn_kernel(q_ref, kc_ref, vct_ref, ks_ref, vst_ref, kw_ref, vwt_ref, g_ref, gn_ref,
                 o_ref, imp_scr):
    i = pl.program_id(1)
    s0 = i * Q_BLOCK
    nrow = GROUP * Q_BLOCK

    q = q_ref[...]
    q_rows = jnp.concatenate([q[:, g * HEAD_DIM:(g + 1) * HEAD_DIM] for g in range(GROUP)], axis=0)
    tpos = s0 + (lax.broadcasted_iota(jnp.int32, (1, nrow), 1) & (Q_BLOCK - 1))

    sc = _nt_dot(kc_ref[0, 0], q_rows)
    cidx = lax.broadcasted_iota(jnp.int32, (N_CMP_PAD, 1), 0)
    cend = jnp.where(cidx < N_CMP, cidx * CMP_STRIDE + (CMP_BLOCK - 1), 2 * SEQ)
    cvalid = cend <= tpos
    sc = jnp.where(cvalid, sc, NEG)
    m_c = jnp.max(sc, axis=0, keepdims=True)
    e_c = jnp.where(cvalid, jnp.exp2(sc - m_c), 0.0)
    l_c = jnp.sum(e_c, axis=0, keepdims=True)
    p_c = e_c * (1.0 / jnp.where(l_c > 0.0, l_c, 1.0))
    o_ct = _dot(vct_ref[0, 0], p_c.astype(BF16))

    imp_c = (p_c[:, 0:128] + p_c[:, 128:256]) + (p_c[:, 256:384] + p_c[:, 384:512])
    imp_scr[0:IMP_PAD, :] = jnp.zeros((IMP_PAD, Q_BLOCK), F32)
    imp_scr[IMP_PAD:IMP_PAD + N_CMP_PAD, :] = imp_c
    imp_scr[IMP_PAD + N_CMP_PAD:IMP_PAD + N_CMP_PAD + 8, :] = jnp.zeros((8, Q_BLOCK), F32)
    ratio = SEL_BLOCK // CMP_STRIDE
    tap = lambda k: imp_scr[pl.ds(IMP_PAD + k, N_BLK, stride=ratio), :]
    imp = tap(-1) + 2.0 * (tap(0) + tap(1) + tap(2)) + tap(3)

    jidx = lax.broadcasted_iota(jnp.int32, (N_BLK, Q_BLOCK), 0).astype(F32)
    tok_l = lax.broadcasted_iota(jnp.int32, (1, Q_BLOCK), 1)
    jt = (2 * i + jnp.where(tok_l >= SEL_BLOCK, 1, 0)).astype(F32)
    v = jnp.where(jidx == 0.0, FORCE, imp)
    v = jnp.where(jidx == jt, FORCE, v)
    v = jnp.where(jidx == jt - 1.0, FORCE, v)
    v = jnp.where(jidx > jt, NEG, v)
    sel = jnp.zeros((N_BLK, Q_BLOCK), F32)
    for _ in range(N_SEL):
        mx = jnp.max(v, axis=0, keepdims=True)
        cand = jnp.where(v == mx, jidx, float(N_BLK))
        pick = jidx == jnp.min(cand, axis=0, keepdims=True)
        sel = jnp.where(pick, 1.0, sel)
        v = jnp.where(pick, REMOVED, v)
    pen = jnp.where(sel.T > 0.5, 0.0, NEG).astype(BF16)
    q_aug = jnp.concatenate([q_rows, jnp.concatenate([pen] * GROUP, axis=0)], axis=1)

    def sel_tile(kt, carry, causal):
        m, l, acc = carry
        k0 = pl.multiple_of(kt * SEL_TK, SEL_TK)
        s = _nt_dot(ks_ref[0, pl.ds(k0, SEL_TK), :], q_aug)
        if causal:
            kpos = k0 + lax.broadcasted_iota(jnp.int32, (SEL_TK, 1), 0)
            s = jnp.where(kpos <= tpos, s, NEG)
        m_new = jnp.maximum(m, jnp.max(s, axis=0, keepdims=True))
        alpha = jnp.exp2(m - m_new)
        p = jnp.exp2(s - m_new)
        l = alpha * l + jnp.sum(p, axis=0, keepdims=True)
        acc = alpha * acc + _dot(vst_ref[0, :, pl.ds(k0, SEL_TK)], p.astype(BF16))
        return m_new, l, acc

    n_full = i // (SEL_TK // Q_BLOCK)
    init = (jnp.full((1, nrow), NEG, F32), jnp.zeros((1, nrow), F32), jnp.zeros((HEAD_DIM, nrow), F32))
    carry = lax.fori_loop(0, n_full, lambda kt, c: sel_tile(kt, c, False), init)
    _, l_s, o_st = sel_tile(n_full, carry, True)

    w0 = pl.multiple_of(jnp.maximum(s0 - WINDOW, 0), Q_BLOCK)
    sw = _nt_dot(kw_ref[0, pl.ds(w0, WIN_KEYS), :], q_rows)
    diff = tpos - (w0 + lax.broadcasted_iota(jnp.int32, (WIN_KEYS, 1), 0))
    sw = jnp.where(diff >= 0, sw, NEG)
    sw = jnp.where(diff < WINDOW, sw, NEG)
    m_w = jnp.max(sw, axis=0, keepdims=True)
    p_w = jnp.exp2(sw - m_w)
    l_w = jnp.sum(p_w, axis=0, keepdims=True)
    o_wt = _dot(vwt_ref[0, :, pl.ds(w0, WIN_KEYS)], p_w.astype(BF16))

    gt = g_ref[...].T
    grow = lambda c: jnp.concatenate([gt[c * GROUP + g:c * GROUP + g + 1, :] for g in range(GROUP)], axis=1)
    out_t = o_ct * grow(0) + o_st * (grow(1) / l_s) + o_wt * (grow(2) / l_w)
    out = jnp.concatenate([out_t[:, g * Q_BLOCK:(g + 1) * Q_BLOCK].T for g in range(GROUP)], axis=1)
    gn = gn_ref[...]
    o_ref[...] = (out * (gn * jax.nn.sigmoid(gn))).astype(BF16)


def _attention(q_r, kc, vct, ks_aug, vst, kw_r, vwt, gates, proj):
    head_res = lambda shape: pl.BlockSpec((1,) + shape, lambda h, i: (h, 0, 0))
    return pl.pallas_call(
        _attn_kernel,
        grid=(N_KV_HEADS, N_QB),
        in_specs=[
            pl.BlockSpec((Q_BLOCK, GROUP * HEAD_DIM), lambda h, i: (i, h)),
            pl.BlockSpec((1, 1, N_CMP_PAD, HEAD_DIM), lambda h, i: (0, h, 0, 0)),
            pl.BlockSpec((1, 1, HEAD_DIM, N_CMP_PAD), lambda h, i: (1, h, 0, 0)),
            head_res((SEQ, 2 * HEAD_DIM)),
            head_res((HEAD_DIM, SEQ)),
            head_res((SEQ, HEAD_DIM)),
            head_res((HEAD_DIM, SEQ)),
            pl.BlockSpec((Q_BLOCK, 128), lambda h, i: (i, h)),
            pl.BlockSpec((Q_BLOCK, GROUP * HEAD_DIM), lambda h, i: (i, COL_GN // (GROUP * HEAD_DIM) + h)),
        ],
        out_specs=pl.BlockSpec((Q_BLOCK, GROUP * HEAD_DIM), lambda h, i: (i, h)),
        out_shape=jax.ShapeDtypeStruct((SEQ, NSA_WIDTH), BF16),
        scratch_shapes=[pltpu.VMEM((IMP_PAD + N_CMP_PAD + 8, Q_BLOCK), F32)],
        compiler_params=pltpu.CompilerParams(
            dimension_semantics=("arbitrary", "arbitrary"), vmem_limit_bytes=VMEM_LIMIT),
        name="attn",
    )(q_r, kc, vct, ks_aug, vst, kw_r, vwt, gates, proj)


def _pool_kernel(u_ref, up_ref, gp_ref, mix_ref, sc_ref, wpo_ref, ya_ref, ext_scr):
    i = pl.program_id(0)
    tm = POOL_TM
    ext_scr[0:POOL_HALO, :] = jnp.where(i > 0, up_ref[...], 0.0)
    ext_scr[POOL_HALO:POOL_HALO + tm, :] = u_ref[...]
    t1 = i * tm + lax.broadcasted_iota(jnp.int32, (tm, 1), 0) + 1
    zs = []
    for g, w in enumerate(POOL_WINDOWS):
        cs = slice(g * POOL_GROUP, (g + 1) * POOL_GROUP)
        u = ext_scr[POOL_HALO:POOL_HALO + tm, cs]
        acc = u
        for k in range(1, w):
            acc = acc + ext_scr[POOL_HALO - k:POOL_HALO - k + tm, cs]
        cnt = jnp.minimum(t1, w).astype(F32)
        pooled = acc / cnt - u
        zs.append(_dot(pooled.astype(BF16), mix_ref[g]))
    z = jnp.concatenate(zs, axis=1) * sc_ref[...]
    gp = gp_ref[...]
    y = z * (gp * jax.nn.sigmoid(gp))
    ya_ref[...] = _dot(y.astype(BF16), wpo_ref[...])


def _pool(proj, mix, scale, w_pool_out):
    tm = POOL_TM
    return pl.pallas_call(
        _pool_kernel,
        grid=(SEQ // tm,),
        in_specs=[
            pl.BlockSpec((tm, POOL_WIDTH), lambda i: (i, COL_U // POOL_WIDTH)),
            pl.BlockSpec((POOL_HALO, POOL_WIDTH),
                         lambda i: (jnp.maximum(i * (tm // POOL_HALO) - 1, 0), COL_U // POOL_WIDTH)),
            pl.BlockSpec((tm, POOL_WIDTH), lambda i: (i, COL_GP // POOL_WIDTH)),
            pl.BlockSpec((len(POOL_WINDOWS), POOL_GROUP, POOL_GROUP), lambda i: (0, 0, 0)),
            pl.BlockSpec((1, POOL_WIDTH), lambda i: (0, 0)),
            pl.BlockSpec((POOL_WIDTH, D_MODEL), lambda i: (0, 0)),
        ],
        out_specs=pl.BlockSpec((tm, D_MODEL), lambda i: (i, 0)),
        out_shape=jax.ShapeDtypeStruct((SEQ, D_MODEL), F32),
        scratch_shapes=[pltpu.VMEM((POOL_HALO + tm, POOL_WIDTH), F32)],
        compiler_params=pltpu.CompilerParams(
            dimension_semantics=("arbitrary",), vmem_limit_bytes=VMEM_LIMIT),
        name="pool",
    )(proj, proj, proj, mix, scale, w_pool_out)


def _out_kernel(x_ref, ya_ref, o_ref, gm0_ref, gm1_ref, wn_ref, wo_ref, fw_ref, out_ref):
    y_b = _dot(o_ref[...], wn_ref[...])
    merged = gm0_ref[...] * ya_ref[...] + gm1_ref[...] * y_b
    r = x_ref[...] + _dot(merged.astype(BF16), wo_ref[...])
    ms = jnp.mean(r * r, axis=-1, keepdims=True)
    out_ref[...] = r * lax.rsqrt(ms + EPS) * fw_ref[...]


def _output(x2, y_a, o_gated, proj, w_nsa_out, w_out, final_norm_w):
    tm = OUT_TM
    row = lambda: pl.BlockSpec((tm, D_MODEL), lambda i: (i, 0))
    resident = lambda: pl.BlockSpec((D_MODEL, D_MODEL), lambda i: (0, 0), pipeline_mode=pl.Buffered(1))
    return pl.pallas_call(
        _out_kernel,
        grid=(SEQ // tm,),
        in_specs=[
            row(), row(), row(),
            pl.BlockSpec((tm, D_MODEL), lambda i: (i, 0)),
            pl.BlockSpec((tm, D_MODEL), lambda i: (i, 1)),
            resident(), resident(),
            pl.BlockSpec((1, D_MODEL), lambda i: (0, 0)),
        ],
        out_specs=row(),
        out_shape=jax.ShapeDtypeStruct((SEQ, D_MODEL), F32),
        compiler_params=pltpu.CompilerParams(
            dimension_semantics=("arbitrary",), vmem_limit_bytes=VMEM_LIMIT),
        name="out",
    )(x2, y_a, o_gated, proj, proj, w_nsa_out, w_out, final_norm_w)


def _rope_tables():
    half = HEAD_DIM // 2
    inv = ROPE_THETA ** (-jnp.arange(half, dtype=F32) / half)
    ang = jnp.arange(SEQ).astype(F32)[:, None] * inv[None, :]
    cos, sin = jnp.cos(ang), jnp.sin(ang)
    return jnp.concatenate([cos, cos], axis=-1), jnp.concatenate([-sin, sin], axis=-1)


def _gate_columns():
    src = np.zeros((N_KV_HEADS * 128,), np.int32)
    valid = np.zeros((N_KV_HEADS * 128,), bool)
    for h in range(N_KV_HEADS):
        for g in range(GROUP):
            for c in range(3):
                src[h * 128 + c * GROUP + g] = W_IN_MAIN + (h * GROUP + g) * 3 + c
                valid[h * 128 + c * GROUP + g] = True
    return src, valid


def kernel(x, norm_w, w_in, pool_mix, pool_scale, cmp_pe_k, cmp_w1_k, cmp_w2_k, cmp_pe_v, cmp_w1_v,
           cmp_w2_v, w_pool_out, w_nsa_out, w_merge, b_merge, w_out, final_norm_w):
    assert x.shape == (1, SEQ, D_MODEL) and norm_w.shape[0] == 1
    x2 = x[0]
    w_cat = jnp.concatenate([w_merge[0], w_in[0, :, :W_IN_MAIN]], axis=1).astype(BF16)
    src, valid = _gate_columns()
    w_gate = jnp.where(valid[None, :], w_in[0][:, src], 0.0).astype(BF16)
    proj, gates = _proj(x2, norm_w, w_cat, w_gate, b_merge)

    cos_tab, sin_tab = _rope_tables()
    q_r, ks_aug, vst, kw_r, vwt = _prep(proj, cos_tab, sin_tab)

    pad = lambda t: jnp.pad(t[CMP_BLOCK - 1::CMP_STRIDE], ((0, N_CMP_PAD - N_CMP), (0, 0)))
    cos_c = jnp.stack([pad(cos_tab), jnp.ones((N_CMP_PAD, HEAD_DIM), F32)])
    sin_c = jnp.stack([pad(sin_tab), jnp.zeros((N_CMP_PAD, HEAD_DIM), F32)])
    pe = jnp.stack([cmp_pe_k[0], cmp_pe_v[0]])
    w1 = jnp.stack([cmp_w1_k[0], cmp_w1_v[0]]).astype(BF16)
    w2 = jnp.stack([cmp_w2_k[0], cmp_w2_v[0]]).astype(BF16)
    cmp_n, cmp_t = _compress(proj, pe, w1, w2, cos_c, sin_c)

    o_gated = _attention(q_r, cmp_n, cmp_t, ks_aug, vst, kw_r, vwt, gates, proj)

    y_a = _pool(proj, pool_mix[0].astype(BF16), pool_scale, w_pool_out[0].astype(BF16))
    out = _output(x2, y_a, o_gated, proj, w_nsa_out[0].astype(BF16), w_out[0].astype(BF16),
                  final_norm_w[None, :])
    return out[None]
```

```python
import functools
import math

import numpy as np
import jax
import jax.numpy as jnp
from jax import lax
from jax.experimental import pallas as pl
from jax.experimental.pallas import tpu as pltpu

F32 = jnp.float32
BF16 = jnp.bfloat16

D_MODEL = 2048
SEQ = 8192
POOL_WIDTH = 1024
POOL_WINDOWS = (2, 4, 8, 16)
POOL_GROUP = 256
POOL_HALO = 16
N_Q_HEADS = 16
N_KV_HEADS = 4
GROUP = 4
HEAD_DIM = 128
NSA_WIDTH = 2048
KV_WIDTH = 512
CMP_BLOCK = 32
CMP_STRIDE = 16
CMP_HIDDEN = 256
N_CMP = (SEQ - CMP_BLOCK) // CMP_STRIDE + 1
N_CMP_PAD = 512
SEL_BLOCK = 64
N_BLK = SEQ // SEL_BLOCK
N_SEL = 16
WINDOW = 512
Q_BLOCK = 128
N_QB = SEQ // Q_BLOCK
ROPE_THETA = 10000.0
EPS = 1e-6
NEG = -1e30
FORCE = 1e6
REMOVED = -3e38
LOG2E = 1.4426950408889634
Q_SCALE = HEAD_DIM ** -0.5 * LOG2E

COL_GM = 0
COL_U = 4096
COL_GP = 5120
COL_Q = 6144
COL_KC = 8192
COL_VC = 8704
COL_KS = 9216
COL_VS = 9728
COL_KW = 10240
COL_VW = 10752
COL_GN = 11264
N_PROJ = 13312
W_IN_MAIN = 9216

VMEM_LIMIT = 56 * 1024 * 1024

PROJ_TM = 512
PROJ_TN = 1024
N_GM_TILES = 4096 // PROJ_TN
PREP_TM = 512
SEL_TK = 512
WIN_KEYS = WINDOW + Q_BLOCK
V_ROWS = HEAD_DIM + 16
POOL_TM = 512
OUT_TM = 256


def _nt_dot(a, b):
    return lax.dot_general(a, b, (((1,), (1,)), ((), ())), preferred_element_type=F32)


def _dot(a, b):
    return jnp.dot(a, b, preferred_element_type=F32)


def _proj_kernel(x_ref, nw_ref, w_ref, wg_ref, b_ref, o_ref, g_ref, h_scr):
    j = pl.program_id(1)

    @pl.when(j == 0)
    def _():
        x = x_ref[...]
        ms = jnp.mean(x * x, axis=-1, keepdims=True)
        hb = (x * lax.rsqrt(ms + EPS) * nw_ref[...]).astype(BF16)
        h_scr[...] = hb
        g_ref[...] = jax.nn.sigmoid(_dot(hb, wg_ref[...]))

    acc = _dot(h_scr[...], w_ref[...])

    @pl.when(j < N_GM_TILES)
    def _():
        o_ref[...] = jax.nn.sigmoid(acc + b_ref[...])

    @pl.when(j >= N_GM_TILES)
    def _():
        o_ref[...] = acc


def _proj(x2, norm_w, w_cat, w_gate, b_merge):
    grid = (SEQ // PROJ_TM, N_PROJ // PROJ_TN)
    return pl.pallas_call(
        _proj_kernel,
        grid=grid,
        in_specs=[
            pl.BlockSpec((PROJ_TM, D_MODEL), lambda i, j: (i, 0)),
            pl.BlockSpec((1, D_MODEL), lambda i, j: (0, 0)),
            pl.BlockSpec((D_MODEL, PROJ_TN), lambda i, j: (0, j)),
            pl.BlockSpec((D_MODEL, N_KV_HEADS * 128), lambda i, j: (0, 0)),
            pl.BlockSpec((1, PROJ_TN), lambda i, j: (0, jnp.minimum(j, N_GM_TILES - 1))),
        ],
        out_specs=[
            pl.BlockSpec((PROJ_TM, PROJ_TN), lambda i, j: (i, j)),
            pl.BlockSpec((PROJ_TM, N_KV_HEADS * 128), lambda i, j: (i, 0)),
        ],
        out_shape=[
            jax.ShapeDtypeStruct((SEQ, N_PROJ), F32),
            jax.ShapeDtypeStruct((SEQ, N_KV_HEADS * 128), F32),
        ],
        scratch_shapes=[pltpu.VMEM((PROJ_TM, D_MODEL), BF16)],
        compiler_params=pltpu.CompilerParams(
            dimension_semantics=("arbitrary", "arbitrary"), vmem_limit_bytes=VMEM_LIMIT),
        name="proj",
    )(x2, norm_w, w_cat, w_gate, b_merge)


def _rope(x, cos, sin_signed):
    return x * cos + pltpu.roll(x, HEAD_DIM // 2, 1) * sin_signed


def _prep_kernel(q_ref, ks_ref, vs_ref, kw_ref, vw_ref, cos_ref, sin_ref,
                 qo_ref, kso_ref, vso_ref, kwo_ref, vwo_ref):
    i = pl.program_id(0)
    cos = cos_ref[...]
    sin = sin_ref[...]
    for hq in range(N_Q_HEADS):
        sl = slice(hq * HEAD_DIM, (hq + 1) * HEAD_DIM)
        qo_ref[:, sl] = (_rope(q_ref[:, sl], cos, sin) * Q_SCALE).astype(BF16)
    blk = (i * PREP_TM + lax.broadcasted_iota(jnp.int32, (PREP_TM, N_BLK), 0)) // SEL_BLOCK
    onehot = jnp.where(blk == lax.broadcasted_iota(jnp.int32, (PREP_TM, N_BLK), 1), 1.0, 0.0).astype(BF16)
    ones_rows = jnp.where(lax.broadcasted_iota(jnp.int32, (V_ROWS - HEAD_DIM, PREP_TM), 0) == 0,
                          1.0, 0.0).astype(BF16)
    for h in range(N_KV_HEADS):
        sl = slice(h * HEAD_DIM, (h + 1) * HEAD_DIM)
        kso_ref[h, :, 0:HEAD_DIM] = _rope(ks_ref[:, sl], cos, sin).astype(BF16)
        kso_ref[h, :, HEAD_DIM:2 * HEAD_DIM] = onehot
        kwo_ref[h] = _rope(kw_ref[:, sl], cos, sin).astype(BF16)
        vso_ref[h, 0:HEAD_DIM, :] = vs_ref[:, sl].T.astype(BF16)
        vso_ref[h, HEAD_DIM:V_ROWS, :] = ones_rows
        vwo_ref[h] = vw_ref[:, sl].T.astype(BF16)


def _prep(proj, cos_tab, sin_tab):
    tm = PREP_TM
    kv_spec = lambda col: pl.BlockSpec((tm, KV_WIDTH), lambda i: (i, col // KV_WIDTH))
    return pl.pallas_call(
        _prep_kernel,
        grid=(SEQ // tm,),
        in_specs=[
            pl.BlockSpec((tm, NSA_WIDTH), lambda i: (i, COL_Q // NSA_WIDTH)),
            kv_spec(COL_KS), kv_spec(COL_VS), kv_spec(COL_KW), kv_spec(COL_VW),
            pl.BlockSpec((tm, HEAD_DIM), lambda i: (i, 0)),
            pl.BlockSpec((tm, HEAD_DIM), lambda i: (i, 0)),
        ],
        out_specs=[
            pl.BlockSpec((tm, NSA_WIDTH), lambda i: (i, 0)),
            pl.BlockSpec((N_KV_HEADS, tm, 2 * HEAD_DIM), lambda i: (0, i, 0)),
            pl.BlockSpec((N_KV_HEADS, V_ROWS, tm), lambda i: (0, 0, i)),
            pl.BlockSpec((N_KV_HEADS, tm, HEAD_DIM), lambda i: (0, i, 0)),
            pl.BlockSpec((N_KV_HEADS, HEAD_DIM, tm), lambda i: (0, 0, i)),
        ],
        out_shape=[
            jax.ShapeDtypeStruct((SEQ, NSA_WIDTH), BF16),
            jax.ShapeDtypeStruct((N_KV_HEADS, SEQ, 2 * HEAD_DIM), BF16),
            jax.ShapeDtypeStruct((N_KV_HEADS, V_ROWS, SEQ), BF16),
            jax.ShapeDtypeStruct((N_KV_HEADS, SEQ, HEAD_DIM), BF16),
            jax.ShapeDtypeStruct((N_KV_HEADS, HEAD_DIM, SEQ), BF16),
        ],
        compiler_params=pltpu.CompilerParams(
            dimension_semantics=("arbitrary",), vmem_limit_bytes=VMEM_LIMIT),
        name="prep",
    )(proj, proj, proj, proj, proj, cos_tab, sin_tab)


def _cmp_kernel(x_ref, pe_ref, w1_ref, w2_ref, cos_ref, sin_ref, o_ref, ot_ref, hi_scr):
    half = CMP_BLOCK // 2
    acc_lo = jnp.zeros((N_CMP_PAD, CMP_HIDDEN), F32)
    acc_hi = jnp.zeros((N_CMP_PAD, CMP_HIDDEN), F32)
    for l in range(half):
        xl = x_ref[pl.ds(l, N_CMP_PAD, stride=CMP_STRIDE), :]
        acc_lo += _dot((xl + pe_ref[0, l:l + 1, :]).astype(BF16), w1_ref[0, l])
        acc_hi += _dot((xl + pe_ref[0, half + l:half + l + 1, :]).astype(BF16), w1_ref[0, half + l])
    hi_scr[0:N_CMP_PAD, :] = acc_hi
    hi_scr[N_CMP_PAD:N_CMP_PAD + 8, :] = jnp.zeros((8, CMP_HIDDEN), F32)
    pre = acc_lo + hi_scr[pl.ds(1, N_CMP_PAD), :]
    hdn = pre * jax.nn.sigmoid(pre)
    out = _dot(hdn.astype(BF16), w2_ref[0])
    out = _rope(out, cos_ref[0], sin_ref[0])
    row = lax.broadcasted_iota(jnp.int32, (N_CMP_PAD, HEAD_DIM), 0)
    out = jnp.where(row < N_CMP, out, 0.0)
    o_ref[0, 0] = out.astype(BF16)
    ot_ref[0, 0] = out.T.astype(BF16)


def _compress(proj, pe, w1, w2, cos_c, sin_c):
    return pl.pallas_call(
        _cmp_kernel,
        grid=(2, N_KV_HEADS),
        in_specs=[
            pl.BlockSpec((SEQ, HEAD_DIM), lambda k, h: (0, COL_KC // HEAD_DIM + N_KV_HEADS * k + h)),
            pl.BlockSpec((1, CMP_BLOCK, HEAD_DIM), lambda k, h: (k, 0, 0)),
            pl.BlockSpec((1, CMP_BLOCK, HEAD_DIM, CMP_HIDDEN), lambda k, h: (k, 0, 0, 0)),
            pl.BlockSpec((1, CMP_HIDDEN, HEAD_DIM), lambda k, h: (k, 0, 0)),
            pl.BlockSpec((1, N_CMP_PAD, HEAD_DIM), lambda k, h: (k, 0, 0)),
            pl.BlockSpec((1, N_CMP_PAD, HEAD_DIM), lambda k, h: (k, 0, 0)),
        ],
        out_specs=[
            pl.BlockSpec((1, 1, N_CMP_PAD, HEAD_DIM), lambda k, h: (k, h, 0, 0)),
            pl.BlockSpec((1, 1, HEAD_DIM, N_CMP_PAD), lambda k, h: (k, h, 0, 0)),
        ],
        out_shape=[
            jax.ShapeDtypeStruct((2, N_KV_HEADS, N_CMP_PAD, HEAD_DIM), BF16),
            jax.ShapeDtypeStruct((2, N_KV_HEADS, HEAD_DIM, N_CMP_PAD), BF16),
        ],
        scratch_shapes=[pltpu.VMEM((N_CMP_PAD + 8, CMP_HIDDEN), F32)],
        compiler_params=pltpu.CompilerParams(
            dimension_semantics=("arbitrary", "arbitrary"), vmem_limit_bytes=VMEM_LIMIT),
        name="compress",
    )(proj, pe, w1, w2, cos_c, sin_c)


IMP_PAD = 8


def _attn_kernel(q_ref, kc_ref, vct_ref, ks_ref, vst_ref, kw_ref, vwt_ref, g_ref, gn_ref,
                 o_ref, imp_scr, s_scr, acc_scr):
    i = pl.program_id(1)
    s0 = i * Q_BLOCK
    nrow = GROUP * Q_BLOCK

    q = q_ref[...]
    q_rows = jnp.concatenate([q[:, g * HEAD_DIM:(g + 1) * HEAD_DIM] for g in range(GROUP)], axis=0)
    tpos = s0 + (lax.broadcasted_iota(jnp.int32, (1, nrow), 1) & (Q_BLOCK - 1))

    sc = _nt_dot(kc_ref[0, 0], q_rows)
    cidx = lax.broadcasted_iota(jnp.int32, (N_CMP_PAD, 1), 0)
    cend = jnp.where(cidx < N_CMP, cidx * CMP_STRIDE + (CMP_BLOCK - 1), 2 * SEQ)
    cvalid = cend <= tpos
    sc = jnp.where(cvalid, sc, NEG)
    m_c = jnp.max(sc, axis=0, keepdims=True)
    e_c = jnp.where(cvalid, jnp.exp2(sc - m_c), 0.0)
    l_c = jnp.sum(e_c, axis=0, keepdims=True)
    p_c = e_c * (1.0 / jnp.where(l_c > 0.0, l_c, 1.0))
    o_ct = _dot(vct_ref[0, 0], p_c.astype(BF16))

    imp_c = (p_c[:, 0:128] + p_c[:, 128:256]) + (p_c[:, 256:384] + p_c[:, 384:512])
    imp_scr[0:IMP_PAD, :] = jnp.zeros((IMP_PAD, Q_BLOCK), F32)
    imp_scr[IMP_PAD:IMP_PAD + N_CMP_PAD, :] = imp_c
    imp_scr[IMP_PAD + N_CMP_PAD:IMP_PAD + N_CMP_PAD + 8, :] = jnp.zeros((8, Q_BLOCK), F32)
    ratio = SEL_BLOCK // CMP_STRIDE
    tap = lambda k: imp_scr[pl.ds(IMP_PAD + k, N_BLK, stride=ratio), :]
    imp = tap(-1) + 2.0 * (tap(0) + tap(1) + tap(2)) + tap(3)

    jidx = lax.broadcasted_iota(jnp.int32, (N_BLK, Q_BLOCK), 0).astype(F32)
    tok_l = lax.broadcasted_iota(jnp.int32, (1, Q_BLOCK), 1)
    jt = (2 * i + jnp.where(tok_l >= SEL_BLOCK, 1, 0)).astype(F32)
    v = jnp.where(jidx == 0.0, FORCE, imp)
    v = jnp.where(jidx == jt, FORCE, v)
    v = jnp.where(jidx == jt - 1.0, FORCE, v)
    v = jnp.where(jidx > jt, NEG, v)
    sel = jnp.zeros((N_BLK, Q_BLOCK), F32)
    for _ in range(N_SEL):
        mx = jnp.max(v, axis=0, keepdims=True)
        cand = jnp.where(v == mx, jidx, float(N_BLK))
        pick = jidx == jnp.min(cand, axis=0, keepdims=True)
        sel = jnp.where(pick, 1.0, sel)
        v = jnp.where(pick, REMOVED, v)
    pen = jnp.where(sel.T > 0.5, 0.0, NEG).astype(BF16)
    q_aug = jnp.concatenate([q_rows, jnp.concatenate([pen] * GROUP, axis=0)], axis=1)

    def key_start(kt):
        return pl.multiple_of(kt * SEL_TK, SEL_TK)

    def scores(kt):
        return _nt_dot(ks_ref[0, pl.ds(key_start(kt), SEL_TK), :], q_aug)

    def consume(kt, s, m, m_tile):
        m_new = jnp.maximum(m, m_tile)
        alpha = jnp.exp2(m - m_new)
        p = jnp.exp2(s - m_new).astype(BF16)
        pv = _dot(vst_ref[0, :, pl.ds(key_start(kt), SEL_TK)], p)
        acc_scr[...] = alpha * acc_scr[...] + pv
        return m_new

    def body(kt, carry):
        m, m_tile = carry
        s_cur = s_scr[...]
        s_nxt = scores(kt + 1)
        m_nxt = jnp.max(s_nxt, axis=0, keepdims=True)
        m_new = consume(kt, s_cur, m, m_tile)
        s_scr[...] = s_nxt
        return m_new, m_nxt

    n_full = i // (SEL_TK // Q_BLOCK)
    s_first = scores(0)
    s_scr[...] = s_first
    acc_scr[...] = jnp.zeros((V_ROWS, nrow), F32)
    init = (jnp.full((1, nrow), NEG, F32), jnp.max(s_first, axis=0, keepdims=True))
    m_s, _ = lax.fori_loop(0, n_full, body, init)
    kpos = key_start(n_full) + lax.broadcasted_iota(jnp.int32, (SEL_TK, 1), 0)
    s_diag = jnp.where(kpos <= tpos, s_scr[...], NEG)
    consume(n_full, s_diag, m_s, jnp.max(s_diag, axis=0, keepdims=True))
    o_st = acc_scr[0:HEAD_DIM, :]
    l_s = acc_scr[HEAD_DIM:HEAD_DIM + 1, :]

    w0 = pl.multiple_of(jnp.maximum(s0 - WINDOW, 0), Q_BLOCK)
    sw = _nt_dot(kw_ref[0, pl.ds(w0, WIN_KEYS), :], q_rows)
    diff = tpos - (w0 + lax.broadcasted_iota(jnp.int32, (WIN_KEYS, 1), 0))
    sw = jnp.where(diff >= 0, sw, NEG)
    sw = jnp.where(diff < WINDOW, sw, NEG)
    m_w = jnp.max(sw, axis=0, keepdims=True)
    p_w = jnp.exp2(sw - m_w)
    l_w = jnp.sum(p_w, axis=0, keepdims=True)
    o_wt = _dot(vwt_ref[0, :, pl.ds(w0, WIN_KEYS)], p_w.astype(BF16))

    gt = g_ref[...].T
    grow = lambda c: jnp.concatenate([gt[c * GROUP + g:c * GROUP + g + 1, :] for g in range(GROUP)], axis=1)
    out_t = o_ct * grow(0) + o_st * (grow(1) / l_s) + o_wt * (grow(2) / l_w)
    out = jnp.concatenate([out_t[:, g * Q_BLOCK:(g + 1) * Q_BLOCK].T for g in range(GROUP)], axis=1)
    gn = gn_ref[...]
    o_ref[...] = (out * (gn * jax.nn.sigmoid(gn))).astype(BF16)


def _attention(q_r, kc, vct, ks_aug, vst, kw_r, vwt, gates, proj):
    head_res = lambda shape: pl.BlockSpec((1,) + shape, lambda h, i: (h, 0, 0))
    return pl.pallas_call(
        _attn_kernel,
        grid=(N_KV_HEADS, N_QB),
        in_specs=[
            pl.BlockSpec((Q_BLOCK, GROUP * HEAD_DIM), lambda h, i: (i, h)),
            pl.BlockSpec((1, 1, N_CMP_PAD, HEAD_DIM), lambda h, i: (0, h, 0, 0)),
            pl.BlockSpec((1, 1, HEAD_DIM, N_CMP_PAD), lambda h, i: (1, h, 0, 0)),
            head_res((SEQ, 2 * HEAD_DIM)),
            head_res((V_ROWS, SEQ)),
            head_res((SEQ, HEAD_DIM)),
            head_res((HEAD_DIM, SEQ)),
            pl.BlockSpec((Q_BLOCK, 128), lambda h, i: (i, h)),
            pl.BlockSpec((Q_BLOCK, GROUP * HEAD_DIM), lambda h, i: (i, COL_GN // (GROUP * HEAD_DIM) + h)),
        ],
        out_specs=pl.BlockSpec((Q_BLOCK, GROUP * HEAD_DIM), lambda h, i: (i, h)),
        out_shape=jax.ShapeDtypeStruct((SEQ, NSA_WIDTH), BF16),
        scratch_shapes=[pltpu.VMEM((IMP_PAD + N_CMP_PAD + 8, Q_BLOCK), F32),
                        pltpu.VMEM((SEL_TK, GROUP * Q_BLOCK), F32),
                        pltpu.VMEM((V_ROWS, GROUP * Q_BLOCK), F32)],
        compiler_params=pltpu.CompilerParams(
            dimension_semantics=("arbitrary", "arbitrary"), vmem_limit_bytes=VMEM_LIMIT),
        name="attn",
    )(q_r, kc, vct, ks_aug, vst, kw_r, vwt, gates, proj)


def _pool_kernel(u_ref, up_ref, gp_ref, mix_ref, sc_ref, wpo_ref, ya_ref, ext_scr):
    i = pl.program_id(0)
    tm = POOL_TM
    ext_scr[0:POOL_HALO, :] = jnp.where(i > 0, up_ref[...], 0.0)
    ext_scr[POOL_HALO:POOL_HALO + tm, :] = u_ref[...]
    t1 = i * tm + lax.broadcasted_iota(jnp.int32, (tm, 1), 0) + 1
    zs = []
    for g, w in enumerate(POOL_WINDOWS):
        cs = slice(g * POOL_GROUP, (g + 1) * POOL_GROUP)
        u = ext_scr[POOL_HALO:POOL_HALO + tm, cs]
        acc = u
        for k in range(1, w):
            acc = acc + ext_scr[POOL_HALO - k:POOL_HALO - k + tm, cs]
        cnt = jnp.minimum(t1, w).astype(F32)
        pooled = acc / cnt - u
        zs.append(_dot(pooled.astype(BF16), mix_ref[g]))
    z = jnp.concatenate(zs, axis=1) * sc_ref[...]
    gp = gp_ref[...]
    y = z * (gp * jax.nn.sigmoid(gp))
    ya_ref[...] = _dot(y.astype(BF16), wpo_ref[...])


def _pool(proj, mix, scale, w_pool_out):
    tm = POOL_TM
    return pl.pallas_call(
        _pool_kernel,
        grid=(SEQ // tm,),
        in_specs=[
            pl.BlockSpec((tm, POOL_WIDTH), lambda i: (i, COL_U // POOL_WIDTH)),
            pl.BlockSpec((POOL_HALO, POOL_WIDTH),
                         lambda i: (jnp.maximum(i * (tm // POOL_HALO) - 1, 0), COL_U // POOL_WIDTH)),
            pl.BlockSpec((tm, POOL_WIDTH), lambda i: (i, COL_GP // POOL_WIDTH)),
            pl.BlockSpec((len(POOL_WINDOWS), POOL_GROUP, POOL_GROUP), lambda i: (0, 0, 0)),
            pl.BlockSpec((1, POOL_WIDTH), lambda i: (0, 0)),
            pl.BlockSpec((POOL_WIDTH, D_MODEL), lambda i: (0, 0)),
        ],
        out_specs=pl.BlockSpec((tm, D_MODEL), lambda i: (i, 0)),
        out_shape=jax.ShapeDtypeStruct((SEQ, D_MODEL), F32),
        scratch_shapes=[pltpu.VMEM((POOL_HALO + tm, POOL_WIDTH), F32)],
        compiler_params=pltpu.CompilerParams(
            dimension_semantics=("arbitrary",), vmem_limit_bytes=VMEM_LIMIT),
        name="pool",
    )(proj, proj, proj, mix, scale, w_pool_out)


def _out_kernel(x_ref, ya_ref, o_ref, gm0_ref, gm1_ref, wn_ref, wo_ref, fw_ref, out_ref):
    y_b = _dot(o_ref[...], wn_ref[...])
    merged = gm0_ref[...] * ya_ref[...] + gm1_ref[...] * y_b
    r = x_ref[...] + _dot(merged.astype(BF16), wo_ref[...])
    ms = jnp.mean(r * r, axis=-1, keepdims=True)
    out_ref[...] = r * lax.rsqrt(ms + EPS) * fw_ref[...]


def _output(x2, y_a, o_gated, proj, w_nsa_out, w_out, final_norm_w):
    tm = OUT_TM
    row = lambda: pl.BlockSpec((tm, D_MODEL), lambda i: (i, 0))
    resident = lambda: pl.BlockSpec((D_MODEL, D_MODEL), lambda i: (0, 0), pipeline_mode=pl.Buffered(1))
    return pl.pallas_call(
        _out_kernel,
        grid=(SEQ // tm,),
        in_specs=[
            row(), row(), row(),
            pl.BlockSpec((tm, D_MODEL), lambda i: (i, 0)),
            pl.BlockSpec((tm, D_MODEL), lambda i: (i, 1)),
            resident(), resident(),
            pl.BlockSpec((1, D_MODEL), lambda i: (0, 0)),
        ],
        out_specs=row(),
        out_shape=jax.ShapeDtypeStruct((SEQ, D_MODEL), F32),
        compiler_params=pltpu.CompilerParams(
            dimension_semantics=("arbitrary",), vmem_limit_bytes=VMEM_LIMIT),
        name="out",
    )(x2, y_a, o_gated, proj, proj, w_nsa_out, w_out, final_norm_w)


def _rope_tables():
    half = HEAD_DIM // 2
    inv = ROPE_THETA ** (-jnp.arange(half, dtype=F32) / half)
    ang = jnp.arange(SEQ).astype(F32)[:, None] * inv[None, :]
    cos, sin = jnp.cos(ang), jnp.sin(ang)
    return jnp.concatenate([cos, cos], axis=-1), jnp.concatenate([-sin, sin], axis=-1)


def _gate_columns():
    src = np.zeros((N_KV_HEADS * 128,), np.int32)
    valid = np.zeros((N_KV_HEADS * 128,), bool)
    for h in range(N_KV_HEADS):
        for g in range(GROUP):
            for c in range(3):
                src[h * 128 + c * GROUP + g] = W_IN_MAIN + (h * GROUP + g) * 3 + c
                valid[h * 128 + c * GROUP + g] = True
    return src, valid


def kernel(x, norm_w, w_in, pool_mix, pool_scale, cmp_pe_k, cmp_w1_k, cmp_w2_k, cmp_pe_v, cmp_w1_v,
           cmp_w2_v, w_pool_out, w_nsa_out, w_merge, b_merge, w_out, final_norm_w):
    assert x.shape == (1, SEQ, D_MODEL) and norm_w.shape[0] == 1
    x2 = x[0]
    w_cat = jnp.concatenate([w_merge[0], w_in[0, :, :W_IN_MAIN]], axis=1).astype(BF16)
    src, valid = _gate_columns()
    w_gate = jnp.where(valid[None, :], w_in[0][:, src], 0.0).astype(BF16)
    proj, gates = _proj(x2, norm_w, w_cat, w_gate, b_merge)

    cos_tab, sin_tab = _rope_tables()
    q_r, ks_aug, vst, kw_r, vwt = _prep(proj, cos_tab, sin_tab)

    pad = lambda t: jnp.pad(t[CMP_BLOCK - 1::CMP_STRIDE], ((0, N_CMP_PAD - N_CMP), (0, 0)))
    cos_c = jnp.stack([pad(cos_tab), jnp.ones((N_CMP_PAD, HEAD_DIM), F32)])
    sin_c = jnp.stack([pad(sin_tab), jnp.zeros((N_CMP_PAD, HEAD_DIM), F32)])
    pe = jnp.stack([cmp_pe_k[0], cmp_pe_v[0]])
    w1 = jnp.stack([cmp_w1_k[0], cmp_w1_v[0]]).astype(BF16)
    w2 = jnp.stack([cmp_w2_k[0], cmp_w2_v[0]]).astype(BF16)
    cmp_n, cmp_t = _compress(proj, pe, w1, w2, cos_c, sin_c)

    o_gated = _attention(q_r, cmp_n, cmp_t, ks_aug, vst, kw_r, vwt, gates, proj)

    y_a = _pool(proj, pool_mix[0].astype(BF16), pool_scale, w_pool_out[0].astype(BF16))
    out = _output(x2, y_a, o_gated, proj, w_nsa_out[0].astype(BF16), w_out[0].astype(BF16),
                  final_norm_w[None, :])
    return out[None]
```

```python
import functools
import math

import numpy as np
import jax
import jax.numpy as jnp
from jax import lax
from jax.experimental import pallas as pl
from jax.experimental.pallas import tpu as pltpu

F32 = jnp.float32
BF16 = jnp.bfloat16

D_MODEL = 2048
SEQ = 8192
POOL_WIDTH = 1024
POOL_WINDOWS = (2, 4, 8, 16)
POOL_GROUP = 256
POOL_HALO = 16
N_Q_HEADS = 16
N_KV_HEADS = 4
GROUP = 4
HEAD_DIM = 128
NSA_WIDTH = 2048
KV_WIDTH = 512
CMP_BLOCK = 32
CMP_STRIDE = 16
CMP_HIDDEN = 256
N_CMP = (SEQ - CMP_BLOCK) // CMP_STRIDE + 1
N_CMP_PAD = 512
SEL_BLOCK = 64
N_BLK = SEQ // SEL_BLOCK
N_SEL = 16
WINDOW = 512
Q_BLOCK = 128
N_QB = SEQ // Q_BLOCK
ROPE_THETA = 10000.0
EPS = 1e-6
NEG = -1e30
FORCE = 1e6
REMOVED = -3e38
LOG2E = 1.4426950408889634
Q_SCALE = HEAD_DIM ** -0.5 * LOG2E

COL_GM = 0
COL_U = 4096
COL_GP = 5120
COL_Q = 6144
COL_KC = 8192
COL_VC = 8704
COL_KS = 9216
COL_VS = 9728
COL_KW = 10240
COL_VW = 10752
COL_GN = 11264
N_PROJ = 13312
W_IN_MAIN = 9216

VMEM_LIMIT = 56 * 1024 * 1024

PROJ_TM = 512
PROJ_TN = 1024
N_GM_TILES = 4096 // PROJ_TN
PREP_TM = 512
SEL_TK = 512
AQ = 256
AR = GROUP * AQ
WIN_KEYS = WINDOW + AQ
CMP_MASK_ROWS = 256
V_ROWS = HEAD_DIM + 16
POOL_TM = 512
OUT_TM = 256


def _nt_dot(a, b):
    return lax.dot_general(a, b, (((1,), (1,)), ((), ())), preferred_element_type=F32)


def _dot(a, b):
    return jnp.dot(a, b, preferred_element_type=F32)


def _proj_kernel(x_ref, nw_ref, w_ref, wg_ref, b_ref, o_ref, g_ref, h_scr):
    j = pl.program_id(1)

    @pl.when(j == 0)
    def _():
        x = x_ref[...]
        ms = jnp.mean(x * x, axis=-1, keepdims=True)
        hb = (x * lax.rsqrt(ms + EPS) * nw_ref[...]).astype(BF16)
        h_scr[...] = hb
        g_ref[...] = jax.nn.sigmoid(_dot(hb, wg_ref[...]))

    acc = _dot(h_scr[...], w_ref[...])

    @pl.when(j < N_GM_TILES)
    def _():
        o_ref[...] = jax.nn.sigmoid(acc + b_ref[...])

    @pl.when(j >= N_GM_TILES)
    def _():
        o_ref[...] = acc


def _proj(x2, norm_w, w_cat, w_gate, b_merge):
    grid = (SEQ // PROJ_TM, N_PROJ // PROJ_TN)
    return pl.pallas_call(
        _proj_kernel,
        grid=grid,
        in_specs=[
            pl.BlockSpec((PROJ_TM, D_MODEL), lambda i, j: (i, 0)),
            pl.BlockSpec((1, D_MODEL), lambda i, j: (0, 0)),
            pl.BlockSpec((D_MODEL, PROJ_TN), lambda i, j: (0, j)),
            pl.BlockSpec((D_MODEL, N_KV_HEADS * 128), lambda i, j: (0, 0)),
            pl.BlockSpec((1, PROJ_TN), lambda i, j: (0, jnp.minimum(j, N_GM_TILES - 1))),
        ],
        out_specs=[
            pl.BlockSpec((PROJ_TM, PROJ_TN), lambda i, j: (i, j)),
            pl.BlockSpec((PROJ_TM, N_KV_HEADS * 128), lambda i, j: (i, 0)),
        ],
        out_shape=[
            jax.ShapeDtypeStruct((SEQ, N_PROJ), F32),
            jax.ShapeDtypeStruct((SEQ, N_KV_HEADS * 128), F32),
        ],
        scratch_shapes=[pltpu.VMEM((PROJ_TM, D_MODEL), BF16)],
        compiler_params=pltpu.CompilerParams(
            dimension_semantics=("arbitrary", "arbitrary"), vmem_limit_bytes=VMEM_LIMIT),
        name="proj",
    )(x2, norm_w, w_cat, w_gate, b_merge)


def _rope(x, cos, sin_signed):
    return x * cos + pltpu.roll(x, HEAD_DIM // 2, 1) * sin_signed


def _prep_kernel(q_ref, ks_ref, vs_ref, kw_ref, vw_ref, cos_ref, sin_ref,
                 qo_ref, kso_ref, vso_ref, kwo_ref, vwo_ref):
    i = pl.program_id(0)
    t0 = jnp.maximum(i - 1, 0) * PREP_TM
    cos = cos_ref[...]
    sin = sin_ref[...]
    for hq in range(N_Q_HEADS):
        sl = slice(hq * HEAD_DIM, (hq + 1) * HEAD_DIM)
        qo_ref[:, sl] = (_rope(q_ref[:, sl], cos, sin) * Q_SCALE).astype(BF16)
    blk = (t0 + lax.broadcasted_iota(jnp.int32, (PREP_TM, N_BLK), 0)) // SEL_BLOCK
    lane = lax.broadcasted_iota(jnp.int32, (PREP_TM, N_BLK), 1)
    onehot = jnp.where(blk == lane, 1.0, 0.0).astype(BF16)
    pad_flag = jnp.where(lane == 0, 1.0, 0.0)
    is_pad = i == 0
    ones_rows = jnp.where(lax.broadcasted_iota(jnp.int32, (V_ROWS - HEAD_DIM, PREP_TM), 0) == 0,
                          1.0, 0.0).astype(BF16)
    for h in range(N_KV_HEADS):
        sl = slice(h * HEAD_DIM, (h + 1) * HEAD_DIM)
        kso_ref[h, :, 0:HEAD_DIM] = _rope(ks_ref[:, sl], cos, sin).astype(BF16)
        kso_ref[h, :, HEAD_DIM:2 * HEAD_DIM] = onehot
        vso_ref[h, 0:HEAD_DIM, :] = vs_ref[:, sl].T.astype(BF16)
        vso_ref[h, HEAD_DIM:V_ROWS, :] = ones_rows
        kwo_ref[h, :, 0:HEAD_DIM] = jnp.where(is_pad, 0.0, _rope(kw_ref[:, sl], cos, sin)).astype(BF16)
        kwo_ref[h, :, HEAD_DIM:2 * HEAD_DIM] = jnp.where(is_pad, pad_flag, 0.0).astype(BF16)
        vwo_ref[h, 0:HEAD_DIM, :] = jnp.where(is_pad, 0.0, vw_ref[:, sl].T).astype(BF16)
        vwo_ref[h, HEAD_DIM:V_ROWS, :] = ones_rows


def _prep(proj, cos_tab, sin_tab):
    tm = PREP_TM
    assert tm == WINDOW
    tok = lambda i: jnp.maximum(i - 1, 0)
    kv_spec = lambda col: pl.BlockSpec((tm, KV_WIDTH), lambda i: (tok(i), col // KV_WIDTH))
    return pl.pallas_call(
        _prep_kernel,
        grid=(SEQ // tm + 1,),
        in_specs=[
            pl.BlockSpec((tm, NSA_WIDTH), lambda i: (tok(i), COL_Q // NSA_WIDTH)),
            kv_spec(COL_KS), kv_spec(COL_VS), kv_spec(COL_KW), kv_spec(COL_VW),
            pl.BlockSpec((tm, HEAD_DIM), lambda i: (tok(i), 0)),
            pl.BlockSpec((tm, HEAD_DIM), lambda i: (tok(i), 0)),
        ],
        out_specs=[
            pl.BlockSpec((tm, NSA_WIDTH), lambda i: (tok(i), 0)),
            pl.BlockSpec((N_KV_HEADS, tm, 2 * HEAD_DIM), lambda i: (0, tok(i), 0)),
            pl.BlockSpec((N_KV_HEADS, V_ROWS, tm), lambda i: (0, 0, tok(i))),
            pl.BlockSpec((N_KV_HEADS, tm, 2 * HEAD_DIM), lambda i: (0, i, 0)),
            pl.BlockSpec((N_KV_HEADS, V_ROWS, tm), lambda i: (0, 0, i)),
        ],
        out_shape=[
            jax.ShapeDtypeStruct((SEQ, NSA_WIDTH), BF16),
            jax.ShapeDtypeStruct((N_KV_HEADS, SEQ, 2 * HEAD_DIM), BF16),
            jax.ShapeDtypeStruct((N_KV_HEADS, V_ROWS, SEQ), BF16),
            jax.ShapeDtypeStruct((N_KV_HEADS, WINDOW + SEQ, 2 * HEAD_DIM), BF16),
            jax.ShapeDtypeStruct((N_KV_HEADS, V_ROWS, WINDOW + SEQ), BF16),
        ],
        compiler_params=pltpu.CompilerParams(
            dimension_semantics=("arbitrary",), vmem_limit_bytes=VMEM_LIMIT),
        name="prep",
    )(proj, proj, proj, proj, proj, cos_tab, sin_tab)


def _cmp_kernel(x_ref, pe_ref, w1_ref, w2_ref, cos_ref, sin_ref, o_ref, ot_ref, hi_scr):
    half = CMP_BLOCK // 2
    acc_lo = jnp.zeros((N_CMP_PAD, CMP_HIDDEN), F32)
    acc_hi = jnp.zeros((N_CMP_PAD, CMP_HIDDEN), F32)
    for l in range(half):
        xl = x_ref[pl.ds(l, N_CMP_PAD, stride=CMP_STRIDE), :]
        acc_lo += _dot((xl + pe_ref[0, l:l + 1, :]).astype(BF16), w1_ref[0, l])
        acc_hi += _dot((xl + pe_ref[0, half + l:half + l + 1, :]).astype(BF16), w1_ref[0, half + l])
    hi_scr[0:N_CMP_PAD, :] = acc_hi
    hi_scr[N_CMP_PAD:N_CMP_PAD + 8, :] = jnp.zeros((8, CMP_HIDDEN), F32)
    pre = acc_lo + hi_scr[pl.ds(1, N_CMP_PAD), :]
    hdn = pre * jax.nn.sigmoid(pre)
    out = _dot(hdn.astype(BF16), w2_ref[0])
    out = _rope(out, cos_ref[0], sin_ref[0])
    row = lax.broadcasted_iota(jnp.int32, (N_CMP_PAD, HEAD_DIM), 0)
    out = jnp.where(row < N_CMP, out, 0.0)
    o_ref[0, 0] = out.astype(BF16)
    ot_ref[0, 0] = out.T.astype(BF16)


def _compress(proj, pe, w1, w2, cos_c, sin_c):
    return pl.pallas_call(
        _cmp_kernel,
        grid=(2, N_KV_HEADS),
        in_specs=[
            pl.BlockSpec((SEQ, HEAD_DIM), lambda k, h: (0, COL_KC // HEAD_DIM + N_KV_HEADS * k + h)),
            pl.BlockSpec((1, CMP_BLOCK, HEAD_DIM), lambda k, h: (k, 0, 0)),
            pl.BlockSpec((1, CMP_BLOCK, HEAD_DIM, CMP_HIDDEN), lambda k, h: (k, 0, 0, 0)),
            pl.BlockSpec((1, CMP_HIDDEN, HEAD_DIM), lambda k, h: (k, 0, 0)),
            pl.BlockSpec((1, N_CMP_PAD, HEAD_DIM), lambda k, h: (k, 0, 0)),
            pl.BlockSpec((1, N_CMP_PAD, HEAD_DIM), lambda k, h: (k, 0, 0)),
        ],
        out_specs=[
            pl.BlockSpec((1, 1, N_CMP_PAD, HEAD_DIM), lambda k, h: (k, h, 0, 0)),
            pl.BlockSpec((1, 1, HEAD_DIM, N_CMP_PAD), lambda k, h: (k, h, 0, 0)),
        ],
        out_shape=[
            jax.ShapeDtypeStruct((2, N_KV_HEADS, N_CMP_PAD, HEAD_DIM), BF16),
            jax.ShapeDtypeStruct((2, N_KV_HEADS, HEAD_DIM, N_CMP_PAD), BF16),
        ],
        scratch_shapes=[pltpu.VMEM((N_CMP_PAD + 8, CMP_HIDDEN), F32)],
        compiler_params=pltpu.CompilerParams(
            dimension_semantics=("arbitrary", "arbitrary"), vmem_limit_bytes=VMEM_LIMIT),
        name="compress",
    )(proj, pe, w1, w2, cos_c, sin_c)


IMP_PAD = 8


def _attn_kernel(q_ref, kc_ref, vct_ref, ks_ref, vst_ref, kw_ref, vwt_ref, g_ref, gn_ref,
                 o_ref, imp_scr, oc_scr, ow_scr, s_scr, acc_scr):
    i = pl.program_id(1)
    s0 = i * AQ
    nrow = AR

    q = q_ref[...]
    q_rows = jnp.concatenate([q[:, g * HEAD_DIM:(g + 1) * HEAD_DIM] for g in range(GROUP)], axis=0)

    def mask_groups(s, mask):
        return jnp.concatenate(
            [jnp.where(mask, s[:, g * AQ:(g + 1) * AQ], NEG) for g in range(GROUP)], axis=1)

    def compressed(n_chunks):
        rows = 128 * n_chunks
        mrows = min(rows, CMP_MASK_ROWS)
        c0 = rows - mrows
        sc = _nt_dot(kc_ref[0, 0, 0:rows, :], q_rows)
        r_i = lax.broadcasted_iota(jnp.int32, (mrows, AQ), 0)
        t_i = lax.broadcasted_iota(jnp.int32, (mrows, AQ), 1)
        visible = CMP_STRIDE * r_i - t_i <= s0 - (CMP_BLOCK - 1) - CMP_STRIDE * c0
        tail = mask_groups(sc[c0:rows], visible)
        sc = tail if c0 == 0 else jnp.concatenate([sc[0:c0], tail], axis=0)
        m_c = jnp.max(sc, axis=0, keepdims=True)
        m_c = jnp.where(m_c > 0.5 * NEG, m_c, 0.0)
        e_c = jnp.exp2(sc - m_c)
        l_c = jnp.sum(e_c, axis=0, keepdims=True)
        p_c = e_c * (1.0 / jnp.where(l_c > 0.0, l_c, 1.0))
        oc_scr[...] = _dot(vct_ref[0, 0, :, 0:rows], p_c.astype(BF16))
        imp_c = (p_c[:, 0:AQ] + p_c[:, AQ:2 * AQ]) + (p_c[:, 2 * AQ:3 * AQ] + p_c[:, 3 * AQ:4 * AQ])
        rest = IMP_PAD + N_CMP_PAD + 8 - (IMP_PAD + rows)
        for a in range(AQ // 128):
            imp_scr[a, IMP_PAD:IMP_PAD + rows, :] = imp_c[:, a * 128:(a + 1) * 128]
            imp_scr[a, IMP_PAD + rows:IMP_PAD + rows + rest, :] = jnp.zeros((rest, 128), F32)

    for a in range(AQ // 128):
        imp_scr[a, 0:IMP_PAD, :] = jnp.zeros((IMP_PAD, 128), F32)
    n_cmp_chunks = i // (128 * CMP_STRIDE // AQ) + 1
    for n in range(1, N_CMP_PAD // 128 + 1):
        pl.when(n_cmp_chunks == n)(functools.partial(compressed, n))

    ratio = SEL_BLOCK // CMP_STRIDE
    tap = lambda k: jnp.concatenate(
        [imp_scr[a, pl.ds(IMP_PAD + k, N_BLK, stride=ratio), :] for a in range(AQ // 128)], axis=1)
    imp = tap(-1) + 2.0 * (tap(0) + tap(1) + tap(2)) + tap(3)

    jidx = lax.broadcasted_iota(jnp.int32, (N_BLK, AQ), 0).astype(F32)
    tok_l = lax.broadcasted_iota(jnp.int32, (1, AQ), 1)
    jt = (i * (AQ // SEL_BLOCK) + tok_l // SEL_BLOCK).astype(F32)
    v = jnp.where(jidx == 0.0, FORCE, imp)
    v = jnp.where(jidx == jt, FORCE, v)
    v = jnp.where(jidx == jt - 1.0, FORCE, v)
    v = jnp.where(jidx > jt, NEG, v)
    sel = jnp.zeros((N_BLK, AQ), F32)
    for _ in range(N_SEL):
        mx = jnp.max(v, axis=0, keepdims=True)
        cand = jnp.where(v == mx, jidx, float(N_BLK))
        pick = jidx == jnp.min(cand, axis=0, keepdims=True)
        sel = jnp.where(pick, 1.0, sel)
        v = jnp.where(pick, REMOVED, v)
    sel = jnp.where(jidx > jt, 0.0, sel)
    pen = jnp.where(sel.T > 0.5, 0.0, NEG).astype(BF16)
    q_aug = jnp.concatenate([q_rows, jnp.concatenate([pen] * GROUP, axis=0)], axis=1)

    flag_pen = jnp.where(lax.broadcasted_iota(jnp.int32, (nrow, HEAD_DIM), 1) == 0, NEG, 0.0).astype(BF16)
    q_win = jnp.concatenate([q_rows, flag_pen], axis=1)
    wk0 = pl.multiple_of(s0, AQ)
    sw = _nt_dot(kw_ref[0, pl.ds(wk0, WIN_KEYS), :], q_win)
    r_i = lax.broadcasted_iota(jnp.int32, (AQ, AQ), 0)
    t_i = lax.broadcasted_iota(jnp.int32, (AQ, AQ), 1)
    sw = jnp.concatenate([mask_groups(sw[0:AQ], r_i > t_i), sw[AQ:WINDOW],
                          mask_groups(sw[WINDOW:WIN_KEYS], r_i <= t_i)], axis=0)
    m_w = jnp.max(sw, axis=0, keepdims=True)
    p_w = jnp.exp2(sw - m_w).astype(BF16)
    ow_scr[...] = _dot(vwt_ref[0, :, pl.ds(wk0, WIN_KEYS)], p_w)

    def key_start(kt):
        return pl.multiple_of(kt * SEL_TK, SEL_TK)

    def scores(kt):
        return _nt_dot(ks_ref[0, pl.ds(key_start(kt), SEL_TK), :], q_aug)

    def consume(kt, s, m, m_tile):
        m_new = jnp.maximum(m, m_tile)
        alpha = jnp.exp2(m - m_new)
        p = jnp.exp2(s - m_new).astype(BF16)
        pv = _dot(vst_ref[0, :, pl.ds(key_start(kt), SEL_TK)], p)
        acc_scr[...] = alpha * acc_scr[...] + pv
        return m_new

    def body(kt, carry):
        m, m_tile = carry
        s_cur = s_scr[...]
        s_nxt = scores(kt + 1)
        m_nxt = jnp.max(s_nxt, axis=0, keepdims=True)
        m_new = consume(kt, s_cur, m, m_tile)
        s_scr[...] = s_nxt
        return m_new, m_nxt

    n_full = (i * AQ) // SEL_TK
    s_first = scores(0)
    s_scr[...] = s_first
    acc_scr[...] = jnp.zeros((V_ROWS, nrow), F32)
    init = (jnp.full((1, nrow), NEG, F32), jnp.max(s_first, axis=0, keepdims=True))
    m_s, _ = lax.fori_loop(0, n_full, body, init)
    off = pl.multiple_of(s0 - n_full * SEL_TK, AQ)
    s_scr[pl.ds(off, AQ), :] = mask_groups(s_scr[pl.ds(off, AQ), :], r_i <= t_i)
    s_diag = s_scr[...]
    consume(n_full, s_diag, m_s, jnp.max(s_diag, axis=0, keepdims=True))

    gt = g_ref[...].T
    grow = lambda c: jnp.concatenate([gt[c * GROUP + g:c * GROUP + g + 1, :] for g in range(GROUP)], axis=1)
    l_s = acc_scr[HEAD_DIM:HEAD_DIM + 1, :]
    l_w = ow_scr[HEAD_DIM:HEAD_DIM + 1, :]
    out_t = (oc_scr[...] * grow(0) + acc_scr[0:HEAD_DIM, :] * (grow(1) / l_s)
             + ow_scr[0:HEAD_DIM, :] * (grow(2) / l_w))
    out = jnp.concatenate([out_t[:, g * AQ:(g + 1) * AQ].T for g in range(GROUP)], axis=1)
    gn = gn_ref[...]
    o_ref[...] = (out * (gn * jax.nn.sigmoid(gn))).astype(BF16)


def _attention(q_r, kc, vct, ks_aug, vst, kw_aug, vwt, gates, proj):
    head_res = lambda shape: pl.BlockSpec((1,) + shape, lambda h, i: (h, 0, 0))
    return pl.pallas_call(
        _attn_kernel,
        grid=(N_KV_HEADS, SEQ // AQ),
        in_specs=[
            pl.BlockSpec((AQ, GROUP * HEAD_DIM), lambda h, i: (i, h)),
            pl.BlockSpec((1, 1, N_CMP_PAD, HEAD_DIM), lambda h, i: (0, h, 0, 0)),
            pl.BlockSpec((1, 1, HEAD_DIM, N_CMP_PAD), lambda h, i: (1, h, 0, 0)),
            head_res((SEQ, 2 * HEAD_DIM)),
            head_res((V_ROWS, SEQ)),
            head_res((WINDOW + SEQ, 2 * HEAD_DIM)),
            head_res((V_ROWS, WINDOW + SEQ)),
            pl.BlockSpec((AQ, 128), lambda h, i: (i, h)),
            pl.BlockSpec((AQ, GROUP * HEAD_DIM), lambda h, i: (i, COL_GN // (GROUP * HEAD_DIM) + h)),
        ],
        out_specs=pl.BlockSpec((AQ, GROUP * HEAD_DIM), lambda h, i: (i, h)),
        out_shape=jax.ShapeDtypeStruct((SEQ, NSA_WIDTH), BF16),
        scratch_shapes=[pltpu.VMEM((AQ // 128, IMP_PAD + N_CMP_PAD + 8, 128), F32),
                        pltpu.VMEM((HEAD_DIM, AR), F32),
                        pltpu.VMEM((V_ROWS, AR), F32),
                        pltpu.VMEM((SEL_TK, AR), F32),
                        pltpu.VMEM((V_ROWS, AR), F32)],
        compiler_params=pltpu.CompilerParams(
            dimension_semantics=("arbitrary", "arbitrary"), vmem_limit_bytes=VMEM_LIMIT),
        name="attn",
    )(q_r, kc, vct, ks_aug, vst, kw_aug, vwt, gates, proj)


def _pool_kernel(u_ref, up_ref, gp_ref, mix_ref, sc_ref, wpo_ref, ya_ref, ext_scr):
    i = pl.program_id(0)
    tm = POOL_TM
    ext_scr[0:POOL_HALO, :] = jnp.where(i > 0, up_ref[...], 0.0)
    ext_scr[POOL_HALO:POOL_HALO + tm, :] = u_ref[...]
    t1 = i * tm + lax.broadcasted_iota(jnp.int32, (tm, 1), 0) + 1
    zs = []
    for g, w in enumerate(POOL_WINDOWS):
        cs = slice(g * POOL_GROUP, (g + 1) * POOL_GROUP)
        u = ext_scr[POOL_HALO:POOL_HALO + tm, cs]
        acc = u
        for k in range(1, w):
            acc = acc + ext_scr[POOL_HALO - k:POOL_HALO - k + tm, cs]
        cnt = jnp.minimum(t1, w).astype(F32)
        pooled = acc / cnt - u
        zs.append(_dot(pooled.astype(BF16), mix_ref[g]))
    z = jnp.concatenate(zs, axis=1) * sc_ref[...]
    gp = gp_ref[...]
    y = z * (gp * jax.nn.sigmoid(gp))
    ya_ref[...] = _dot(y.astype(BF16), wpo_ref[...])


def _pool(proj, mix, scale, w_pool_out):
    tm = POOL_TM
    return pl.pallas_call(
        _pool_kernel,
        grid=(SEQ // tm,),
        in_specs=[
            pl.BlockSpec((tm, POOL_WIDTH), lambda i: (i, COL_U // POOL_WIDTH)),
            pl.BlockSpec((POOL_HALO, POOL_WIDTH),
                         lambda i: (jnp.maximum(i * (tm // POOL_HALO) - 1, 0), COL_U // POOL_WIDTH)),
            pl.BlockSpec((tm, POOL_WIDTH), lambda i: (i, COL_GP // POOL_WIDTH)),
            pl.BlockSpec((len(POOL_WINDOWS), POOL_GROUP, POOL_GROUP), lambda i: (0, 0, 0)),
            pl.BlockSpec((1, POOL_WIDTH), lambda i: (0, 0)),
            pl.BlockSpec((POOL_WIDTH, D_MODEL), lambda i: (0, 0)),
        ],
        out_specs=pl.BlockSpec((tm, D_MODEL), lambda i: (i, 0)),
        out_shape=jax.ShapeDtypeStruct((SEQ, D_MODEL), F32),
        scratch_shapes=[pltpu.VMEM((POOL_HALO + tm, POOL_WIDTH), F32)],
        compiler_params=pltpu.CompilerParams(
            dimension_semantics=("arbitrary",), vmem_limit_bytes=VMEM_LIMIT),
        name="pool",
    )(proj, proj, proj, mix, scale, w_pool_out)


def _out_kernel(x_ref, ya_ref, o_ref, gm0_ref, gm1_ref, wn_ref, wo_ref, fw_ref, out_ref):
    y_b = _dot(o_ref[...], wn_ref[...])
    merged = gm0_ref[...] * ya_ref[...] + gm1_ref[...] * y_b
    r = x_ref[...] + _dot(merged.astype(BF16), wo_ref[...])
    ms = jnp.mean(r * r, axis=-1, keepdims=True)
    out_ref[...] = r * lax.rsqrt(ms + EPS) * fw_ref[...]


def _output(x2, y_a, o_gated, proj, w_nsa_out, w_out, final_norm_w):
    tm = OUT_TM
    row = lambda: pl.BlockSpec((tm, D_MODEL), lambda i: (i, 0))
    resident = lambda: pl.BlockSpec((D_MODEL, D_MODEL), lambda i: (0, 0), pipeline_mode=pl.Buffered(1))
    return pl.pallas_call(
        _out_kernel,
        grid=(SEQ // tm,),
        in_specs=[
            row(), row(), row(),
            pl.BlockSpec((tm, D_MODEL), lambda i: (i, 0)),
            pl.BlockSpec((tm, D_MODEL), lambda i: (i, 1)),
            resident(), resident(),
            pl.BlockSpec((1, D_MODEL), lambda i: (0, 0)),
        ],
        out_specs=row(),
        out_shape=jax.ShapeDtypeStruct((SEQ, D_MODEL), F32),
        compiler_params=pltpu.CompilerParams(
            dimension_semantics=("arbitrary",), vmem_limit_bytes=VMEM_LIMIT),
        name="out",
    )(x2, y_a, o_gated, proj, proj, w_nsa_out, w_out, final_norm_w)


def _rope_tables():
    half = HEAD_DIM // 2
    inv = ROPE_THETA ** (-jnp.arange(half, dtype=F32) / half)
    ang = jnp.arange(SEQ).astype(F32)[:, None] * inv[None, :]
    cos, sin = jnp.cos(ang), jnp.sin(ang)
    return jnp.concatenate([cos, cos], axis=-1), jnp.concatenate([-sin, sin], axis=-1)


def _gate_columns():
    src = np.zeros((N_KV_HEADS * 128,), np.int32)
    valid = np.zeros((N_KV_HEADS * 128,), bool)
    for h in range(N_KV_HEADS):
        for g in range(GROUP):
            for c in range(3):
                src[h * 128 + c * GROUP + g] = W_IN_MAIN + (h * GROUP + g) * 3 + c
                valid[h * 128 + c * GROUP + g] = True
    return src, valid


def kernel(x, norm_w, w_in, pool_mix, pool_scale, cmp_pe_k, cmp_w1_k, cmp_w2_k, cmp_pe_v, cmp_w1_v,
           cmp_w2_v, w_pool_out, w_nsa_out, w_merge, b_merge, w_out, final_norm_w):
    assert x.shape == (1, SEQ, D_MODEL) and norm_w.shape[0] == 1
    x2 = x[0]
    w_cat = jnp.concatenate([w_merge[0], w_in[0, :, :W_IN_MAIN]], axis=1).astype(BF16)
    src, valid = _gate_columns()
    w_gate = jnp.where(valid[None, :], w_in[0][:, src], 0.0).astype(BF16)
    proj, gates = _proj(x2, norm_w, w_cat, w_gate, b_merge)

    cos_tab, sin_tab = _rope_tables()
    q_r, ks_aug, vst, kw_aug, vwt = _prep(proj, cos_tab, sin_tab)

    pad = lambda t: jnp.pad(t[CMP_BLOCK - 1::CMP_STRIDE], ((0, N_CMP_PAD - N_CMP), (0, 0)))
    cos_c = jnp.stack([pad(cos_tab), jnp.ones((N_CMP_PAD, HEAD_DIM), F32)])
    sin_c = jnp.stack([pad(sin_tab), jnp.zeros((N_CMP_PAD, HEAD_DIM), F32)])
    pe = jnp.stack([cmp_pe_k[0], cmp_pe_v[0]])
    w1 = jnp.stack([cmp_w1_k[0], cmp_w1_v[0]]).astype(BF16)
    w2 = jnp.stack([cmp_w2_k[0], cmp_w2_v[0]]).astype(BF16)
    cmp_n, cmp_t = _compress(proj, pe, w1, w2, cos_c, sin_c)

    o_gated = _attention(q_r, cmp_n, cmp_t, ks_aug, vst, kw_aug, vwt, gates, proj)

    y_a = _pool(proj, pool_mix[0].astype(BF16), pool_scale, w_pool_out[0].astype(BF16))
    out = _output(x2, y_a, o_gated, proj, w_nsa_out[0].astype(BF16), w_out[0].astype(BF16),
                  final_norm_w[None, :])
    return out[None]
```

```python
import functools
import math

import numpy as np
import jax
import jax.numpy as jnp
from jax import lax
from jax.experimental import pallas as pl
from jax.experimental.pallas import tpu as pltpu

F32 = jnp.float32
BF16 = jnp.bfloat16

D_MODEL = 2048
SEQ = 8192
POOL_WIDTH = 1024
POOL_WINDOWS = (2, 4, 8, 16)
POOL_GROUP = 256
POOL_HALO = 16
N_Q_HEADS = 16
N_KV_HEADS = 4
GROUP = 4
HEAD_DIM = 128
NSA_WIDTH = 2048
KV_WIDTH = 512
CMP_BLOCK = 32
CMP_STRIDE = 16
CMP_HIDDEN = 256
N_CMP = (SEQ - CMP_BLOCK) // CMP_STRIDE + 1
N_CMP_PAD = 512
SEL_BLOCK = 64
N_BLK = SEQ // SEL_BLOCK
N_SEL = 16
WINDOW = 512
Q_BLOCK = 128
N_QB = SEQ // Q_BLOCK
ROPE_THETA = 10000.0
EPS = 1e-6
NEG = -1e30
FORCE = 1e6
REMOVED = -3e38
LOG2E = 1.4426950408889634
Q_SCALE = HEAD_DIM ** -0.5 * LOG2E

COL_GM = 0
COL_U = 4096
COL_GP = 5120
COL_Q = 6144
COL_KC = 8192
COL_VC = 8704
COL_KS = 9216
COL_VS = 9728
COL_KW = 10240
COL_VW = 10752
COL_GN = 11264
N_PROJ = 13312
W_IN_MAIN = 9216

VMEM_LIMIT = 56 * 1024 * 1024

PROJ_TM = 1024
PROJ_TN = 1024
N_GM_TILES = 4096 // PROJ_TN
PREP_TM = 512
SEL_TK = 512
AQ = 256
AR = GROUP * AQ
WIN_KEYS = WINDOW + AQ
CMP_MASK_ROWS = 256
V_ROWS = HEAD_DIM + 16
POOL_TM = 512
OUT_TM = 256


def _nt_dot(a, b):
    return lax.dot_general(a, b, (((1,), (1,)), ((), ())), preferred_element_type=F32)


def _dot(a, b):
    return jnp.dot(a, b, preferred_element_type=F32)


def _proj_kernel(x_ref, nw_ref, w_ref, wg_ref, b_ref, o_ref, g_ref, h_scr):
    j = pl.program_id(1)

    @pl.when(j == 0)
    def _():
        x = x_ref[...]
        ms = jnp.mean(x * x, axis=-1, keepdims=True)
        hb = (x * lax.rsqrt(ms + EPS) * nw_ref[...]).astype(BF16)
        h_scr[...] = hb
        g_ref[...] = jax.nn.sigmoid(_dot(hb, wg_ref[...]))

    acc = _dot(h_scr[...], w_ref[...])

    @pl.when(j < N_GM_TILES)
    def _():
        o_ref[...] = jax.nn.sigmoid(acc + b_ref[...])

    @pl.when(j >= N_GM_TILES)
    def _():
        o_ref[...] = acc


def _proj(x2, norm_w, w_cat, w_gate, b_merge):
    grid = (SEQ // PROJ_TM, N_PROJ // PROJ_TN)
    return pl.pallas_call(
        _proj_kernel,
        grid=grid,
        in_specs=[
            pl.BlockSpec((PROJ_TM, D_MODEL), lambda i, j: (i, 0)),
            pl.BlockSpec((1, D_MODEL), lambda i, j: (0, 0)),
            pl.BlockSpec((D_MODEL, PROJ_TN), lambda i, j: (0, j)),
            pl.BlockSpec((D_MODEL, N_KV_HEADS * 128), lambda i, j: (0, 0)),
            pl.BlockSpec((1, PROJ_TN), lambda i, j: (0, jnp.minimum(j, N_GM_TILES - 1))),
        ],
        out_specs=[
            pl.BlockSpec((PROJ_TM, PROJ_TN), lambda i, j: (i, j)),
            pl.BlockSpec((PROJ_TM, N_KV_HEADS * 128), lambda i, j: (i, 0)),
        ],
        out_shape=[
            jax.ShapeDtypeStruct((SEQ, N_PROJ), F32),
            jax.ShapeDtypeStruct((SEQ, N_KV_HEADS * 128), F32),
        ],
        scratch_shapes=[pltpu.VMEM((PROJ_TM, D_MODEL), BF16)],
        compiler_params=pltpu.CompilerParams(
            dimension_semantics=("arbitrary", "arbitrary"), vmem_limit_bytes=VMEM_LIMIT),
        name="proj",
    )(x2, norm_w, w_cat, w_gate, b_merge)


def _rope(x, cos, sin_signed):
    return x * cos + pltpu.roll(x, HEAD_DIM // 2, 1) * sin_signed


def _prep_kernel(q_ref, ks_ref, vs_ref, kw_ref, vw_ref, cos_ref, sin_ref,
                 qo_ref, kso_ref, vso_ref, kwo_ref, vwo_ref):
    i = pl.program_id(0)
    t0 = jnp.maximum(i - 1, 0) * PREP_TM
    cos = cos_ref[...]
    sin = sin_ref[...]
    for hq in range(N_Q_HEADS):
        sl = slice(hq * HEAD_DIM, (hq + 1) * HEAD_DIM)
        qo_ref[hq] = (_rope(q_ref[:, sl], cos, sin) * Q_SCALE).T.astype(BF16)
    blk = (t0 + lax.broadcasted_iota(jnp.int32, (PREP_TM, N_BLK), 0)) // SEL_BLOCK
    lane = lax.broadcasted_iota(jnp.int32, (PREP_TM, N_BLK), 1)
    onehot = jnp.where(blk == lane, 1.0, 0.0).astype(BF16)
    pad_flag = jnp.where(lane == 0, 1.0, 0.0)
    is_pad = i == 0
    ones_rows = jnp.where(lax.broadcasted_iota(jnp.int32, (V_ROWS - HEAD_DIM, PREP_TM), 0) == 0,
                          1.0, 0.0).astype(BF16)
    for h in range(N_KV_HEADS):
        sl = slice(h * HEAD_DIM, (h + 1) * HEAD_DIM)
        kso_ref[h, :, 0:HEAD_DIM] = _rope(ks_ref[:, sl], cos, sin).astype(BF16)
        kso_ref[h, :, HEAD_DIM:2 * HEAD_DIM] = onehot
        vso_ref[h, 0:HEAD_DIM, :] = vs_ref[:, sl].T.astype(BF16)
        vso_ref[h, HEAD_DIM:V_ROWS, :] = ones_rows
        kwo_ref[h, :, 0:HEAD_DIM] = jnp.where(is_pad, 0.0, _rope(kw_ref[:, sl], cos, sin)).astype(BF16)
        kwo_ref[h, :, HEAD_DIM:2 * HEAD_DIM] = jnp.where(is_pad, pad_flag, 0.0).astype(BF16)
        vwo_ref[h, 0:HEAD_DIM, :] = jnp.where(is_pad, 0.0, vw_ref[:, sl].T).astype(BF16)
        vwo_ref[h, HEAD_DIM:V_ROWS, :] = ones_rows


def _prep(proj, cos_tab, sin_tab):
    tm = PREP_TM
    assert tm == WINDOW
    tok = lambda i: jnp.maximum(i - 1, 0)
    kv_spec = lambda col: pl.BlockSpec((tm, KV_WIDTH), lambda i: (tok(i), col // KV_WIDTH))
    return pl.pallas_call(
        _prep_kernel,
        grid=(SEQ // tm + 1,),
        in_specs=[
            pl.BlockSpec((tm, NSA_WIDTH), lambda i: (tok(i), COL_Q // NSA_WIDTH)),
            kv_spec(COL_KS), kv_spec(COL_VS), kv_spec(COL_KW), kv_spec(COL_VW),
            pl.BlockSpec((tm, HEAD_DIM), lambda i: (tok(i), 0)),
            pl.BlockSpec((tm, HEAD_DIM), lambda i: (tok(i), 0)),
        ],
        out_specs=[
            pl.BlockSpec((N_Q_HEADS, HEAD_DIM, tm), lambda i: (0, 0, tok(i))),
            pl.BlockSpec((N_KV_HEADS, tm, 2 * HEAD_DIM), lambda i: (0, tok(i), 0)),
            pl.BlockSpec((N_KV_HEADS, V_ROWS, tm), lambda i: (0, 0, tok(i))),
            pl.BlockSpec((N_KV_HEADS, tm, 2 * HEAD_DIM), lambda i: (0, i, 0)),
            pl.BlockSpec((N_KV_HEADS, V_ROWS, tm), lambda i: (0, 0, i)),
        ],
        out_shape=[
            jax.ShapeDtypeStruct((N_Q_HEADS, HEAD_DIM, SEQ), BF16),
            jax.ShapeDtypeStruct((N_KV_HEADS, SEQ, 2 * HEAD_DIM), BF16),
            jax.ShapeDtypeStruct((N_KV_HEADS, V_ROWS, SEQ), BF16),
            jax.ShapeDtypeStruct((N_KV_HEADS, WINDOW + SEQ, 2 * HEAD_DIM), BF16),
            jax.ShapeDtypeStruct((N_KV_HEADS, V_ROWS, WINDOW + SEQ), BF16),
        ],
        compiler_params=pltpu.CompilerParams(
            dimension_semantics=("arbitrary",), vmem_limit_bytes=VMEM_LIMIT),
        name="prep",
    )(proj, proj, proj, proj, proj, cos_tab, sin_tab)


def _cmp_kernel(x_ref, pe_ref, w1_ref, w2_ref, cos_ref, sin_ref, o_ref, ot_ref, hi_scr):
    half = CMP_BLOCK // 2
    acc_lo = jnp.zeros((N_CMP_PAD, CMP_HIDDEN), F32)
    acc_hi = jnp.zeros((N_CMP_PAD, CMP_HIDDEN), F32)
    for l in range(half):
        xl = x_ref[pl.ds(l, N_CMP_PAD, stride=CMP_STRIDE), :]
        acc_lo += _dot((xl + pe_ref[0, l:l + 1, :]).astype(BF16), w1_ref[0, l])
        acc_hi += _dot((xl + pe_ref[0, half + l:half + l + 1, :]).astype(BF16), w1_ref[0, half + l])
    hi_scr[0:N_CMP_PAD, :] = acc_hi
    hi_scr[N_CMP_PAD:N_CMP_PAD + 8, :] = jnp.zeros((8, CMP_HIDDEN), F32)
    pre = acc_lo + hi_scr[pl.ds(1, N_CMP_PAD), :]
    hdn = pre * jax.nn.sigmoid(pre)
    out = _dot(hdn.astype(BF16), w2_ref[0])
    out = _rope(out, cos_ref[0], sin_ref[0])
    row = lax.broadcasted_iota(jnp.int32, (N_CMP_PAD, HEAD_DIM), 0)
    out = jnp.where(row < N_CMP, out, 0.0)
    o_ref[0, 0] = out.astype(BF16)
    ot_ref[0, 0] = out.T.astype(BF16)


def _compress(proj, pe, w1, w2, cos_c, sin_c):
    return pl.pallas_call(
        _cmp_kernel,
        grid=(2, N_KV_HEADS),
        in_specs=[
            pl.BlockSpec((SEQ, HEAD_DIM), lambda k, h: (0, COL_KC // HEAD_DIM + N_KV_HEADS * k + h)),
            pl.BlockSpec((1, CMP_BLOCK, HEAD_DIM), lambda k, h: (k, 0, 0)),
            pl.BlockSpec((1, CMP_BLOCK, HEAD_DIM, CMP_HIDDEN), lambda k, h: (k, 0, 0, 0)),
            pl.BlockSpec((1, CMP_HIDDEN, HEAD_DIM), lambda k, h: (k, 0, 0)),
            pl.BlockSpec((1, N_CMP_PAD, HEAD_DIM), lambda k, h: (k, 0, 0)),
            pl.BlockSpec((1, N_CMP_PAD, HEAD_DIM), lambda k, h: (k, 0, 0)),
        ],
        out_specs=[
            pl.BlockSpec((1, 1, N_CMP_PAD, HEAD_DIM), lambda k, h: (k, h, 0, 0)),
            pl.BlockSpec((1, 1, HEAD_DIM, N_CMP_PAD), lambda k, h: (k, h, 0, 0)),
        ],
        out_shape=[
            jax.ShapeDtypeStruct((2, N_KV_HEADS, N_CMP_PAD, HEAD_DIM), BF16),
            jax.ShapeDtypeStruct((2, N_KV_HEADS, HEAD_DIM, N_CMP_PAD), BF16),
        ],
        scratch_shapes=[pltpu.VMEM((N_CMP_PAD + 8, CMP_HIDDEN), F32)],
        compiler_params=pltpu.CompilerParams(
            dimension_semantics=("arbitrary", "arbitrary"), vmem_limit_bytes=VMEM_LIMIT),
        name="compress",
    )(proj, pe, w1, w2, cos_c, sin_c)


IMP_PAD = 8


def _attn_kernel(q_ref, kc_ref, vct_ref, ks_ref, vst_ref, kw_ref, vwt_ref, g_ref, gn_ref,
                 o_ref, imp_scr, sel_scr, oc_scr, ow_scr, s_scr, p_scr, acc_scr):
    i = pl.program_id(1)
    s0 = i * AQ
    nrow = AR

    q_t = jnp.concatenate([q_ref[g] for g in range(GROUP)], axis=1)

    def mask_groups(s, mask):
        return jnp.concatenate(
            [jnp.where(mask, s[:, g * AQ:(g + 1) * AQ], NEG) for g in range(GROUP)], axis=1)

    def compressed(n_chunks):
        rows = 128 * n_chunks
        mrows = min(rows, CMP_MASK_ROWS)
        c0 = rows - mrows
        sc = _dot(kc_ref[0, 0, 0:rows, :], q_t)
        r_i = lax.broadcasted_iota(jnp.int32, (mrows, AQ), 0)
        t_i = lax.broadcasted_iota(jnp.int32, (mrows, AQ), 1)
        visible = CMP_STRIDE * r_i - t_i <= s0 - (CMP_BLOCK - 1) - CMP_STRIDE * c0
        tail = mask_groups(sc[c0:rows], visible)
        sc = tail if c0 == 0 else jnp.concatenate([sc[0:c0], tail], axis=0)
        m_c = jnp.max(sc, axis=0, keepdims=True)
        m_c = jnp.where(m_c > 0.5 * NEG, m_c, 0.0)
        e_c = jnp.exp2(sc - m_c)
        l_c = jnp.sum(e_c, axis=0, keepdims=True)
        p_c = e_c * (1.0 / jnp.where(l_c > 0.0, l_c, 1.0))
        oc_scr[...] = _dot(vct_ref[0, 0, :, 0:rows], p_c.astype(BF16))
        imp_c = (p_c[:, 0:AQ] + p_c[:, AQ:2 * AQ]) + (p_c[:, 2 * AQ:3 * AQ] + p_c[:, 3 * AQ:4 * AQ])
        rest = IMP_PAD + N_CMP_PAD + 8 - (IMP_PAD + rows)
        for a in range(AQ // 128):
            imp_scr[a, IMP_PAD:IMP_PAD + rows, :] = imp_c[:, a * 128:(a + 1) * 128]
            imp_scr[a, IMP_PAD + rows:IMP_PAD + rows + rest, :] = jnp.zeros((rest, 128), F32)

    for a in range(AQ // 128):
        imp_scr[a, 0:IMP_PAD, :] = jnp.zeros((IMP_PAD, 128), F32)
    n_cmp_chunks = i // (128 * CMP_STRIDE // AQ) + 1
    for n in range(1, N_CMP_PAD // 128 + 1):
        pl.when(n_cmp_chunks == n)(functools.partial(compressed, n))

    ratio = SEL_BLOCK // CMP_STRIDE
    tap = lambda k: jnp.concatenate(
        [imp_scr[a, pl.ds(IMP_PAD + k, N_BLK, stride=ratio), :] for a in range(AQ // 128)], axis=1)
    imp = tap(-1) + 2.0 * (tap(0) + tap(1) + tap(2)) + tap(3)

    jidx = lax.broadcasted_iota(jnp.int32, (N_BLK, AQ), 0).astype(F32)
    tok_l = lax.broadcasted_iota(jnp.int32, (1, AQ), 1)
    jt = (i * (AQ // SEL_BLOCK) + tok_l // SEL_BLOCK).astype(F32)
    v = jnp.where(jidx == 0.0, REMOVED, imp)
    v = jnp.where(jidx == jt, REMOVED, v)
    v = jnp.where(jidx == jt - 1.0, REMOVED, v)
    v = jnp.where(jidx > jt, NEG, v)
    for _ in range(N_SEL - 3):
        v = jnp.where(v == jnp.max(v, axis=0, keepdims=True), REMOVED, v)
    sel_fast = jnp.where(jidx > jt, 0.0, jnp.where(v == REMOVED, 1.0, 0.0))
    n_picked = jnp.sum(sel_fast, axis=0, keepdims=True)
    miscount = jnp.max(jnp.abs(n_picked - jnp.minimum(jt + 1.0, float(N_SEL))))
    sel_scr[...] = sel_fast

    flag_pen = jnp.where(lax.broadcasted_iota(jnp.int32, (HEAD_DIM, nrow), 0) == 0, NEG, 0.0).astype(BF16)
    q_win = jnp.concatenate([q_t, flag_pen], axis=0)
    wk0 = pl.multiple_of(s0, AQ)
    sw = _dot(kw_ref[0, pl.ds(wk0, WIN_KEYS), :], q_win).astype(BF16)
    r_i = lax.broadcasted_iota(jnp.int32, (AQ, AQ), 0)
    t_i = lax.broadcasted_iota(jnp.int32, (AQ, AQ), 1)
    tri_old = jnp.concatenate([jnp.where(r_i > t_i, 0.0, NEG).astype(BF16)] * GROUP, axis=1)
    tri_causal = jnp.concatenate([jnp.where(r_i <= t_i, 0.0, NEG).astype(BF16)] * GROUP, axis=1)
    sw = jnp.concatenate([sw[0:AQ] + tri_old, sw[AQ:WINDOW], sw[WINDOW:WIN_KEYS] + tri_causal], axis=0)
    m_w = jnp.max(sw, axis=0, keepdims=True)
    p_w = jnp.exp2(sw - m_w)
    ow_scr[...] = _dot(vwt_ref[0, :, pl.ds(wk0, WIN_KEYS)], p_w)

    @pl.when(miscount > 0.5)
    def _():
        w = jnp.where(jidx == 0.0, FORCE, imp)
        w = jnp.where(jidx == jt, FORCE, w)
        w = jnp.where(jidx == jt - 1.0, FORCE, w)
        w = jnp.where(jidx > jt, NEG, w)
        sel = jnp.zeros((N_BLK, AQ), F32)
        for _ in range(N_SEL):
            mx = jnp.max(w, axis=0, keepdims=True)
            cand = jnp.where(w == mx, jidx, float(N_BLK))
            pick = jidx == jnp.min(cand, axis=0, keepdims=True)
            sel = jnp.where(pick, 1.0, sel)
            w = jnp.where(pick, REMOVED, w)
        sel_scr[...] = jnp.where(jidx > jt, 0.0, sel)

    pen = jnp.where(sel_scr[...] > 0.5, 0.0, NEG).astype(BF16)
    q_aug = jnp.concatenate([q_t, jnp.concatenate([pen] * GROUP, axis=1)], axis=0)

    def key_start(kt):
        return pl.multiple_of(kt * SEL_TK, SEL_TK)

    def scores(kt):
        return _dot(ks_ref[0, pl.ds(key_start(kt), SEL_TK), :], q_aug).astype(BF16)

    def col_max(s):
        return jnp.max(s, axis=0, keepdims=True).astype(F32)

    def add_pv(kt, p, alpha):
        pv = _dot(vst_ref[0, :, pl.ds(key_start(kt), SEL_TK)], p)
        acc_scr[...] = alpha * acc_scr[...] + pv

    def body(kt, carry):
        m, m_tile, alpha_prev = carry
        p_prev = p_scr[...]
        s_cur = s_scr[...]
        s_nxt = scores(kt + 1)
        m_nxt = col_max(s_nxt)
        add_pv(jnp.maximum(kt - 1, 0), p_prev, alpha_prev)
        m_new = jnp.maximum(m, m_tile)
        p_scr[...] = jnp.exp2(s_cur - m_new.astype(BF16))
        s_scr[...] = s_nxt
        return m_new, m_nxt, jnp.exp2(m - m_new)

    n_full = (i * AQ) // SEL_TK
    s_first = scores(0)
    s_scr[...] = s_first
    p_scr[...] = jnp.zeros((SEL_TK, nrow), BF16)
    acc_scr[...] = jnp.zeros((V_ROWS, nrow), F32)
    init = (jnp.full((1, nrow), NEG, F32), col_max(s_first), jnp.ones((1, nrow), F32))
    carry = lax.fori_loop(0, n_full // 2, lambda j, c: body(2 * j + 1, body(2 * j, c)), init)
    m_s, _, alpha_last = lax.cond(n_full % 2 == 1, lambda c: body(n_full - 1, c), lambda c: c, carry)
    add_pv(jnp.maximum(n_full - 1, 0), p_scr[...], alpha_last)
    off = pl.multiple_of(s0 - n_full * SEL_TK, AQ)
    s_scr[pl.ds(off, AQ), :] = s_scr[pl.ds(off, AQ), :] + tri_causal
    s_diag = s_scr[...]
    m_fin = jnp.maximum(m_s, col_max(s_diag))
    add_pv(n_full, jnp.exp2(s_diag - m_fin.astype(BF16)), jnp.exp2(m_s - m_fin))

    gt = g_ref[...].T
    grow = lambda c: jnp.concatenate([gt[c * GROUP + g:c * GROUP + g + 1, :] for g in range(GROUP)], axis=1)
    l_s = acc_scr[HEAD_DIM:HEAD_DIM + 1, :]
    l_w = ow_scr[HEAD_DIM:HEAD_DIM + 1, :]
    out_t = (oc_scr[...] * grow(0) + acc_scr[0:HEAD_DIM, :] * (grow(1) / l_s)
             + ow_scr[0:HEAD_DIM, :] * (grow(2) / l_w))
    out = jnp.concatenate([out_t[:, g * AQ:(g + 1) * AQ].T for g in range(GROUP)], axis=1)
    gn = gn_ref[...]
    o_ref[...] = (out * (gn * jax.nn.sigmoid(gn))).astype(BF16)


def _attention(q_r, kc, vct, ks_aug, vst, kw_aug, vwt, gates, proj):
    head_res = lambda shape: pl.BlockSpec((1,) + shape, lambda h, i: (h, 0, 0))
    return pl.pallas_call(
        _attn_kernel,
        grid=(N_KV_HEADS, SEQ // AQ),
        in_specs=[
            pl.BlockSpec((GROUP, HEAD_DIM, AQ), lambda h, i: (h, 0, i)),
            pl.BlockSpec((1, 1, N_CMP_PAD, HEAD_DIM), lambda h, i: (0, h, 0, 0)),
            pl.BlockSpec((1, 1, HEAD_DIM, N_CMP_PAD), lambda h, i: (1, h, 0, 0)),
            head_res((SEQ, 2 * HEAD_DIM)),
            head_res((V_ROWS, SEQ)),
            head_res((WINDOW + SEQ, 2 * HEAD_DIM)),
            head_res((V_ROWS, WINDOW + SEQ)),
            pl.BlockSpec((AQ, 128), lambda h, i: (i, h)),
            pl.BlockSpec((AQ, GROUP * HEAD_DIM), lambda h, i: (i, COL_GN // (GROUP * HEAD_DIM) + h)),
        ],
        out_specs=pl.BlockSpec((AQ, GROUP * HEAD_DIM), lambda h, i: (i, h)),
        out_shape=jax.ShapeDtypeStruct((SEQ, NSA_WIDTH), BF16),
        scratch_shapes=[pltpu.VMEM((AQ // 128, IMP_PAD + N_CMP_PAD + 8, 128), F32),
                        pltpu.VMEM((N_BLK, AQ), F32),
                        pltpu.VMEM((HEAD_DIM, AR), F32),
                        pltpu.VMEM((V_ROWS, AR), F32),
                        pltpu.VMEM((SEL_TK, AR), BF16),
                        pltpu.VMEM((SEL_TK, AR), BF16),
                        pltpu.VMEM((V_ROWS, AR), F32)],
        compiler_params=pltpu.CompilerParams(
            dimension_semantics=("arbitrary", "arbitrary"), vmem_limit_bytes=VMEM_LIMIT),
        name="attn",
    )(q_r, kc, vct, ks_aug, vst, kw_aug, vwt, gates, proj)


def _pool_kernel(u_ref, up_ref, gp_ref, mix_ref, sc_ref, wpo_ref, ya_ref, ext_scr):
    i = pl.program_id(0)
    tm = POOL_TM
    ext_scr[0:POOL_HALO, :] = jnp.where(i > 0, up_ref[...], 0.0)
    ext_scr[POOL_HALO:POOL_HALO + tm, :] = u_ref[...]
    t1 = i * tm + lax.broadcasted_iota(jnp.int32, (tm, 1), 0) + 1
    zs = []
    for g, w in enumerate(POOL_WINDOWS):
        cs = slice(g * POOL_GROUP, (g + 1) * POOL_GROUP)
        u = ext_scr[POOL_HALO:POOL_HALO + tm, cs]
        acc = u
        for k in range(1, w):
            acc = acc + ext_scr[POOL_HALO - k:POOL_HALO - k + tm, cs]
        cnt = jnp.minimum(t1, w).astype(F32)
        pooled = acc / cnt - u
        zs.append(_dot(pooled.astype(BF16), mix_ref[g]))
    z = jnp.concatenate(zs, axis=1) * sc_ref[...]
    gp = gp_ref[...]
    y = z * (gp * jax.nn.sigmoid(gp))
    ya_ref[...] = _dot(y.astype(BF16), wpo_ref[...])


def _pool(proj, mix, scale, w_pool_out):
    tm = POOL_TM
    return pl.pallas_call(
        _pool_kernel,
        grid=(SEQ // tm,),
        in_specs=[
            pl.BlockSpec((tm, POOL_WIDTH), lambda i: (i, COL_U // POOL_WIDTH)),
            pl.BlockSpec((POOL_HALO, POOL_WIDTH),
                         lambda i: (jnp.maximum(i * (tm // POOL_HALO) - 1, 0), COL_U // POOL_WIDTH)),
            pl.BlockSpec((tm, POOL_WIDTH), lambda i: (i, COL_GP // POOL_WIDTH)),
            pl.BlockSpec((len(POOL_WINDOWS), POOL_GROUP, POOL_GROUP), lambda i: (0, 0, 0)),
            pl.BlockSpec((1, POOL_WIDTH), lambda i: (0, 0)),
            pl.BlockSpec((POOL_WIDTH, D_MODEL), lambda i: (0, 0)),
        ],
        out_specs=pl.BlockSpec((tm, D_MODEL), lambda i: (i, 0)),
        out_shape=jax.ShapeDtypeStruct((SEQ, D_MODEL), F32),
        scratch_shapes=[pltpu.VMEM((POOL_HALO + tm, POOL_WIDTH), F32)],
        compiler_params=pltpu.CompilerParams(
            dimension_semantics=("arbitrary",), vmem_limit_bytes=VMEM_LIMIT),
        name="pool",
    )(proj, proj, proj, mix, scale, w_pool_out)


def _out_kernel(x_ref, ya_ref, o_ref, gm0_ref, gm1_ref, wn_ref, wo_ref, fw_ref, out_ref):
    y_b = _dot(o_ref[...], wn_ref[...])
    merged = gm0_ref[...] * ya_ref[...] + gm1_ref[...] * y_b
    r = x_ref[...] + _dot(merged.astype(BF16), wo_ref[...])
    ms = jnp.mean(r * r, axis=-1, keepdims=True)
    out_ref[...] = r * lax.rsqrt(ms + EPS) * fw_ref[...]


def _output(x2, y_a, o_gated, proj, w_nsa_out, w_out, final_norm_w):
    tm = OUT_TM
    row = lambda: pl.BlockSpec((tm, D_MODEL), lambda i: (i, 0))
    resident = lambda: pl.BlockSpec((D_MODEL, D_MODEL), lambda i: (0, 0), pipeline_mode=pl.Buffered(1))
    return pl.pallas_call(
        _out_kernel,
        grid=(SEQ // tm,),
        in_specs=[
            row(), row(), row(),
            pl.BlockSpec((tm, D_MODEL), lambda i: (i, 0)),
            pl.BlockSpec((tm, D_MODEL), lambda i: (i, 1)),
            resident(), resident(),
            pl.BlockSpec((1, D_MODEL), lambda i: (0, 0)),
        ],
        out_specs=row(),
        out_shape=jax.ShapeDtypeStruct((SEQ, D_MODEL), F32),
        compiler_params=pltpu.CompilerParams(
            dimension_semantics=("arbitrary",), vmem_limit_bytes=VMEM_LIMIT),
        name="out",
    )(x2, y_a, o_gated, proj, proj, w_nsa_out, w_out, final_norm_w)


def _rope_tables():
    half = HEAD_DIM // 2
    inv = ROPE_THETA ** (-jnp.arange(half, dtype=F32) / half)
    ang = jnp.arange(SEQ).astype(F32)[:, None] * inv[None, :]
    cos, sin = jnp.cos(ang), jnp.sin(ang)
    return jnp.concatenate([cos, cos], axis=-1), jnp.concatenate([-sin, sin], axis=-1)


def _gate_columns():
    src = np.zeros((N_KV_HEADS * 128,), np.int32)
    valid = np.zeros((N_KV_HEADS * 128,), bool)
    for h in range(N_KV_HEADS):
        for g in range(GROUP):
            for c in range(3):
                src[h * 128 + c * GROUP + g] = W_IN_MAIN + (h * GROUP + g) * 3 + c
                valid[h * 128 + c * GROUP + g] = True
    return src, valid


def kernel(x, norm_w, w_in, pool_mix, pool_scale, cmp_pe_k, cmp_w1_k, cmp_w2_k, cmp_pe_v, cmp_w1_v,
           cmp_w2_v, w_pool_out, w_nsa_out, w_merge, b_merge, w_out, final_norm_w):
    assert x.shape == (1, SEQ, D_MODEL) and norm_w.shape[0] == 1
    x2 = x[0]
    w_cat = jnp.concatenate([w_merge[0], w_in[0, :, :W_IN_MAIN]], axis=1).astype(BF16)
    src, valid = _gate_columns()
    w_gate = jnp.where(valid[None, :], w_in[0][:, src], 0.0).astype(BF16)
    proj, gates = _proj(x2, norm_w, w_cat, w_gate, b_merge)

    cos_tab, sin_tab = _rope_tables()
    q_r, ks_aug, vst, kw_aug, vwt = _prep(proj, cos_tab, sin_tab)

    pad = lambda t: jnp.pad(t[CMP_BLOCK - 1::CMP_STRIDE], ((0, N_CMP_PAD - N_CMP), (0, 0)))
    cos_c = jnp.stack([pad(cos_tab), jnp.ones((N_CMP_PAD, HEAD_DIM), F32)])
    sin_c = jnp.stack([pad(sin_tab), jnp.zeros((N_CMP_PAD, HEAD_DIM), F32)])
    pe = jnp.stack([cmp_pe_k[0], cmp_pe_v[0]])
    w1 = jnp.stack([cmp_w1_k[0], cmp_w1_v[0]]).astype(BF16)
    w2 = jnp.stack([cmp_w2_k[0], cmp_w2_v[0]]).astype(BF16)
    cmp_n, cmp_t = _compress(proj, pe, w1, w2, cos_c, sin_c)

    o_gated = _attention(q_r, cmp_n, cmp_t, ks_aug, vst, kw_aug, vwt, gates, proj)

    y_a = _pool(proj, pool_mix[0].astype(BF16), pool_scale, w_pool_out[0].astype(BF16))
    out = _output(x2, y_a, o_gated, proj, w_nsa_out[0].astype(BF16), w_out[0].astype(BF16),
                  final_norm_w[None, :])
    return out[None]
```

```python
import functools
import math

import numpy as np
import jax
import jax.numpy as jnp
from jax import lax
from jax.experimental import pallas as pl
from jax.experimental.pallas import tpu as pltpu

F32 = jnp.float32
BF16 = jnp.bfloat16

D_MODEL = 2048
SEQ = 8192
POOL_WIDTH = 1024
POOL_WINDOWS = (2, 4, 8, 16)
POOL_GROUP = 256
POOL_HALO = 16
N_Q_HEADS = 16
N_KV_HEADS = 4
GROUP = 4
HEAD_DIM = 128
NSA_WIDTH = 2048
KV_WIDTH = 512
CMP_BLOCK = 32
CMP_STRIDE = 16
CMP_HIDDEN = 256
N_CMP = (SEQ - CMP_BLOCK) // CMP_STRIDE + 1
N_CMP_PAD = 512
SEL_BLOCK = 64
N_BLK = SEQ // SEL_BLOCK
N_SEL = 16
WINDOW = 512
Q_BLOCK = 128
N_QB = SEQ // Q_BLOCK
ROPE_THETA = 10000.0
EPS = 1e-6
NEG = -1e30
FORCE = 1e6
REMOVED = -3e38
LOG2E = 1.4426950408889634
Q_SCALE = HEAD_DIM ** -0.5 * LOG2E

COL_GM = 0
COL_U = 4096
COL_GP = 5120
COL_Q = 6144
COL_KC = 8192
COL_VC = 8704
COL_KS = 9216
COL_VS = 9728
COL_KW = 10240
COL_VW = 10752
COL_GN = 11264
N_PROJ = 13312
W_IN_MAIN = 9216

VMEM_LIMIT = 56 * 1024 * 1024

PROJ_TM = 1024
PROJ_TN = 1024
N_GM_TILES = 4096 // PROJ_TN
PREP_TM = 512
SEL_TK = 512
AQ = 256
AR = GROUP * AQ
WIN_KEYS = WINDOW + AQ
CMP_MASK_ROWS = 256
V_ROWS = HEAD_DIM + 16
POOL_TM = 512
OUT_TM = 256


def _nt_dot(a, b):
    return lax.dot_general(a, b, (((1,), (1,)), ((), ())), preferred_element_type=F32)


def _dot(a, b):
    return jnp.dot(a, b, preferred_element_type=F32)


def _proj_kernel(x_ref, nw_ref, w_ref, wg_ref, b_ref, o_ref, g_ref, h_scr):
    j = pl.program_id(1)

    @pl.when(j == 0)
    def _():
        x = x_ref[...]
        ms = jnp.mean(x * x, axis=-1, keepdims=True)
        hb = (x * lax.rsqrt(ms + EPS) * nw_ref[...]).astype(BF16)
        h_scr[...] = hb
        g_ref[...] = jax.nn.sigmoid(_dot(hb, wg_ref[...]))

    acc = _dot(h_scr[...], w_ref[...])

    @pl.when(j < N_GM_TILES)
    def _():
        o_ref[...] = jax.nn.sigmoid(acc + b_ref[...])

    @pl.when(j >= N_GM_TILES)
    def _():
        o_ref[...] = acc


def _proj(x2, norm_w, w_cat, w_gate, b_merge):
    grid = (SEQ // PROJ_TM, N_PROJ // PROJ_TN)
    return pl.pallas_call(
        _proj_kernel,
        grid=grid,
        in_specs=[
            pl.BlockSpec((PROJ_TM, D_MODEL), lambda i, j: (i, 0)),
            pl.BlockSpec((1, D_MODEL), lambda i, j: (0, 0)),
            pl.BlockSpec((D_MODEL, PROJ_TN), lambda i, j: (0, j)),
            pl.BlockSpec((D_MODEL, N_KV_HEADS * 128), lambda i, j: (0, 0)),
            pl.BlockSpec((1, PROJ_TN), lambda i, j: (0, jnp.minimum(j, N_GM_TILES - 1))),
        ],
        out_specs=[
            pl.BlockSpec((PROJ_TM, PROJ_TN), lambda i, j: (i, j)),
            pl.BlockSpec((PROJ_TM, N_KV_HEADS * 128), lambda i, j: (i, 0)),
        ],
        out_shape=[
            jax.ShapeDtypeStruct((SEQ, N_PROJ), F32),
            jax.ShapeDtypeStruct((SEQ, N_KV_HEADS * 128), F32),
        ],
        scratch_shapes=[pltpu.VMEM((PROJ_TM, D_MODEL), BF16)],
        compiler_params=pltpu.CompilerParams(
            dimension_semantics=("arbitrary", "arbitrary"), vmem_limit_bytes=VMEM_LIMIT),
        name="proj",
    )(x2, norm_w, w_cat, w_gate, b_merge)


def _rope(x, cos, sin_signed):
    return x * cos + pltpu.roll(x, HEAD_DIM // 2, 1) * sin_signed


def _prep_kernel(q_ref, ks_ref, vs_ref, kw_ref, vw_ref, cos_ref, sin_ref,
                 qo_ref, kso_ref, vso_ref, kwo_ref, vwo_ref):
    i = pl.program_id(0)
    t0 = jnp.maximum(i - 1, 0) * PREP_TM
    cos = cos_ref[...]
    sin = sin_ref[...]
    for hq in range(N_Q_HEADS):
        sl = slice(hq * HEAD_DIM, (hq + 1) * HEAD_DIM)
        qo_ref[hq] = (_rope(q_ref[:, sl], cos, sin) * Q_SCALE).T.astype(BF16)
    blk = (t0 + lax.broadcasted_iota(jnp.int32, (PREP_TM, N_BLK), 0)) // SEL_BLOCK
    lane = lax.broadcasted_iota(jnp.int32, (PREP_TM, N_BLK), 1)
    onehot = jnp.where(blk == lane, 1.0, 0.0).astype(BF16)
    pad_flag = jnp.where(lane == 0, 1.0, 0.0)
    is_pad = i == 0
    ones_rows = jnp.where(lax.broadcasted_iota(jnp.int32, (V_ROWS - HEAD_DIM, PREP_TM), 0) == 0,
                          1.0, 0.0).astype(BF16)
    for h in range(N_KV_HEADS):
        sl = slice(h * HEAD_DIM, (h + 1) * HEAD_DIM)
        kso_ref[h, :, 0:HEAD_DIM] = _rope(ks_ref[:, sl], cos, sin).astype(BF16)
        kso_ref[h, :, HEAD_DIM:2 * HEAD_DIM] = onehot
        vso_ref[h, 0:HEAD_DIM, :] = vs_ref[:, sl].T.astype(BF16)
        vso_ref[h, HEAD_DIM:V_ROWS, :] = ones_rows
        kwo_ref[h, :, 0:HEAD_DIM] = jnp.where(is_pad, 0.0, _rope(kw_ref[:, sl], cos, sin)).astype(BF16)
        kwo_ref[h, :, HEAD_DIM:2 * HEAD_DIM] = jnp.where(is_pad, pad_flag, 0.0).astype(BF16)
        vwo_ref[h, 0:HEAD_DIM, :] = jnp.where(is_pad, 0.0, vw_ref[:, sl].T).astype(BF16)
        vwo_ref[h, HEAD_DIM:V_ROWS, :] = ones_rows


def _prep(proj, cos_tab, sin_tab):
    tm = PREP_TM
    assert tm == WINDOW
    tok = lambda i: jnp.maximum(i - 1, 0)
    kv_spec = lambda col: pl.BlockSpec((tm, KV_WIDTH), lambda i: (tok(i), col // KV_WIDTH))
    return pl.pallas_call(
        _prep_kernel,
        grid=(SEQ // tm + 1,),
        in_specs=[
            pl.BlockSpec((tm, NSA_WIDTH), lambda i: (tok(i), COL_Q // NSA_WIDTH)),
            kv_spec(COL_KS), kv_spec(COL_VS), kv_spec(COL_KW), kv_spec(COL_VW),
            pl.BlockSpec((tm, HEAD_DIM), lambda i: (tok(i), 0)),
            pl.BlockSpec((tm, HEAD_DIM), lambda i: (tok(i), 0)),
        ],
        out_specs=[
            pl.BlockSpec((N_Q_HEADS, HEAD_DIM, tm), lambda i: (0, 0, tok(i))),
            pl.BlockSpec((N_KV_HEADS, tm, 2 * HEAD_DIM), lambda i: (0, tok(i), 0)),
            pl.BlockSpec((N_KV_HEADS, V_ROWS, tm), lambda i: (0, 0, tok(i))),
            pl.BlockSpec((N_KV_HEADS, tm, 2 * HEAD_DIM), lambda i: (0, i, 0)),
            pl.BlockSpec((N_KV_HEADS, V_ROWS, tm), lambda i: (0, 0, i)),
        ],
        out_shape=[
            jax.ShapeDtypeStruct((N_Q_HEADS, HEAD_DIM, SEQ), BF16),
            jax.ShapeDtypeStruct((N_KV_HEADS, SEQ, 2 * HEAD_DIM), BF16),
            jax.ShapeDtypeStruct((N_KV_HEADS, V_ROWS, SEQ), BF16),
            jax.ShapeDtypeStruct((N_KV_HEADS, WINDOW + SEQ, 2 * HEAD_DIM), BF16),
            jax.ShapeDtypeStruct((N_KV_HEADS, V_ROWS, WINDOW + SEQ), BF16),
        ],
        compiler_params=pltpu.CompilerParams(
            dimension_semantics=("arbitrary",), vmem_limit_bytes=VMEM_LIMIT),
        name="prep",
    )(proj, proj, proj, proj, proj, cos_tab, sin_tab)


def _cmp_kernel(x_ref, pe_ref, w1_ref, w2_ref, cos_ref, sin_ref, o_ref, ot_ref, hi_scr):
    half = CMP_BLOCK // 2
    acc_lo = jnp.zeros((N_CMP_PAD, CMP_HIDDEN), F32)
    acc_hi = jnp.zeros((N_CMP_PAD, CMP_HIDDEN), F32)
    for l in range(half):
        xl = x_ref[pl.ds(l, N_CMP_PAD, stride=CMP_STRIDE), :]
        acc_lo += _dot((xl + pe_ref[0, l:l + 1, :]).astype(BF16), w1_ref[0, l])
        acc_hi += _dot((xl + pe_ref[0, half + l:half + l + 1, :]).astype(BF16), w1_ref[0, half + l])
    hi_scr[0:N_CMP_PAD, :] = acc_hi
    hi_scr[N_CMP_PAD:N_CMP_PAD + 8, :] = jnp.zeros((8, CMP_HIDDEN), F32)
    pre = acc_lo + hi_scr[pl.ds(1, N_CMP_PAD), :]
    hdn = pre * jax.nn.sigmoid(pre)
    out = _dot(hdn.astype(BF16), w2_ref[0])
    out = _rope(out, cos_ref[0], sin_ref[0])
    row = lax.broadcasted_iota(jnp.int32, (N_CMP_PAD, HEAD_DIM), 0)
    out = jnp.where(row < N_CMP, out, 0.0)
    o_ref[0, 0] = out.astype(BF16)
    ot_ref[0, 0] = out.T.astype(BF16)


def _compress(proj, pe, w1, w2, cos_c, sin_c):
    return pl.pallas_call(
        _cmp_kernel,
        grid=(2, N_KV_HEADS),
        in_specs=[
            pl.BlockSpec((SEQ, HEAD_DIM), lambda k, h: (0, COL_KC // HEAD_DIM + N_KV_HEADS * k + h)),
            pl.BlockSpec((1, CMP_BLOCK, HEAD_DIM), lambda k, h: (k, 0, 0)),
            pl.BlockSpec((1, CMP_BLOCK, HEAD_DIM, CMP_HIDDEN), lambda k, h: (k, 0, 0, 0)),
            pl.BlockSpec((1, CMP_HIDDEN, HEAD_DIM), lambda k, h: (k, 0, 0)),
            pl.BlockSpec((1, N_CMP_PAD, HEAD_DIM), lambda k, h: (k, 0, 0)),
            pl.BlockSpec((1, N_CMP_PAD, HEAD_DIM), lambda k, h: (k, 0, 0)),
        ],
        out_specs=[
            pl.BlockSpec((1, 1, N_CMP_PAD, HEAD_DIM), lambda k, h: (k, h, 0, 0)),
            pl.BlockSpec((1, 1, HEAD_DIM, N_CMP_PAD), lambda k, h: (k, h, 0, 0)),
        ],
        out_shape=[
            jax.ShapeDtypeStruct((2, N_KV_HEADS, N_CMP_PAD, HEAD_DIM), BF16),
            jax.ShapeDtypeStruct((2, N_KV_HEADS, HEAD_DIM, N_CMP_PAD), BF16),
        ],
        scratch_shapes=[pltpu.VMEM((N_CMP_PAD + 8, CMP_HIDDEN), F32)],
        compiler_params=pltpu.CompilerParams(
            dimension_semantics=("arbitrary", "arbitrary"), vmem_limit_bytes=VMEM_LIMIT),
        name="compress",
    )(proj, pe, w1, w2, cos_c, sin_c)


IMP_PAD = 8


def _attn_kernel(q_ref, kc_ref, vct_ref, ks_ref, vst_ref, kw_ref, vwt_ref, g_ref, gn_ref,
                 o_ref, imp_scr, sel_scr, oc_scr, ow_scr, s_scr, p_scr, acc_scr):
    i = pl.program_id(1)
    s0 = i * AQ
    nrow = AR

    q_t = jnp.concatenate([q_ref[g] for g in range(GROUP)], axis=1)

    def mask_groups(s, mask):
        return jnp.concatenate(
            [jnp.where(mask, s[:, g * AQ:(g + 1) * AQ], NEG) for g in range(GROUP)], axis=1)

    def compressed(n_chunks):
        rows = 128 * n_chunks
        mrows = min(rows, CMP_MASK_ROWS)
        c0 = rows - mrows
        sc = _dot(kc_ref[0, 0, 0:rows, :], q_t)
        r_i = lax.broadcasted_iota(jnp.int32, (mrows, AQ), 0)
        t_i = lax.broadcasted_iota(jnp.int32, (mrows, AQ), 1)
        visible = CMP_STRIDE * r_i - t_i <= s0 - (CMP_BLOCK - 1) - CMP_STRIDE * c0
        tail = mask_groups(sc[c0:rows], visible)
        sc = tail if c0 == 0 else jnp.concatenate([sc[0:c0], tail], axis=0)
        m_c = jnp.max(sc, axis=0, keepdims=True)
        m_c = jnp.where(m_c > 0.5 * NEG, m_c, 0.0)
        e_c = jnp.exp2(sc - m_c)
        l_c = jnp.sum(e_c, axis=0, keepdims=True)
        p_c = e_c * (1.0 / jnp.where(l_c > 0.0, l_c, 1.0))
        oc_scr[...] = _dot(vct_ref[0, 0, :, 0:rows], p_c.astype(BF16))
        imp_c = (p_c[:, 0:AQ] + p_c[:, AQ:2 * AQ]) + (p_c[:, 2 * AQ:3 * AQ] + p_c[:, 3 * AQ:4 * AQ])
        rest = IMP_PAD + N_CMP_PAD + 8 - (IMP_PAD + rows)
        for a in range(AQ // 128):
            imp_scr[a, IMP_PAD:IMP_PAD + rows, :] = imp_c[:, a * 128:(a + 1) * 128]
            imp_scr[a, IMP_PAD + rows:IMP_PAD + rows + rest, :] = jnp.zeros((rest, 128), F32)

    for a in range(AQ // 128):
        imp_scr[a, 0:IMP_PAD, :] = jnp.zeros((IMP_PAD, 128), F32)
    n_cmp_chunks = i // (128 * CMP_STRIDE // AQ) + 1
    for n in range(1, N_CMP_PAD // 128 + 1):
        pl.when(n_cmp_chunks == n)(functools.partial(compressed, n))

    ratio = SEL_BLOCK // CMP_STRIDE
    tap = lambda k: jnp.concatenate(
        [imp_scr[a, pl.ds(IMP_PAD + k, N_BLK, stride=ratio), :] for a in range(AQ // 128)], axis=1)
    imp = tap(-1) + 2.0 * (tap(0) + tap(1) + tap(2)) + tap(3)

    jidx = lax.broadcasted_iota(jnp.int32, (N_BLK, AQ), 0).astype(F32)
    tok_l = lax.broadcasted_iota(jnp.int32, (1, AQ), 1)
    jt = (i * (AQ // SEL_BLOCK) + tok_l // SEL_BLOCK).astype(F32)
    v = jnp.where(jidx == 0.0, REMOVED, imp)
    v = jnp.where(jidx == jt, REMOVED, v)
    v = jnp.where(jidx == jt - 1.0, REMOVED, v)
    v = jnp.where(jidx > jt, NEG, v)
    for _ in range(N_SEL - 3):
        v = jnp.where(v == jnp.max(v, axis=0, keepdims=True), REMOVED, v)
    sel_fast = jnp.where(jidx > jt, 0.0, jnp.where(v == REMOVED, 1.0, 0.0))
    n_picked = jnp.sum(sel_fast, axis=0, keepdims=True)
    miscount = jnp.max(jnp.abs(n_picked - jnp.minimum(jt + 1.0, float(N_SEL))))
    sel_scr[...] = sel_fast

    flag_pen = jnp.where(lax.broadcasted_iota(jnp.int32, (HEAD_DIM, nrow), 0) == 0, NEG, 0.0).astype(BF16)
    q_win = jnp.concatenate([q_t, flag_pen], axis=0)
    wk0 = pl.multiple_of(s0, AQ)
    sw = _dot(kw_ref[0, pl.ds(wk0, WIN_KEYS), :], q_win).astype(BF16)
    r_i = lax.broadcasted_iota(jnp.int32, (AQ, AQ), 0)
    t_i = lax.broadcasted_iota(jnp.int32, (AQ, AQ), 1)
    tri_old = jnp.concatenate([jnp.where(r_i > t_i, 0.0, NEG).astype(BF16)] * GROUP, axis=1)
    tri_causal = jnp.concatenate([jnp.where(r_i <= t_i, 0.0, NEG).astype(BF16)] * GROUP, axis=1)
    sw = jnp.concatenate([sw[0:AQ] + tri_old, sw[AQ:WINDOW], sw[WINDOW:WIN_KEYS] + tri_causal], axis=0)
    m_w = jnp.max(sw, axis=0, keepdims=True)
    p_w = jnp.exp2(sw - m_w)
    ow_scr[...] = _dot(vwt_ref[0, :, pl.ds(wk0, WIN_KEYS)], p_w)

    @pl.when(miscount > 0.5)
    def _():
        w = jnp.where(jidx == 0.0, FORCE, imp)
        w = jnp.where(jidx == jt, FORCE, w)
        w = jnp.where(jidx == jt - 1.0, FORCE, w)
        w = jnp.where(jidx > jt, NEG, w)
        sel = jnp.zeros((N_BLK, AQ), F32)
        for _ in range(N_SEL):
            mx = jnp.max(w, axis=0, keepdims=True)
            cand = jnp.where(w == mx, jidx, float(N_BLK))
            pick = jidx == jnp.min(cand, axis=0, keepdims=True)
            sel = jnp.where(pick, 1.0, sel)
            w = jnp.where(pick, REMOVED, w)
        sel_scr[...] = jnp.where(jidx > jt, 0.0, sel)

    pen = jnp.where(sel_scr[...] > 0.5, 0.0, NEG).astype(BF16)
    q_aug = jnp.concatenate([q_t, jnp.concatenate([pen] * GROUP, axis=1)], axis=0)

    def key_start(kt):
        return pl.multiple_of(kt * SEL_TK, SEL_TK)

    def scores(kt):
        return _dot(ks_ref[0, pl.ds(key_start(kt), SEL_TK), :], q_aug).astype(BF16)

    def col_max(s):
        return jnp.max(s, axis=0, keepdims=True).astype(F32)

    def add_pv(kt, p, alpha):
        pv = _dot(vst_ref[0, :, pl.ds(key_start(kt), SEL_TK)], p)
        acc_scr[...] = alpha * acc_scr[...] + pv

    def body(kt, carry):
        m, m_tile, alpha_prev = carry
        p_prev = p_scr[...]
        s_cur = s_scr[...]
        s_nxt = scores(kt + 1)
        m_nxt = col_max(s_nxt)
        add_pv(jnp.maximum(kt - 1, 0), p_prev, alpha_prev)
        m_new = jnp.maximum(m, m_tile)
        p_scr[...] = jnp.exp2(s_cur - m_new.astype(BF16))
        s_scr[...] = s_nxt
        return m_new, m_nxt, jnp.exp2(m - m_new)

    n_full = (i * AQ) // SEL_TK
    s_first = scores(0)
    s_scr[...] = s_first
    p_scr[...] = jnp.zeros((SEL_TK, nrow), BF16)
    acc_scr[...] = jnp.zeros((V_ROWS, nrow), F32)
    init = (jnp.full((1, nrow), NEG, F32), col_max(s_first), jnp.ones((1, nrow), F32))
    carry = lax.fori_loop(0, n_full // 2, lambda j, c: body(2 * j + 1, body(2 * j, c)), init)
    m_s, _, alpha_last = lax.cond(n_full % 2 == 1, lambda c: body(n_full - 1, c), lambda c: c, carry)
    add_pv(jnp.maximum(n_full - 1, 0), p_scr[...], alpha_last)
    off = pl.multiple_of(s0 - n_full * SEL_TK, AQ)
    s_scr[pl.ds(off, AQ), :] = s_scr[pl.ds(off, AQ), :] + tri_causal
    s_diag = s_scr[...]
    m_fin = jnp.maximum(m_s, col_max(s_diag))
    add_pv(n_full, jnp.exp2(s_diag - m_fin.astype(BF16)), jnp.exp2(m_s - m_fin))

    gt = g_ref[...].T
    grow = lambda c: jnp.concatenate([gt[c * GROUP + g:c * GROUP + g + 1, :] for g in range(GROUP)], axis=1)
    l_s = acc_scr[HEAD_DIM:HEAD_DIM + 1, :]
    l_w = ow_scr[HEAD_DIM:HEAD_DIM + 1, :]
    out_t = (oc_scr[...] * grow(0) + acc_scr[0:HEAD_DIM, :] * (grow(1) / l_s)
             + ow_scr[0:HEAD_DIM, :] * (grow(2) / l_w))
    out = jnp.concatenate([out_t[:, g * AQ:(g + 1) * AQ].T for g in range(GROUP)], axis=1)
    gn = gn_ref[...]
    o_ref[...] = (out * (gn * jax.nn.sigmoid(gn))).astype(BF16)


def _attention(q_r, kc, vct, ks_aug, vst, kw_aug, vwt, gates, proj):
    head_res = lambda shape: pl.BlockSpec((1,) + shape, lambda h, i: (h, 0, 0))
    return pl.pallas_call(
        _attn_kernel,
        grid=(N_KV_HEADS, SEQ // AQ),
        in_specs=[
            pl.BlockSpec((GROUP, HEAD_DIM, AQ), lambda h, i: (h, 0, i)),
            pl.BlockSpec((1, 1, N_CMP_PAD, HEAD_DIM), lambda h, i: (0, h, 0, 0)),
            pl.BlockSpec((1, 1, HEAD_DIM, N_CMP_PAD), lambda h, i: (1, h, 0, 0)),
            head_res((SEQ, 2 * HEAD_DIM)),
            head_res((V_ROWS, SEQ)),
            head_res((WINDOW + SEQ, 2 * HEAD_DIM)),
            head_res((V_ROWS, WINDOW + SEQ)),
            pl.BlockSpec((AQ, 128), lambda h, i: (i, h)),
            pl.BlockSpec((AQ, GROUP * HEAD_DIM), lambda h, i: (i, COL_GN // (GROUP * HEAD_DIM) + h)),
        ],
        out_specs=pl.BlockSpec((AQ, GROUP * HEAD_DIM), lambda h, i: (i, h)),
        out_shape=jax.ShapeDtypeStruct((SEQ, NSA_WIDTH), BF16),
        scratch_shapes=[pltpu.VMEM((AQ // 128, IMP_PAD + N_CMP_PAD + 8, 128), F32),
                        pltpu.VMEM((N_BLK, AQ), F32),
                        pltpu.VMEM((HEAD_DIM, AR), F32),
                        pltpu.VMEM((V_ROWS, AR), F32),
                        pltpu.VMEM((SEL_TK, AR), BF16),
                        pltpu.VMEM((SEL_TK, AR), BF16),
                        pltpu.VMEM((V_ROWS, AR), F32)],
        compiler_params=pltpu.CompilerParams(
            dimension_semantics=("arbitrary", "arbitrary"), vmem_limit_bytes=VMEM_LIMIT),
        name="attn",
    )(q_r, kc, vct, ks_aug, vst, kw_aug, vwt, gates, proj)


def _pool_kernel(u_ref, up_ref, gp_ref, mix_ref, sc_ref, wpo_ref, ya_ref, ext_scr):
    i = pl.program_id(0)
    tm = POOL_TM
    ext_scr[0:POOL_HALO, :] = jnp.where(i > 0, up_ref[...], 0.0)
    ext_scr[POOL_HALO:POOL_HALO + tm, :] = u_ref[...]
    t1 = i * tm + lax.broadcasted_iota(jnp.int32, (tm, 1), 0) + 1
    zs = []
    for g, w in enumerate(POOL_WINDOWS):
        cs = slice(g * POOL_GROUP, (g + 1) * POOL_GROUP)
        u = ext_scr[POOL_HALO:POOL_HALO + tm, cs]
        acc = u
        for k in range(1, w):
            acc = acc + ext_scr[POOL_HALO - k:POOL_HALO - k + tm, cs]
        cnt = jnp.minimum(t1, w).astype(F32)
        pooled = acc / cnt - u
        zs.append(_dot(pooled.astype(BF16), mix_ref[g]))
    z = jnp.concatenate(zs, axis=1) * sc_ref[...]
    gp = gp_ref[...]
    y = z * (gp * jax.nn.sigmoid(gp))
    ya_ref[...] = _dot(y.astype(BF16), wpo_ref[...])


def _pool(proj, mix, scale, w_pool_out):
    tm = POOL_TM
    return pl.pallas_call(
        _pool_kernel,
        grid=(SEQ // tm,),
        in_specs=[
            pl.BlockSpec((tm, POOL_WIDTH), lambda i: (i, COL_U // POOL_WIDTH)),
            pl.BlockSpec((POOL_HALO, POOL_WIDTH),
                         lambda i: (jnp.maximum(i * (tm // POOL_HALO) - 1, 0), COL_U // POOL_WIDTH)),
            pl.BlockSpec((tm, POOL_WIDTH), lambda i: (i, COL_GP // POOL_WIDTH)),
            pl.BlockSpec((len(POOL_WINDOWS), POOL_GROUP, POOL_GROUP), lambda i: (0, 0, 0)),
            pl.BlockSpec((1, POOL_WIDTH), lambda i: (0, 0)),
            pl.BlockSpec((POOL_WIDTH, D_MODEL), lambda i: (0, 0)),
        ],
        out_specs=pl.BlockSpec((tm, D_MODEL), lambda i: (i, 0)),
        out_shape=jax.ShapeDtypeStruct((SEQ, D_MODEL), F32),
        scratch_shapes=[pltpu.VMEM((POOL_HALO + tm, POOL_WIDTH), F32)],
        compiler_params=pltpu.CompilerParams(
            dimension_semantics=("arbitrary",), vmem_limit_bytes=VMEM_LIMIT),
        name="pool",
    )(proj, proj, proj, mix, scale, w_pool_out)


def _out_kernel(x_ref, ya_ref, o_ref, gm0_ref, gm1_ref, wn_ref, wo_ref, fw_ref, out_ref):
    y_b = _dot(o_ref[...], wn_ref[...])
    merged = gm0_ref[...] * ya_ref[...] + gm1_ref[...] * y_b
    r = x_ref[...] + _dot(merged.astype(BF16), wo_ref[...])
    ms = jnp.mean(r * r, axis=-1, keepdims=True)
    out_ref[...] = r * lax.rsqrt(ms + EPS) * fw_ref[...]


def _output(x2, y_a, o_gated, proj, w_nsa_out, w_out, final_norm_w):
    tm = OUT_TM
    row = lambda: pl.BlockSpec((tm, D_MODEL), lambda i: (i, 0))
    resident = lambda: pl.BlockSpec((D_MODEL, D_MODEL), lambda i: (0, 0), pipeline_mode=pl.Buffered(1))
    return pl.pallas_call(
        _out_kernel,
        grid=(SEQ // tm,),
        in_specs=[
            row(), row(), row(),
            pl.BlockSpec((tm, D_MODEL), lambda i: (i, 0)),
            pl.BlockSpec((tm, D_MODEL), lambda i: (i, 1)),
            resident(), resident(),
            pl.BlockSpec((1, D_MODEL), lambda i: (0, 0)),
        ],
        out_specs=row(),
        out_shape=jax.ShapeDtypeStruct((SEQ, D_MODEL), F32),
        compiler_params=pltpu.CompilerParams(
            dimension_semantics=("arbitrary",), vmem_limit_bytes=VMEM_LIMIT),
        name="out",
    )(x2, y_a, o_gated, proj, proj, w_nsa_out, w_out, final_norm_w)


def _rope_tables():
    half = HEAD_DIM // 2
    inv = ROPE_THETA ** (-np.arange(half, dtype=np.float64) / half)
    ang = np.arange(SEQ, dtype=np.float64)[:, None] * inv[None, :]
    cos = np.concatenate([np.cos(ang), np.cos(ang)], axis=-1).astype(np.float32)
    sin = np.concatenate([-np.sin(ang), np.sin(ang)], axis=-1).astype(np.float32)
    pad = lambda t: np.pad(t[CMP_BLOCK - 1::CMP_STRIDE], ((0, N_CMP_PAD - N_CMP), (0, 0)))
    cos_c = np.stack([pad(cos), np.ones((N_CMP_PAD, HEAD_DIM), np.float32)])
    sin_c = np.stack([pad(sin), np.zeros((N_CMP_PAD, HEAD_DIM), np.float32)])
    return cos, sin, cos_c, sin_c


def _gate_weight(w_in_l):
    wg = w_in_l[:, W_IN_MAIN:].reshape(D_MODEL, N_KV_HEADS, GROUP, 3)
    wg = wg.transpose(0, 1, 3, 2).reshape(D_MODEL, N_KV_HEADS, 3 * GROUP)
    wg = jnp.pad(wg, ((0, 0), (0, 0), (0, 128 - 3 * GROUP)))
    return wg.reshape(D_MODEL, N_KV_HEADS * 128).astype(BF16)


def kernel(x, norm_w, w_in, pool_mix, pool_scale, cmp_pe_k, cmp_w1_k, cmp_w2_k, cmp_pe_v, cmp_w1_v,
           cmp_w2_v, w_pool_out, w_nsa_out, w_merge, b_merge, w_out, final_norm_w):
    assert x.shape == (1, SEQ, D_MODEL) and norm_w.shape[0] == 1
    x2 = x[0]
    w_cat = jnp.concatenate([w_merge[0], w_in[0, :, :W_IN_MAIN]], axis=1).astype(BF16)
    proj, gates = _proj(x2, norm_w, w_cat, _gate_weight(w_in[0]), b_merge)

    cos_tab, sin_tab, cos_c, sin_c = _rope_tables()
    q_r, ks_aug, vst, kw_aug, vwt = _prep(proj, cos_tab, sin_tab)

    pe = jnp.stack([cmp_pe_k[0], cmp_pe_v[0]])
    w1 = jnp.stack([cmp_w1_k[0], cmp_w1_v[0]]).astype(BF16)
    w2 = jnp.stack([cmp_w2_k[0], cmp_w2_v[0]]).astype(BF16)
    cmp_n, cmp_t = _compress(proj, pe, w1, w2, cos_c, sin_c)

    o_gated = _attention(q_r, cmp_n, cmp_t, ks_aug, vst, kw_aug, vwt, gates, proj)

    y_a = _pool(proj, pool_mix[0].astype(BF16), pool_scale, w_pool_out[0].astype(BF16))
    out = _output(x2, y_a, o_gated, proj, w_nsa_out[0].astype(BF16), w_out[0].astype(BF16),
                  final_norm_w[None, :])
    return out[None]
```

```python
import functools
import math

import numpy as np
import jax
import jax.numpy as jnp
from jax import lax
from jax.experimental import pallas as pl
from jax.experimental.pallas import tpu as pltpu

F32 = jnp.float32
BF16 = jnp.bfloat16

D_MODEL = 2048
SEQ = 8192
POOL_WIDTH = 1024
POOL_WINDOWS = (2, 4, 8, 16)
POOL_GROUP = 256
POOL_HALO = 16
N_Q_HEADS = 16
N_KV_HEADS = 4
GROUP = 4
HEAD_DIM = 128
NSA_WIDTH = 2048
KV_WIDTH = 512
CMP_BLOCK = 32
CMP_STRIDE = 16
CMP_HIDDEN = 256
N_CMP = (SEQ - CMP_BLOCK) // CMP_STRIDE + 1
N_CMP_PAD = 512
SEL_BLOCK = 64
N_BLK = SEQ // SEL_BLOCK
N_SEL = 16
WINDOW = 512
Q_BLOCK = 128
N_QB = SEQ // Q_BLOCK
ROPE_THETA = 10000.0
EPS = 1e-6
NEG = -1e30
FORCE = 1e6
REMOVED = -3e38
LOG2E = 1.4426950408889634
Q_SCALE = HEAD_DIM ** -0.5 * LOG2E

COL_GM = 0
COL_U = 4096
COL_GP = 5120
COL_Q = 6144
COL_KC = 8192
COL_VC = 8704
COL_KS = 9216
COL_VS = 9728
COL_KW = 10240
COL_VW = 10752
COL_GN = 11264
N_PROJ = 13312
W_IN_MAIN = 9216

VMEM_LIMIT = 56 * 1024 * 1024

PROJ_TM = 1024
PROJ_TN = 1024
N_GM_TILES = 4096 // PROJ_TN
PREP_TM = 512
SEL_TK = 512
AQ = 256
AR = GROUP * AQ
WIN_KEYS = WINDOW + AQ
CMP_MASK_ROWS = 256
V_ROWS = HEAD_DIM + 16
POOL_TM = 512
OUT_TM = 256


def _nt_dot(a, b):
    return lax.dot_general(a, b, (((1,), (1,)), ((), ())), preferred_element_type=F32)


def _dot(a, b):
    return jnp.dot(a, b, preferred_element_type=F32)


def _proj_kernel(x_ref, nw_ref, wm_ref, wi_ref, wg_ref, b_ref, o_ref, g_ref, h_scr):
    j = pl.program_id(1)

    @pl.when(j == 0)
    def _():
        x = x_ref[...]
        ms = jnp.mean(x * x, axis=-1, keepdims=True)
        hb = (x * lax.rsqrt(ms + EPS) * nw_ref[...]).astype(BF16)
        h_scr[...] = hb
        g_ref[...] = jax.nn.sigmoid(_dot(hb, wg_ref[...]))

    @pl.when(j < N_GM_TILES)
    def _():
        o_ref[...] = jax.nn.sigmoid(_dot(h_scr[...], wm_ref[...]) + b_ref[...])

    @pl.when(j >= N_GM_TILES)
    def _():
        o_ref[...] = _dot(h_scr[...], wi_ref[...])


def _proj(x2, norm_w, w_merge, w_in, w_gate, b_merge):
    grid = (SEQ // PROJ_TM, N_PROJ // PROJ_TN)
    return pl.pallas_call(
        _proj_kernel,
        grid=grid,
        in_specs=[
            pl.BlockSpec((PROJ_TM, D_MODEL), lambda i, j: (i, 0)),
            pl.BlockSpec((1, D_MODEL), lambda i, j: (0, 0)),
            pl.BlockSpec((D_MODEL, PROJ_TN), lambda i, j: (0, jnp.minimum(j, N_GM_TILES - 1))),
            pl.BlockSpec((D_MODEL, PROJ_TN), lambda i, j: (0, jnp.maximum(j - N_GM_TILES, 0))),
            pl.BlockSpec((D_MODEL, N_KV_HEADS * 128), lambda i, j: (0, 0)),
            pl.BlockSpec((1, PROJ_TN), lambda i, j: (0, jnp.minimum(j, N_GM_TILES - 1))),
        ],
        out_specs=[
            pl.BlockSpec((PROJ_TM, PROJ_TN), lambda i, j: (i, j)),
            pl.BlockSpec((PROJ_TM, N_KV_HEADS * 128), lambda i, j: (i, 0)),
        ],
        out_shape=[
            jax.ShapeDtypeStruct((SEQ, N_PROJ), F32),
            jax.ShapeDtypeStruct((SEQ, N_KV_HEADS * 128), F32),
        ],
        scratch_shapes=[pltpu.VMEM((PROJ_TM, D_MODEL), BF16)],
        compiler_params=pltpu.CompilerParams(
            dimension_semantics=("arbitrary", "arbitrary"), vmem_limit_bytes=VMEM_LIMIT),
        name="proj",
    )(x2, norm_w, w_merge, w_in, w_gate, b_merge)


def _rope(x, cos, sin_signed):
    return x * cos + pltpu.roll(x, HEAD_DIM // 2, 1) * sin_signed


def _prep_kernel(q_ref, ks_ref, vs_ref, kw_ref, vw_ref, cos_ref, sin_ref,
                 qo_ref, kso_ref, vso_ref, kwo_ref, vwo_ref):
    i = pl.program_id(0)
    t0 = jnp.maximum(i - 1, 0) * PREP_TM
    cos = cos_ref[...]
    sin = sin_ref[...]
    for hq in range(N_Q_HEADS):
        sl = slice(hq * HEAD_DIM, (hq + 1) * HEAD_DIM)
        qo_ref[hq] = (_rope(q_ref[:, sl], cos, sin) * Q_SCALE).T.astype(BF16)
    blk = (t0 + lax.broadcasted_iota(jnp.int32, (PREP_TM, N_BLK), 0)) // SEL_BLOCK
    lane = lax.broadcasted_iota(jnp.int32, (PREP_TM, N_BLK), 1)
    onehot = jnp.where(blk == lane, 1.0, 0.0).astype(BF16)
    pad_flag = jnp.where(lane == 0, 1.0, 0.0)
    is_pad = i == 0
    ones_rows = jnp.where(lax.broadcasted_iota(jnp.int32, (V_ROWS - HEAD_DIM, PREP_TM), 0) == 0,
                          1.0, 0.0).astype(BF16)
    for h in range(N_KV_HEADS):
        sl = slice(h * HEAD_DIM, (h + 1) * HEAD_DIM)
        kso_ref[h, :, 0:HEAD_DIM] = _rope(ks_ref[:, sl], cos, sin).astype(BF16)
        kso_ref[h, :, HEAD_DIM:2 * HEAD_DIM] = onehot
        vso_ref[h, 0:HEAD_DIM, :] = vs_ref[:, sl].T.astype(BF16)
        vso_ref[h, HEAD_DIM:V_ROWS, :] = ones_rows
        kwo_ref[h, :, 0:HEAD_DIM] = jnp.where(is_pad, 0.0, _rope(kw_ref[:, sl], cos, sin)).astype(BF16)
        kwo_ref[h, :, HEAD_DIM:2 * HEAD_DIM] = jnp.where(is_pad, pad_flag, 0.0).astype(BF16)
        vwo_ref[h, 0:HEAD_DIM, :] = jnp.where(is_pad, 0.0, vw_ref[:, sl].T).astype(BF16)
        vwo_ref[h, HEAD_DIM:V_ROWS, :] = ones_rows


def _prep(proj, cos_tab, sin_tab):
    tm = PREP_TM
    assert tm == WINDOW
    tok = lambda i: jnp.maximum(i - 1, 0)
    kv_spec = lambda col: pl.BlockSpec((tm, KV_WIDTH), lambda i: (tok(i), col // KV_WIDTH))
    return pl.pallas_call(
        _prep_kernel,
        grid=(SEQ // tm + 1,),
        in_specs=[
            pl.BlockSpec((tm, NSA_WIDTH), lambda i: (tok(i), COL_Q // NSA_WIDTH)),
            kv_spec(COL_KS), kv_spec(COL_VS), kv_spec(COL_KW), kv_spec(COL_VW),
            pl.BlockSpec((tm, HEAD_DIM), lambda i: (tok(i), 0)),
            pl.BlockSpec((tm, HEAD_DIM), lambda i: (tok(i), 0)),
        ],
        out_specs=[
            pl.BlockSpec((N_Q_HEADS, HEAD_DIM, tm), lambda i: (0, 0, tok(i))),
            pl.BlockSpec((N_KV_HEADS, tm, 2 * HEAD_DIM), lambda i: (0, tok(i), 0)),
            pl.BlockSpec((N_KV_HEADS, V_ROWS, tm), lambda i: (0, 0, tok(i))),
            pl.BlockSpec((N_KV_HEADS, tm, 2 * HEAD_DIM), lambda i: (0, i, 0)),
            pl.BlockSpec((N_KV_HEADS, V_ROWS, tm), lambda i: (0, 0, i)),
        ],
        out_shape=[
            jax.ShapeDtypeStruct((N_Q_HEADS, HEAD_DIM, SEQ), BF16),
            jax.ShapeDtypeStruct((N_KV_HEADS, SEQ, 2 * HEAD_DIM), BF16),
            jax.ShapeDtypeStruct((N_KV_HEADS, V_ROWS, SEQ), BF16),
            jax.ShapeDtypeStruct((N_KV_HEADS, WINDOW + SEQ, 2 * HEAD_DIM), BF16),
            jax.ShapeDtypeStruct((N_KV_HEADS, V_ROWS, WINDOW + SEQ), BF16),
        ],
        compiler_params=pltpu.CompilerParams(
            dimension_semantics=("arbitrary",), vmem_limit_bytes=VMEM_LIMIT),
        name="prep",
    )(proj, proj, proj, proj, proj, cos_tab, sin_tab)


def _cmp_kernel(x_ref, pe_ref, w1_ref, w2_ref, cos_ref, sin_ref, o_ref, ot_ref, hi_scr):
    half = CMP_BLOCK // 2
    acc_lo = jnp.zeros((N_CMP_PAD, CMP_HIDDEN), F32)
    acc_hi = jnp.zeros((N_CMP_PAD, CMP_HIDDEN), F32)
    for l in range(half):
        xl = x_ref[pl.ds(l, N_CMP_PAD, stride=CMP_STRIDE), :]
        acc_lo += _dot((xl + pe_ref[0, l:l + 1, :]).astype(BF16), w1_ref[0, l])
        acc_hi += _dot((xl + pe_ref[0, half + l:half + l + 1, :]).astype(BF16), w1_ref[0, half + l])
    hi_scr[0:N_CMP_PAD, :] = acc_hi
    hi_scr[N_CMP_PAD:N_CMP_PAD + 8, :] = jnp.zeros((8, CMP_HIDDEN), F32)
    pre = acc_lo + hi_scr[pl.ds(1, N_CMP_PAD), :]
    hdn = pre * jax.nn.sigmoid(pre)
    out = _dot(hdn.astype(BF16), w2_ref[0])
    out = _rope(out, cos_ref[0], sin_ref[0])
    row = lax.broadcasted_iota(jnp.int32, (N_CMP_PAD, HEAD_DIM), 0)
    out = jnp.where(row < N_CMP, out, 0.0)
    o_ref[0, 0] = out.astype(BF16)
    ot_ref[0, 0] = out.T.astype(BF16)


def _compress(proj, pe, w1, w2, cos_c, sin_c):
    return pl.pallas_call(
        _cmp_kernel,
        grid=(2, N_KV_HEADS),
        in_specs=[
            pl.BlockSpec((SEQ, HEAD_DIM), lambda k, h: (0, COL_KC // HEAD_DIM + N_KV_HEADS * k + h)),
            pl.BlockSpec((1, CMP_BLOCK, HEAD_DIM), lambda k, h: (k, 0, 0)),
            pl.BlockSpec((1, CMP_BLOCK, HEAD_DIM, CMP_HIDDEN), lambda k, h: (k, 0, 0, 0)),
            pl.BlockSpec((1, CMP_HIDDEN, HEAD_DIM), lambda k, h: (k, 0, 0)),
            pl.BlockSpec((1, N_CMP_PAD, HEAD_DIM), lambda k, h: (k, 0, 0)),
            pl.BlockSpec((1, N_CMP_PAD, HEAD_DIM), lambda k, h: (k, 0, 0)),
        ],
        out_specs=[
            pl.BlockSpec((1, 1, N_CMP_PAD, HEAD_DIM), lambda k, h: (k, h, 0, 0)),
            pl.BlockSpec((1, 1, HEAD_DIM, N_CMP_PAD), lambda k, h: (k, h, 0, 0)),
        ],
        out_shape=[
            jax.ShapeDtypeStruct((2, N_KV_HEADS, N_CMP_PAD, HEAD_DIM), BF16),
            jax.ShapeDtypeStruct((2, N_KV_HEADS, HEAD_DIM, N_CMP_PAD), BF16),
        ],
        scratch_shapes=[pltpu.VMEM((N_CMP_PAD + 8, CMP_HIDDEN), F32)],
        compiler_params=pltpu.CompilerParams(
            dimension_semantics=("arbitrary", "arbitrary"), vmem_limit_bytes=VMEM_LIMIT),
        name="compress",
    )(proj, pe, w1, w2, cos_c, sin_c)


IMP_PAD = 8


def _attn_kernel(q_ref, kc_ref, vct_ref, ks_ref, vst_ref, kw_ref, vwt_ref, g_ref, gn_ref,
                 o_ref, imp_scr, sel_scr, oc_scr, ow_scr, s_scr, p_scr, acc_scr):
    i = pl.program_id(1)
    s0 = i * AQ
    nrow = AR

    q_t = jnp.concatenate([q_ref[g] for g in range(GROUP)], axis=1)

    def mask_groups(s, mask):
        return jnp.concatenate(
            [jnp.where(mask, s[:, g * AQ:(g + 1) * AQ], NEG) for g in range(GROUP)], axis=1)

    def compressed(n_chunks):
        rows = 128 * n_chunks
        mrows = min(rows, CMP_MASK_ROWS)
        c0 = rows - mrows
        sc = _dot(kc_ref[0, 0, 0:rows, :], q_t)
        r_i = lax.broadcasted_iota(jnp.int32, (mrows, AQ), 0)
        t_i = lax.broadcasted_iota(jnp.int32, (mrows, AQ), 1)
        visible = CMP_STRIDE * r_i - t_i <= s0 - (CMP_BLOCK - 1) - CMP_STRIDE * c0
        tail = mask_groups(sc[c0:rows], visible)
        sc = tail if c0 == 0 else jnp.concatenate([sc[0:c0], tail], axis=0)
        m_c = jnp.max(sc, axis=0, keepdims=True)
        m_c = jnp.where(m_c > 0.5 * NEG, m_c, 0.0)
        e_c = jnp.exp2(sc - m_c)
        l_c = jnp.sum(e_c, axis=0, keepdims=True)
        p_c = e_c * (1.0 / jnp.where(l_c > 0.0, l_c, 1.0))
        oc_scr[...] = _dot(vct_ref[0, 0, :, 0:rows], p_c.astype(BF16))
        imp_c = (p_c[:, 0:AQ] + p_c[:, AQ:2 * AQ]) + (p_c[:, 2 * AQ:3 * AQ] + p_c[:, 3 * AQ:4 * AQ])
        rest = IMP_PAD + N_CMP_PAD + 8 - (IMP_PAD + rows)
        for a in range(AQ // 128):
            imp_scr[a, IMP_PAD:IMP_PAD + rows, :] = imp_c[:, a * 128:(a + 1) * 128]
            imp_scr[a, IMP_PAD + rows:IMP_PAD + rows + rest, :] = jnp.zeros((rest, 128), F32)

    for a in range(AQ // 128):
        imp_scr[a, 0:IMP_PAD, :] = jnp.zeros((IMP_PAD, 128), F32)
    n_cmp_chunks = i // (128 * CMP_STRIDE // AQ) + 1
    for n in range(1, N_CMP_PAD // 128 + 1):
        pl.when(n_cmp_chunks == n)(functools.partial(compressed, n))

    ratio = SEL_BLOCK // CMP_STRIDE
    tap = lambda k: jnp.concatenate(
        [imp_scr[a, pl.ds(IMP_PAD + k, N_BLK, stride=ratio), :] for a in range(AQ // 128)], axis=1)
    imp = tap(-1) + 2.0 * (tap(0) + tap(1) + tap(2)) + tap(3)

    jidx = lax.broadcasted_iota(jnp.int32, (N_BLK, AQ), 0).astype(F32)
    tok_l = lax.broadcasted_iota(jnp.int32, (1, AQ), 1)
    jt = (i * (AQ // SEL_BLOCK) + tok_l // SEL_BLOCK).astype(F32)
    v = jnp.where(jidx == 0.0, REMOVED, imp)
    v = jnp.where(jidx == jt, REMOVED, v)
    v = jnp.where(jidx == jt - 1.0, REMOVED, v)
    v = jnp.where(jidx > jt, NEG, v)
    for _ in range(N_SEL - 3):
        v = jnp.where(v == jnp.max(v, axis=0, keepdims=True), REMOVED, v)
    sel_fast = jnp.where(jidx > jt, 0.0, jnp.where(v == REMOVED, 1.0, 0.0))
    n_picked = jnp.sum(sel_fast, axis=0, keepdims=True)
    miscount = jnp.max(jnp.abs(n_picked - jnp.minimum(jt + 1.0, float(N_SEL))))
    sel_scr[...] = sel_fast

    flag_pen = jnp.where(lax.broadcasted_iota(jnp.int32, (HEAD_DIM, nrow), 0) == 0, NEG, 0.0).astype(BF16)
    q_win = jnp.concatenate([q_t, flag_pen], axis=0)
    wk0 = pl.multiple_of(s0, AQ)
    sw = _dot(kw_ref[0, pl.ds(wk0, WIN_KEYS), :], q_win).astype(BF16)
    r_i = lax.broadcasted_iota(jnp.int32, (AQ, AQ), 0)
    t_i = lax.broadcasted_iota(jnp.int32, (AQ, AQ), 1)
    tri_old = jnp.concatenate([jnp.where(r_i > t_i, 0.0, NEG).astype(BF16)] * GROUP, axis=1)
    tri_causal = jnp.concatenate([jnp.where(r_i <= t_i, 0.0, NEG).astype(BF16)] * GROUP, axis=1)
    sw = jnp.concatenate([sw[0:AQ] + tri_old, sw[AQ:WINDOW], sw[WINDOW:WIN_KEYS] + tri_causal], axis=0)
    m_w = jnp.max(sw, axis=0, keepdims=True)
    p_w = jnp.exp2(sw - m_w)
    ow_scr[...] = _dot(vwt_ref[0, :, pl.ds(wk0, WIN_KEYS)], p_w)

    s_scr[...] = _dot(ks_ref[0, 0:SEL_TK, 0:HEAD_DIM], q_t).astype(BF16)

    @pl.when(miscount > 0.5)
    def _():
        w = jnp.where(jidx == 0.0, FORCE, imp)
        w = jnp.where(jidx == jt, FORCE, w)
        w = jnp.where(jidx == jt - 1.0, FORCE, w)
        w = jnp.where(jidx > jt, NEG, w)
        sel = jnp.zeros((N_BLK, AQ), F32)
        for _ in range(N_SEL):
            mx = jnp.max(w, axis=0, keepdims=True)
            cand = jnp.where(w == mx, jidx, float(N_BLK))
            pick = jidx == jnp.min(cand, axis=0, keepdims=True)
            sel = jnp.where(pick, 1.0, sel)
            w = jnp.where(pick, REMOVED, w)
        sel_scr[...] = jnp.where(jidx > jt, 0.0, sel)

    pen = jnp.where(sel_scr[...] > 0.5, 0.0, NEG).astype(BF16)
    q_aug = jnp.concatenate([q_t, jnp.concatenate([pen] * GROUP, axis=1)], axis=0)

    def key_start(kt):
        return pl.multiple_of(kt * SEL_TK, SEL_TK)

    def scores(kt):
        return _dot(ks_ref[0, pl.ds(key_start(kt), SEL_TK), :], q_aug).astype(BF16)

    def col_max(s):
        return jnp.max(s, axis=0, keepdims=True).astype(F32)

    def add_pv(kt, p, alpha):
        pv = _dot(vst_ref[0, :, pl.ds(key_start(kt), SEL_TK)], p)
        acc_scr[...] = alpha * acc_scr[...] + pv

    def body(kt, carry):
        m, m_tile, alpha_prev = carry
        p_prev = p_scr[...]
        s_cur = s_scr[...]
        s_nxt = scores(kt + 1)
        m_nxt = col_max(s_nxt)
        add_pv(jnp.maximum(kt - 1, 0), p_prev, alpha_prev)
        m_new = jnp.maximum(m, m_tile)
        p_scr[...] = jnp.exp2(s_cur - m_new.astype(BF16))
        s_scr[...] = s_nxt
        return m_new, m_nxt, jnp.exp2(m - m_new)

    n_full = (i * AQ) // SEL_TK
    pen0 = jnp.where(sel_scr[0:SEL_TK // SEL_BLOCK, :] > 0.5, 0.0, NEG)
    pen0 = jnp.concatenate([jnp.broadcast_to(pen0[b:b + 1, :], (SEL_BLOCK, AQ))
                            for b in range(SEL_TK // SEL_BLOCK)], axis=0).astype(BF16)
    s_first = s_scr[...] + jnp.concatenate([pen0] * GROUP, axis=1)
    s_scr[...] = s_first
    p_scr[...] = jnp.zeros((SEL_TK, nrow), BF16)
    acc_scr[...] = jnp.zeros((V_ROWS, nrow), F32)
    init = (jnp.full((1, nrow), NEG, F32), col_max(s_first), jnp.ones((1, nrow), F32))
    carry = lax.fori_loop(0, n_full // 2, lambda j, c: body(2 * j + 1, body(2 * j, c)), init)
    m_s, _, alpha_last = lax.cond(n_full % 2 == 1, lambda c: body(n_full - 1, c), lambda c: c, carry)
    add_pv(jnp.maximum(n_full - 1, 0), p_scr[...], alpha_last)
    off = pl.multiple_of(s0 - n_full * SEL_TK, AQ)
    s_scr[pl.ds(off, AQ), :] = s_scr[pl.ds(off, AQ), :] + tri_causal
    s_diag = s_scr[...]
    m_fin = jnp.maximum(m_s, col_max(s_diag))
    add_pv(n_full, jnp.exp2(s_diag - m_fin.astype(BF16)), jnp.exp2(m_s - m_fin))

    gt = g_ref[...].T
    grow = lambda c: jnp.concatenate([gt[c * GROUP + g:c * GROUP + g + 1, :] for g in range(GROUP)], axis=1)
    l_s = acc_scr[HEAD_DIM:HEAD_DIM + 1, :]
    l_w = ow_scr[HEAD_DIM:HEAD_DIM + 1, :]
    out_t = (oc_scr[...] * grow(0) + acc_scr[0:HEAD_DIM, :] * (grow(1) / l_s)
             + ow_scr[0:HEAD_DIM, :] * (grow(2) / l_w))
    out = jnp.concatenate([out_t[:, g * AQ:(g + 1) * AQ].T for g in range(GROUP)], axis=1)
    gn = gn_ref[...]
    o_ref[...] = (out * (gn * jax.nn.sigmoid(gn))).astype(BF16)


def _attention(q_r, kc, vct, ks_aug, vst, kw_aug, vwt, gates, proj):
    head_res = lambda shape: pl.BlockSpec((1,) + shape, lambda h, i: (h, 0, 0))
    return pl.pallas_call(
        _attn_kernel,
        grid=(N_KV_HEADS, SEQ // AQ),
        in_specs=[
            pl.BlockSpec((GROUP, HEAD_DIM, AQ), lambda h, i: (h, 0, i)),
            pl.BlockSpec((1, 1, N_CMP_PAD, HEAD_DIM), lambda h, i: (0, h, 0, 0)),
            pl.BlockSpec((1, 1, HEAD_DIM, N_CMP_PAD), lambda h, i: (1, h, 0, 0)),
            head_res((SEQ, 2 * HEAD_DIM)),
            head_res((V_ROWS, SEQ)),
            head_res((WINDOW + SEQ, 2 * HEAD_DIM)),
            head_res((V_ROWS, WINDOW + SEQ)),
            pl.BlockSpec((AQ, 128), lambda h, i: (i, h)),
            pl.BlockSpec((AQ, GROUP * HEAD_DIM), lambda h, i: (i, COL_GN // (GROUP * HEAD_DIM) + h)),
        ],
        out_specs=pl.BlockSpec((AQ, GROUP * HEAD_DIM), lambda h, i: (i, h)),
        out_shape=jax.ShapeDtypeStruct((SEQ, NSA_WIDTH), BF16),
        scratch_shapes=[pltpu.VMEM((AQ // 128, IMP_PAD + N_CMP_PAD + 8, 128), F32),
                        pltpu.VMEM((N_BLK, AQ), F32),
                        pltpu.VMEM((HEAD_DIM, AR), F32),
                        pltpu.VMEM((V_ROWS, AR), F32),
                        pltpu.VMEM((SEL_TK, AR), BF16),
                        pltpu.VMEM((SEL_TK, AR), BF16),
                        pltpu.VMEM((V_ROWS, AR), F32)],
        compiler_params=pltpu.CompilerParams(
            dimension_semantics=("arbitrary", "arbitrary"), vmem_limit_bytes=VMEM_LIMIT),
        name="attn",
    )(q_r, kc, vct, ks_aug, vst, kw_aug, vwt, gates, proj)


def _pool_kernel(u_ref, up_ref, gp_ref, mix_ref, sc_ref, wpo_ref, ya_ref, ext_scr):
    i = pl.program_id(0)
    tm = POOL_TM
    ext_scr[0:POOL_HALO, :] = jnp.where(i > 0, up_ref[...], 0.0)
    ext_scr[POOL_HALO:POOL_HALO + tm, :] = u_ref[...]
    t1 = i * tm + lax.broadcasted_iota(jnp.int32, (tm, 1), 0) + 1
    zs = []
    for g, w in enumerate(POOL_WINDOWS):
        cs = slice(g * POOL_GROUP, (g + 1) * POOL_GROUP)
        u = ext_scr[POOL_HALO:POOL_HALO + tm, cs]
        acc = u
        for k in range(1, w):
            acc = acc + ext_scr[POOL_HALO - k:POOL_HALO - k + tm, cs]
        cnt = jnp.minimum(t1, w).astype(F32)
        pooled = acc / cnt - u
        zs.append(_dot(pooled.astype(BF16), mix_ref[g]))
    z = jnp.concatenate(zs, axis=1) * sc_ref[...]
    gp = gp_ref[...]
    y = z * (gp * jax.nn.sigmoid(gp))
    ya_ref[...] = _dot(y.astype(BF16), wpo_ref[...])


def _pool(proj, mix, scale, w_pool_out):
    tm = POOL_TM
    return pl.pallas_call(
        _pool_kernel,
        grid=(SEQ // tm,),
        in_specs=[
            pl.BlockSpec((tm, POOL_WIDTH), lambda i: (i, COL_U // POOL_WIDTH)),
            pl.BlockSpec((POOL_HALO, POOL_WIDTH),
                         lambda i: (jnp.maximum(i * (tm // POOL_HALO) - 1, 0), COL_U // POOL_WIDTH)),
            pl.BlockSpec((tm, POOL_WIDTH), lambda i: (i, COL_GP // POOL_WIDTH)),
            pl.BlockSpec((len(POOL_WINDOWS), POOL_GROUP, POOL_GROUP), lambda i: (0, 0, 0)),
            pl.BlockSpec((1, POOL_WIDTH), lambda i: (0, 0)),
            pl.BlockSpec((POOL_WIDTH, D_MODEL), lambda i: (0, 0)),
        ],
        out_specs=pl.BlockSpec((tm, D_MODEL), lambda i: (i, 0)),
        out_shape=jax.ShapeDtypeStruct((SEQ, D_MODEL), F32),
        scratch_shapes=[pltpu.VMEM((POOL_HALO + tm, POOL_WIDTH), F32)],
        compiler_params=pltpu.CompilerParams(
            dimension_semantics=("arbitrary",), vmem_limit_bytes=VMEM_LIMIT),
        name="pool",
    )(proj, proj, proj, mix, scale, w_pool_out)


def _out_kernel(x_ref, ya_ref, o_ref, gm0_ref, gm1_ref, wn_ref, wo_ref, fw_ref, out_ref):
    y_b = _dot(o_ref[...], wn_ref[...])
    merged = gm0_ref[...] * ya_ref[...] + gm1_ref[...] * y_b
    r = x_ref[...] + _dot(merged.astype(BF16), wo_ref[...])
    ms = jnp.mean(r * r, axis=-1, keepdims=True)
    out_ref[...] = r * lax.rsqrt(ms + EPS) * fw_ref[...]


def _output(x2, y_a, o_gated, proj, w_nsa_out, w_out, final_norm_w):
    tm = OUT_TM
    row = lambda: pl.BlockSpec((tm, D_MODEL), lambda i: (i, 0))
    resident = lambda: pl.BlockSpec((D_MODEL, D_MODEL), lambda i: (0, 0), pipeline_mode=pl.Buffered(1))
    return pl.pallas_call(
        _out_kernel,
        grid=(SEQ // tm,),
        in_specs=[
            row(), row(), row(),
            pl.BlockSpec((tm, D_MODEL), lambda i: (i, 0)),
            pl.BlockSpec((tm, D_MODEL), lambda i: (i, 1)),
            resident(), resident(),
            pl.BlockSpec((1, D_MODEL), lambda i: (0, 0)),
        ],
        out_specs=row(),
        out_shape=jax.ShapeDtypeStruct((SEQ, D_MODEL), F32),
        compiler_params=pltpu.CompilerParams(
            dimension_semantics=("arbitrary",), vmem_limit_bytes=VMEM_LIMIT),
        name="out",
    )(x2, y_a, o_gated, proj, proj, w_nsa_out, w_out, final_norm_w)


def _rope_tables():
    half = HEAD_DIM // 2
    inv = ROPE_THETA ** (-np.arange(half, dtype=np.float64) / half)
    ang = np.arange(SEQ, dtype=np.float64)[:, None] * inv[None, :]
    cos = np.concatenate([np.cos(ang), np.cos(ang)], axis=-1).astype(np.float32)
    sin = np.concatenate([-np.sin(ang), np.sin(ang)], axis=-1).astype(np.float32)
    pad = lambda t: np.pad(t[CMP_BLOCK - 1::CMP_STRIDE], ((0, N_CMP_PAD - N_CMP), (0, 0)))
    cos_c = np.stack([pad(cos), np.ones((N_CMP_PAD, HEAD_DIM), np.float32)])
    sin_c = np.stack([pad(sin), np.zeros((N_CMP_PAD, HEAD_DIM), np.float32)])
    return cos, sin, cos_c, sin_c


def _gate_weight(w_in_l):
    wg = w_in_l[:, W_IN_MAIN:].reshape(D_MODEL, N_KV_HEADS, GROUP, 3)
    wg = wg.transpose(0, 1, 3, 2).reshape(D_MODEL, N_KV_HEADS, 3 * GROUP)
    wg = jnp.pad(wg, ((0, 0), (0, 0), (0, 128 - 3 * GROUP)))
    return wg.reshape(D_MODEL, N_KV_HEADS * 128).astype(BF16)


def kernel(x, norm_w, w_in, pool_mix, pool_scale, cmp_pe_k, cmp_w1_k, cmp_w2_k, cmp_pe_v, cmp_w1_v,
           cmp_w2_v, w_pool_out, w_nsa_out, w_merge, b_merge, w_out, final_norm_w):
    assert x.shape == (1, SEQ, D_MODEL) and norm_w.shape[0] == 1
    x2 = x[0]
    proj, gates = _proj(x2, norm_w, w_merge[0].astype(BF16), w_in[0].astype(BF16), _gate_weight(w_in[0]),
                        b_merge)

    cos_tab, sin_tab, cos_c, sin_c = _rope_tables()
    q_r, ks_aug, vst, kw_aug, vwt = _prep(proj, cos_tab, sin_tab)

    pe = jnp.stack([cmp_pe_k[0], cmp_pe_v[0]])
    w1 = jnp.stack([cmp_w1_k[0], cmp_w1_v[0]]).astype(BF16)
    w2 = jnp.stack([cmp_w2_k[0], cmp_w2_v[0]]).astype(BF16)
    cmp_n, cmp_t = _compress(proj, pe, w1, w2, cos_c, sin_c)

    o_gated = _attention(q_r, cmp_n, cmp_t, ks_aug, vst, kw_aug, vwt, gates, proj)

    y_a = _pool(proj, pool_mix[0].astype(BF16), pool_scale, w_pool_out[0].astype(BF16))
    out = _output(x2, y_a, o_gated, proj, w_nsa_out[0].astype(BF16), w_out[0].astype(BF16),
                  final_norm_w[None, :])
    return out[None]
```

```python
import functools
import math

import numpy as np
import jax
import jax.numpy as jnp
from jax import lax
from jax.experimental import pallas as pl
from jax.experimental.pallas import tpu as pltpu

F32 = jnp.float32
BF16 = jnp.bfloat16

D_MODEL = 2048
SEQ = 8192
POOL_WIDTH = 1024
POOL_WINDOWS = (2, 4, 8, 16)
POOL_GROUP = 256
POOL_HALO = 16
N_Q_HEADS = 16
N_KV_HEADS = 4
GROUP = 4
HEAD_DIM = 128
NSA_WIDTH = 2048
KV_WIDTH = 512
CMP_BLOCK = 32
CMP_STRIDE = 16
CMP_HIDDEN = 256
N_CMP = (SEQ - CMP_BLOCK) // CMP_STRIDE + 1
N_CMP_PAD = 512
SEL_BLOCK = 64
N_BLK = SEQ // SEL_BLOCK
N_SEL = 16
WINDOW = 512
ROPE_THETA = 10000.0
EPS = 1e-6
NEG = -1e30
FORCE = 1e6
REMOVED = -3e38
LOG2E = 1.4426950408889634
Q_SCALE = HEAD_DIM ** -0.5 * LOG2E

COL_GM = 0
COL_U = 4096
COL_GP = 5120
COL_Q = 6144
COL_KC = 8192
COL_VC = 8704
COL_KS = 9216
COL_VS = 9728
COL_KW = 10240
COL_VW = 10752
COL_GN = 11264
N_PROJ = 13312
W_IN_MAIN = 9216

VMEM_LIMIT = 56 * 1024 * 1024

PROJ_TM = 1024
PROJ_TN = 1024
N_GM_TILES = 4096 // PROJ_TN
PREP_TM = 512
SEL_TK = 512
AQ = 256
AR = GROUP * AQ
WIN_KEYS = WINDOW + AQ
CMP_MASK_ROWS = 256
V_ROWS = HEAD_DIM + 16
POOL_TM = 512
OUT_TM = 256


def _dot(a, b):
    return jnp.dot(a, b, preferred_element_type=F32)


def _proj_kernel(x_ref, nw_ref, wm_ref, wi_ref, wg_ref, b_ref, o_ref, g_ref, h_scr):
    j = pl.program_id(1)

    @pl.when(j == 0)
    def _():
        x = x_ref[...]
        ms = jnp.mean(x * x, axis=-1, keepdims=True)
        hb = (x * lax.rsqrt(ms + EPS) * nw_ref[...]).astype(BF16)
        h_scr[...] = hb
        g_ref[...] = jax.nn.sigmoid(_dot(hb, wg_ref[...]))

    @pl.when(j < N_GM_TILES)
    def _():
        o_ref[...] = jax.nn.sigmoid(_dot(h_scr[...], wm_ref[...]) + b_ref[...])

    @pl.when(j >= N_GM_TILES)
    def _():
        o_ref[...] = _dot(h_scr[...], wi_ref[...])


def _proj(x2, norm_w, w_merge, w_in, w_gate, b_merge):
    grid = (SEQ // PROJ_TM, N_PROJ // PROJ_TN)
    return pl.pallas_call(
        _proj_kernel,
        grid=grid,
        in_specs=[
            pl.BlockSpec((PROJ_TM, D_MODEL), lambda i, j: (i, 0)),
            pl.BlockSpec((1, D_MODEL), lambda i, j: (0, 0)),
            pl.BlockSpec((D_MODEL, PROJ_TN), lambda i, j: (0, jnp.minimum(j, N_GM_TILES - 1))),
            pl.BlockSpec((D_MODEL, PROJ_TN), lambda i, j: (0, jnp.maximum(j - N_GM_TILES, 0))),
            pl.BlockSpec((D_MODEL, N_KV_HEADS * 128), lambda i, j: (0, 0)),
            pl.BlockSpec((1, PROJ_TN), lambda i, j: (0, jnp.minimum(j, N_GM_TILES - 1))),
        ],
        out_specs=[
            pl.BlockSpec((PROJ_TM, PROJ_TN), lambda i, j: (i, j)),
            pl.BlockSpec((PROJ_TM, N_KV_HEADS * 128), lambda i, j: (i, 0)),
        ],
        out_shape=[
            jax.ShapeDtypeStruct((SEQ, N_PROJ), F32),
            jax.ShapeDtypeStruct((SEQ, N_KV_HEADS * 128), F32),
        ],
        scratch_shapes=[pltpu.VMEM((PROJ_TM, D_MODEL), BF16)],
        compiler_params=pltpu.CompilerParams(
            dimension_semantics=("arbitrary", "arbitrary"), vmem_limit_bytes=VMEM_LIMIT),
        name="proj",
    )(x2, norm_w, w_merge, w_in, w_gate, b_merge)


def _rope(x, cos, sin_signed):
    return x * cos + pltpu.roll(x, HEAD_DIM // 2, 1) * sin_signed


def _prep_kernel(q_ref, ks_ref, vs_ref, kw_ref, vw_ref, cos_ref, sin_ref,
                 qo_ref, kso_ref, vso_ref, kwo_ref, vwo_ref):
    i = pl.program_id(0)
    t0 = jnp.maximum(i - 1, 0) * PREP_TM
    cos = cos_ref[...]
    sin = sin_ref[...]
    for hq in range(N_Q_HEADS):
        sl = slice(hq * HEAD_DIM, (hq + 1) * HEAD_DIM)
        qo_ref[hq] = (_rope(q_ref[:, sl], cos, sin) * Q_SCALE).T.astype(BF16)
    blk = (t0 + lax.broadcasted_iota(jnp.int32, (PREP_TM, N_BLK), 0)) // SEL_BLOCK
    lane = lax.broadcasted_iota(jnp.int32, (PREP_TM, N_BLK), 1)
    onehot = jnp.where(blk == lane, 1.0, 0.0).astype(BF16)
    pad_flag = jnp.where(lane == 0, 1.0, 0.0)
    is_pad = i == 0
    ones_rows = jnp.where(lax.broadcasted_iota(jnp.int32, (V_ROWS - HEAD_DIM, PREP_TM), 0) == 0,
                          1.0, 0.0).astype(BF16)
    for h in range(N_KV_HEADS):
        sl = slice(h * HEAD_DIM, (h + 1) * HEAD_DIM)
        kso_ref[h, :, 0:HEAD_DIM] = _rope(ks_ref[:, sl], cos, sin).astype(BF16)
        kso_ref[h, :, HEAD_DIM:2 * HEAD_DIM] = onehot
        vso_ref[h, 0:HEAD_DIM, :] = vs_ref[:, sl].T.astype(BF16)
        vso_ref[h, HEAD_DIM:V_ROWS, :] = ones_rows
        kwo_ref[h, :, 0:HEAD_DIM] = jnp.where(is_pad, 0.0, _rope(kw_ref[:, sl], cos, sin)).astype(BF16)
        kwo_ref[h, :, HEAD_DIM:2 * HEAD_DIM] = jnp.where(is_pad, pad_flag, 0.0).astype(BF16)
        vwo_ref[h, 0:HEAD_DIM, :] = jnp.where(is_pad, 0.0, vw_ref[:, sl].T).astype(BF16)
        vwo_ref[h, HEAD_DIM:V_ROWS, :] = ones_rows


def _prep(proj, cos_tab, sin_tab):
    tm = PREP_TM
    assert tm == WINDOW
    tok = lambda i: jnp.maximum(i - 1, 0)
    kv_spec = lambda col: pl.BlockSpec((tm, KV_WIDTH), lambda i: (tok(i), col // KV_WIDTH))
    return pl.pallas_call(
        _prep_kernel,
        grid=(SEQ // tm + 1,),
        in_specs=[
            pl.BlockSpec((tm, NSA_WIDTH), lambda i: (tok(i), COL_Q // NSA_WIDTH)),
            kv_spec(COL_KS), kv_spec(COL_VS), kv_spec(COL_KW), kv_spec(COL_VW),
            pl.BlockSpec((tm, HEAD_DIM), lambda i: (tok(i), 0)),
            pl.BlockSpec((tm, HEAD_DIM), lambda i: (tok(i), 0)),
        ],
        out_specs=[
            pl.BlockSpec((N_Q_HEADS, HEAD_DIM, tm), lambda i: (0, 0, tok(i))),
            pl.BlockSpec((N_KV_HEADS, tm, 2 * HEAD_DIM), lambda i: (0, tok(i), 0)),
            pl.BlockSpec((N_KV_HEADS, V_ROWS, tm), lambda i: (0, 0, tok(i))),
            pl.BlockSpec((N_KV_HEADS, tm, 2 * HEAD_DIM), lambda i: (0, i, 0)),
            pl.BlockSpec((N_KV_HEADS, V_ROWS, tm), lambda i: (0, 0, i)),
        ],
        out_shape=[
            jax.ShapeDtypeStruct((N_Q_HEADS, HEAD_DIM, SEQ), BF16),
            jax.ShapeDtypeStruct((N_KV_HEADS, SEQ, 2 * HEAD_DIM), BF16),
            jax.ShapeDtypeStruct((N_KV_HEADS, V_ROWS, SEQ), BF16),
            jax.ShapeDtypeStruct((N_KV_HEADS, WINDOW + SEQ, 2 * HEAD_DIM), BF16),
            jax.ShapeDtypeStruct((N_KV_HEADS, V_ROWS, WINDOW + SEQ), BF16),
        ],
        compiler_params=pltpu.CompilerParams(
            dimension_semantics=("arbitrary",), vmem_limit_bytes=VMEM_LIMIT),
        name="prep",
    )(proj, proj, proj, proj, proj, cos_tab, sin_tab)


def _cmp_kernel(x_ref, pe_ref, w1_ref, w2_ref, cos_ref, sin_ref, o_ref, ot_ref, hi_scr):
    half = CMP_BLOCK // 2
    acc_lo = jnp.zeros((N_CMP_PAD, CMP_HIDDEN), F32)
    acc_hi = jnp.zeros((N_CMP_PAD, CMP_HIDDEN), F32)
    for l in range(half):
        xl = x_ref[pl.ds(l, N_CMP_PAD, stride=CMP_STRIDE), :]
        acc_lo += _dot((xl + pe_ref[0, l:l + 1, :]).astype(BF16), w1_ref[0, l])
        acc_hi += _dot((xl + pe_ref[0, half + l:half + l + 1, :]).astype(BF16), w1_ref[0, half + l])
    hi_scr[0:N_CMP_PAD, :] = acc_hi
    hi_scr[N_CMP_PAD:N_CMP_PAD + 8, :] = jnp.zeros((8, CMP_HIDDEN), F32)
    pre = acc_lo + hi_scr[pl.ds(1, N_CMP_PAD), :]
    hdn = pre * jax.nn.sigmoid(pre)
    out = _dot(hdn.astype(BF16), w2_ref[0])
    out = _rope(out, cos_ref[0], sin_ref[0])
    row = lax.broadcasted_iota(jnp.int32, (N_CMP_PAD, HEAD_DIM), 0)
    out = jnp.where(row < N_CMP, out, 0.0)
    o_ref[0, 0] = out.astype(BF16)
    ot_ref[0, 0] = out.T.astype(BF16)


def _compress(proj, pe, w1, w2, cos_c, sin_c):
    return pl.pallas_call(
        _cmp_kernel,
        grid=(2, N_KV_HEADS),
        in_specs=[
            pl.BlockSpec((SEQ, HEAD_DIM), lambda k, h: (0, COL_KC // HEAD_DIM + N_KV_HEADS * k + h)),
            pl.BlockSpec((1, CMP_BLOCK, HEAD_DIM), lambda k, h: (k, 0, 0)),
            pl.BlockSpec((1, CMP_BLOCK, HEAD_DIM, CMP_HIDDEN), lambda k, h: (k, 0, 0, 0)),
            pl.BlockSpec((1, CMP_HIDDEN, HEAD_DIM), lambda k, h: (k, 0, 0)),
            pl.BlockSpec((1, N_CMP_PAD, HEAD_DIM), lambda k, h: (k, 0, 0)),
            pl.BlockSpec((1, N_CMP_PAD, HEAD_DIM), lambda k, h: (k, 0, 0)),
        ],
        out_specs=[
            pl.BlockSpec((1, 1, N_CMP_PAD, HEAD_DIM), lambda k, h: (k, h, 0, 0)),
            pl.BlockSpec((1, 1, HEAD_DIM, N_CMP_PAD), lambda k, h: (k, h, 0, 0)),
        ],
        out_shape=[
            jax.ShapeDtypeStruct((2, N_KV_HEADS, N_CMP_PAD, HEAD_DIM), BF16),
            jax.ShapeDtypeStruct((2, N_KV_HEADS, HEAD_DIM, N_CMP_PAD), BF16),
        ],
        scratch_shapes=[pltpu.VMEM((N_CMP_PAD + 8, CMP_HIDDEN), F32)],
        compiler_params=pltpu.CompilerParams(
            dimension_semantics=("arbitrary", "arbitrary"), vmem_limit_bytes=VMEM_LIMIT),
        name="compress",
    )(proj, pe, w1, w2, cos_c, sin_c)


IMP_PAD = 8
NH = 2
N_STAT = 3


def _attn_kernel(q_ref, kc_ref, vct_ref, ks_ref, vst_ref, kw_ref, vwt_ref, g_ref, gn_ref,
                 o_ref, imp_scr, sel_scr, oc_scr, ow_scr, s_scr, p_scr, acc_scr, stat_scr):
    i = pl.program_id(1)
    s0 = i * AQ
    nrow = AR
    heads = range(NH)

    q_t = [jnp.concatenate([q_ref[hh * GROUP + g] for g in range(GROUP)], axis=1) for hh in heads]

    def mask_groups(s, mask):
        return jnp.concatenate(
            [jnp.where(mask, s[:, g * AQ:(g + 1) * AQ], NEG) for g in range(GROUP)], axis=1)

    def compressed(n_chunks):
        rows = 128 * n_chunks
        mrows = min(rows, CMP_MASK_ROWS)
        c0 = rows - mrows
        r_i = lax.broadcasted_iota(jnp.int32, (mrows, AQ), 0)
        t_i = lax.broadcasted_iota(jnp.int32, (mrows, AQ), 1)
        visible = CMP_STRIDE * r_i - t_i <= s0 - (CMP_BLOCK - 1) - CMP_STRIDE * c0
        rest = IMP_PAD + N_CMP_PAD + 8 - (IMP_PAD + rows)
        for hh in heads:
            sc = _dot(kc_ref[0, hh, 0:rows, :], q_t[hh])
            tail = mask_groups(sc[c0:rows], visible)
            sc = tail if c0 == 0 else jnp.concatenate([sc[0:c0], tail], axis=0)
            m_c = jnp.max(sc, axis=0, keepdims=True)
            m_c = jnp.where(m_c > 0.5 * NEG, m_c, 0.0)
            e_c = jnp.exp2(sc - m_c)
            l_c = jnp.sum(e_c, axis=0, keepdims=True)
            p_c = e_c * (1.0 / jnp.where(l_c > 0.0, l_c, 1.0))
            oc_scr[hh] = _dot(vct_ref[0, hh, :, 0:rows], p_c.astype(BF16))
            imp_c = (p_c[:, 0:AQ] + p_c[:, AQ:2 * AQ]) + (p_c[:, 2 * AQ:3 * AQ] + p_c[:, 3 * AQ:4 * AQ])
            for a in range(AQ // 128):
                imp_scr[hh, a, IMP_PAD:IMP_PAD + rows, :] = imp_c[:, a * 128:(a + 1) * 128]
                imp_scr[hh, a, IMP_PAD + rows:IMP_PAD + rows + rest, :] = jnp.zeros((rest, 128), F32)

    for hh in heads:
        for a in range(AQ // 128):
            imp_scr[hh, a, 0:IMP_PAD, :] = jnp.zeros((IMP_PAD, 128), F32)
    n_cmp_chunks = i // (128 * CMP_STRIDE // AQ) + 1
    for n in range(1, N_CMP_PAD // 128 + 1):
        pl.when(n_cmp_chunks == n)(functools.partial(compressed, n))

    ratio = SEL_BLOCK // CMP_STRIDE

    def importance(hh):
        tap = lambda k: jnp.concatenate(
            [imp_scr[hh, a, pl.ds(IMP_PAD + k, N_BLK, stride=ratio), :] for a in range(AQ // 128)], axis=1)
        return tap(-1) + 2.0 * (tap(0) + tap(1) + tap(2)) + tap(3)

    jidx = lax.broadcasted_iota(jnp.int32, (N_BLK, AQ), 0).astype(F32)
    tok_l = lax.broadcasted_iota(jnp.int32, (1, AQ), 1)
    jt = (i * (AQ // SEL_BLOCK) + tok_l // SEL_BLOCK).astype(F32)
    miscount = []
    for hh in heads:
        v = jnp.where(jidx == 0.0, REMOVED, importance(hh))
        v = jnp.where(jidx == jt, REMOVED, v)
        v = jnp.where(jidx == jt - 1.0, REMOVED, v)
        v = jnp.where(jidx > jt, NEG, v)
        for _ in range(N_SEL - 3):
            v = jnp.where(v == jnp.max(v, axis=0, keepdims=True), REMOVED, v)
        sel_fast = jnp.where(jidx > jt, 0.0, jnp.where(v == REMOVED, 1.0, 0.0))
        n_picked = jnp.sum(sel_fast, axis=0, keepdims=True)
        miscount.append(jnp.max(jnp.abs(n_picked - jnp.minimum(jt + 1.0, float(N_SEL)))))
        sel_scr[hh] = sel_fast

    flag_pen = jnp.where(lax.broadcasted_iota(jnp.int32, (HEAD_DIM, nrow), 0) == 0, NEG, 0.0).astype(BF16)
    wk0 = pl.multiple_of(s0, AQ)
    r_i = lax.broadcasted_iota(jnp.int32, (AQ, AQ), 0)
    t_i = lax.broadcasted_iota(jnp.int32, (AQ, AQ), 1)
    tri_old = jnp.concatenate([jnp.where(r_i > t_i, 0.0, NEG).astype(BF16)] * GROUP, axis=1)
    tri_causal = jnp.concatenate([jnp.where(r_i <= t_i, 0.0, NEG).astype(BF16)] * GROUP, axis=1)
    for hh in heads:
        q_win = jnp.concatenate([q_t[hh], flag_pen], axis=0)
        sw = _dot(kw_ref[hh, pl.ds(wk0, WIN_KEYS), :], q_win).astype(BF16)
        sw = jnp.concatenate([sw[0:AQ] + tri_old, sw[AQ:WINDOW], sw[WINDOW:WIN_KEYS] + tri_causal], axis=0)
        m_w = jnp.max(sw, axis=0, keepdims=True)
        p_w = jnp.exp2(sw - m_w)
        ow_scr[hh] = _dot(vwt_ref[hh, :, pl.ds(wk0, WIN_KEYS)], p_w)
        s_scr[hh] = _dot(ks_ref[hh, 0:SEL_TK, 0:HEAD_DIM], q_t[hh]).astype(BF16)

    @pl.when(functools.reduce(jnp.maximum, miscount) > 0.5)
    def _():
        for hh in heads:
            w = jnp.where(jidx == 0.0, FORCE, importance(hh))
            w = jnp.where(jidx == jt, FORCE, w)
            w = jnp.where(jidx == jt - 1.0, FORCE, w)
            w = jnp.where(jidx > jt, NEG, w)
            sel = jnp.zeros((N_BLK, AQ), F32)
            for _ in range(N_SEL):
                mx = jnp.max(w, axis=0, keepdims=True)
                cand = jnp.where(w == mx, jidx, float(N_BLK))
                pick = jidx == jnp.min(cand, axis=0, keepdims=True)
                sel = jnp.where(pick, 1.0, sel)
                w = jnp.where(pick, REMOVED, w)
            sel_scr[hh] = jnp.where(jidx > jt, 0.0, sel)

    q_aug = []
    for hh in heads:
        pen = jnp.where(sel_scr[hh] > 0.5, 0.0, NEG).astype(BF16)
        q_aug.append(jnp.concatenate([q_t[hh], jnp.concatenate([pen] * GROUP, axis=1)], axis=0))

    def key_start(kt):
        return pl.multiple_of(kt * SEL_TK, SEL_TK)

    def col_max(s):
        return jnp.max(s, axis=0, keepdims=True).astype(F32)

    def stat(hh, k):
        return stat_scr[hh * N_STAT + k:hh * N_STAT + k + 1, :]

    def set_stat(hh, k, value):
        stat_scr[hh * N_STAT + k:hh * N_STAT + k + 1, :] = value

    def add_pv(hh, kt, p, alpha):
        pv = _dot(vst_ref[hh, :, pl.ds(key_start(kt), SEL_TK)], p)
        acc_scr[hh] = alpha * acc_scr[hh] + pv

    def step(kt):
        for hh in heads:
            m, m_tile, alpha_prev = stat(hh, 0), stat(hh, 1), stat(hh, 2)
            p_prev = p_scr[hh]
            s_cur = s_scr[hh]
            s_nxt = _dot(ks_ref[hh, pl.ds(key_start(kt + 1), SEL_TK), :], q_aug[hh]).astype(BF16)
            add_pv(hh, jnp.maximum(kt - 1, 0), p_prev, alpha_prev)
            m_new = jnp.maximum(m, m_tile)
            p_scr[hh] = jnp.exp2(s_cur - m_new.astype(BF16))
            s_scr[hh] = s_nxt
            set_stat(hh, 0, m_new)
            set_stat(hh, 1, col_max(s_nxt))
            set_stat(hh, 2, jnp.exp2(m - m_new))

    n_full = (i * AQ) // SEL_TK
    for hh in heads:
        pen0 = jnp.where(sel_scr[hh, 0:SEL_TK // SEL_BLOCK, :] > 0.5, 0.0, NEG)
        pen0 = jnp.concatenate([jnp.broadcast_to(pen0[b:b + 1, :], (SEL_BLOCK, AQ))
                                for b in range(SEL_TK // SEL_BLOCK)], axis=0).astype(BF16)
        s_first = s_scr[hh] + jnp.concatenate([pen0] * GROUP, axis=1)
        s_scr[hh] = s_first
        p_scr[hh] = jnp.zeros((SEL_TK, nrow), BF16)
        acc_scr[hh] = jnp.zeros((V_ROWS, nrow), F32)
        set_stat(hh, 0, jnp.full((1, nrow), NEG, F32))
        set_stat(hh, 1, col_max(s_first))
        set_stat(hh, 2, jnp.ones((1, nrow), F32))

    @pl.loop(0, n_full // 2)
    def _(j):
        step(2 * j)
        step(2 * j + 1)

    @pl.when(n_full % 2 == 1)
    def _():
        step(n_full - 1)

    off = pl.multiple_of(s0 - n_full * SEL_TK, AQ)
    for hh in heads:
        add_pv(hh, jnp.maximum(n_full - 1, 0), p_scr[hh], stat(hh, 2))
        s_scr[hh, pl.ds(off, AQ), :] = s_scr[hh, pl.ds(off, AQ), :] + tri_causal
        s_diag = s_scr[hh]
        m_s = stat(hh, 0)
        m_fin = jnp.maximum(m_s, col_max(s_diag))
        add_pv(hh, n_full, jnp.exp2(s_diag - m_fin.astype(BF16)), jnp.exp2(m_s - m_fin))

    for hh in heads:
        gt = g_ref[:, hh * 128:(hh + 1) * 128].T
        grow = lambda c: jnp.concatenate([gt[c * GROUP + g:c * GROUP + g + 1, :] for g in range(GROUP)], axis=1)
        l_s = acc_scr[hh, HEAD_DIM:HEAD_DIM + 1, :]
        l_w = ow_scr[hh, HEAD_DIM:HEAD_DIM + 1, :]
        out_t = (oc_scr[hh] * grow(0) + acc_scr[hh, 0:HEAD_DIM, :] * (grow(1) / l_s)
                 + ow_scr[hh, 0:HEAD_DIM, :] * (grow(2) / l_w))
        out = jnp.concatenate([out_t[:, g * AQ:(g + 1) * AQ].T for g in range(GROUP)], axis=1)
        cols = slice(hh * GROUP * HEAD_DIM, (hh + 1) * GROUP * HEAD_DIM)
        gn = gn_ref[:, cols]
        o_ref[:, cols] = (out * (gn * jax.nn.sigmoid(gn))).astype(BF16)


def _attention(q_r, kc, vct, ks_aug, vst, kw_aug, vwt, gates, proj):
    head_res = lambda shape: pl.BlockSpec((NH,) + shape, lambda hp, i: (hp, 0, 0), pipeline_mode=pl.Buffered(1))
    width = NH * GROUP * HEAD_DIM
    return pl.pallas_call(
        _attn_kernel,
        grid=(N_KV_HEADS // NH, SEQ // AQ),
        in_specs=[
            pl.BlockSpec((NH * GROUP, HEAD_DIM, AQ), lambda hp, i: (hp, 0, i)),
            pl.BlockSpec((1, NH, N_CMP_PAD, HEAD_DIM), lambda hp, i: (0, hp, 0, 0)),
            pl.BlockSpec((1, NH, HEAD_DIM, N_CMP_PAD), lambda hp, i: (1, hp, 0, 0)),
            head_res((SEQ, 2 * HEAD_DIM)),
            head_res((V_ROWS, SEQ)),
            head_res((WINDOW + SEQ, 2 * HEAD_DIM)),
            head_res((V_ROWS, WINDOW + SEQ)),
            pl.BlockSpec((AQ, NH * 128), lambda hp, i: (i, hp)),
            pl.BlockSpec((AQ, width), lambda hp, i: (i, COL_GN // width + hp)),
        ],
        out_specs=pl.BlockSpec((AQ, width), lambda hp, i: (i, hp)),
        out_shape=jax.ShapeDtypeStruct((SEQ, NSA_WIDTH), BF16),
        scratch_shapes=[pltpu.VMEM((NH, AQ // 128, IMP_PAD + N_CMP_PAD + 8, 128), F32),
                        pltpu.VMEM((NH, N_BLK, AQ), F32),
                        pltpu.VMEM((NH, HEAD_DIM, AR), F32),
                        pltpu.VMEM((NH, V_ROWS, AR), F32),
                        pltpu.VMEM((NH, SEL_TK, AR), BF16),
                        pltpu.VMEM((NH, SEL_TK, AR), BF16),
                        pltpu.VMEM((NH, V_ROWS, AR), F32),
                        pltpu.VMEM((NH * N_STAT, AR), F32)],
        compiler_params=pltpu.CompilerParams(
            dimension_semantics=("arbitrary", "arbitrary"), vmem_limit_bytes=VMEM_LIMIT),
        name="attn",
    )(q_r, kc, vct, ks_aug, vst, kw_aug, vwt, gates, proj)


def _pool_kernel(u_ref, up_ref, gp_ref, mix_ref, sc_ref, wpo_ref, ya_ref, ext_scr):
    i = pl.program_id(0)
    tm = POOL_TM
    ext_scr[0:POOL_HALO, :] = jnp.where(i > 0, up_ref[...], 0.0)
    ext_scr[POOL_HALO:POOL_HALO + tm, :] = u_ref[...]
    t1 = i * tm + lax.broadcasted_iota(jnp.int32, (tm, 1), 0) + 1
    zs = []
    for g, w in enumerate(POOL_WINDOWS):
        cs = slice(g * POOL_GROUP, (g + 1) * POOL_GROUP)
        u = ext_scr[POOL_HALO:POOL_HALO + tm, cs]
        acc = u
        for k in range(1, w):
            acc = acc + ext_scr[POOL_HALO - k:POOL_HALO - k + tm, cs]
        cnt = jnp.minimum(t1, w).astype(F32)
        pooled = acc / cnt - u
        zs.append(_dot(pooled.astype(BF16), mix_ref[g]))
    z = jnp.concatenate(zs, axis=1) * sc_ref[...]
    gp = gp_ref[...]
    y = z * (gp * jax.nn.sigmoid(gp))
    ya_ref[...] = _dot(y.astype(BF16), wpo_ref[...])


def _pool(proj, mix, scale, w_pool_out):
    tm = POOL_TM
    return pl.pallas_call(
        _pool_kernel,
        grid=(SEQ // tm,),
        in_specs=[
            pl.BlockSpec((tm, POOL_WIDTH), lambda i: (i, COL_U // POOL_WIDTH)),
            pl.BlockSpec((POOL_HALO, POOL_WIDTH),
                         lambda i: (jnp.maximum(i * (tm // POOL_HALO) - 1, 0), COL_U // POOL_WIDTH)),
            pl.BlockSpec((tm, POOL_WIDTH), lambda i: (i, COL_GP // POOL_WIDTH)),
            pl.BlockSpec((len(POOL_WINDOWS), POOL_GROUP, POOL_GROUP), lambda i: (0, 0, 0)),
            pl.BlockSpec((1, POOL_WIDTH), lambda i: (0, 0)),
            pl.BlockSpec((POOL_WIDTH, D_MODEL), lambda i: (0, 0)),
        ],
        out_specs=pl.BlockSpec((tm, D_MODEL), lambda i: (i, 0)),
        out_shape=jax.ShapeDtypeStruct((SEQ, D_MODEL), F32),
        scratch_shapes=[pltpu.VMEM((POOL_HALO + tm, POOL_WIDTH), F32)],
        compiler_params=pltpu.CompilerParams(
            dimension_semantics=("arbitrary",), vmem_limit_bytes=VMEM_LIMIT),
        name="pool",
    )(proj, proj, proj, mix, scale, w_pool_out)


def _out_kernel(x_ref, ya_ref, o_ref, gm0_ref, gm1_ref, wn_ref, wo_ref, fw_ref, out_ref):
    y_b = _dot(o_ref[...], wn_ref[...])
    merged = gm0_ref[...] * ya_ref[...] + gm1_ref[...] * y_b
    r = x_ref[...] + _dot(merged.astype(BF16), wo_ref[...])
    ms = jnp.mean(r * r, axis=-1, keepdims=True)
    out_ref[...] = r * lax.rsqrt(ms + EPS) * fw_ref[...]


def _output(x2, y_a, o_gated, proj, w_nsa_out, w_out, final_norm_w):
    tm = OUT_TM
    row = lambda: pl.BlockSpec((tm, D_MODEL), lambda i: (i, 0))
    resident = lambda: pl.BlockSpec((D_MODEL, D_MODEL), lambda i: (0, 0), pipeline_mode=pl.Buffered(1))
    return pl.pallas_call(
        _out_kernel,
        grid=(SEQ // tm,),
        in_specs=[
            row(), row(), row(),
            pl.BlockSpec((tm, D_MODEL), lambda i: (i, 0)),
            pl.BlockSpec((tm, D_MODEL), lambda i: (i, 1)),
            resident(), resident(),
            pl.BlockSpec((1, D_MODEL), lambda i: (0, 0)),
        ],
        out_specs=row(),
        out_shape=jax.ShapeDtypeStruct((SEQ, D_MODEL), F32),
        compiler_params=pltpu.CompilerParams(
            dimension_semantics=("arbitrary",), vmem_limit_bytes=VMEM_LIMIT),
        name="out",
    )(x2, y_a, o_gated, proj, proj, w_nsa_out, w_out, final_norm_w)


def _rope_tables():
    half = HEAD_DIM // 2
    inv = ROPE_THETA ** (-np.arange(half, dtype=np.float64) / half)
    ang = np.arange(SEQ, dtype=np.float64)[:, None] * inv[None, :]
    cos = np.concatenate([np.cos(ang), np.cos(ang)], axis=-1).astype(np.float32)
    sin = np.concatenate([-np.sin(ang), np.sin(ang)], axis=-1).astype(np.float32)
    pad = lambda t: np.pad(t[CMP_BLOCK - 1::CMP_STRIDE], ((0, N_CMP_PAD - N_CMP), (0, 0)))
    cos_c = np.stack([pad(cos), np.ones((N_CMP_PAD, HEAD_DIM), np.float32)])
    sin_c = np.stack([pad(sin), np.zeros((N_CMP_PAD, HEAD_DIM), np.float32)])
    return cos, sin, cos_c, sin_c


def _gate_weight(w_in_l):
    wg = w_in_l[:, W_IN_MAIN:].reshape(D_MODEL, N_KV_HEADS, GROUP, 3)
    wg = wg.transpose(0, 1, 3, 2).reshape(D_MODEL, N_KV_HEADS, 3 * GROUP)
    wg = jnp.pad(wg, ((0, 0), (0, 0), (0, 128 - 3 * GROUP)))
    return wg.reshape(D_MODEL, N_KV_HEADS * 128).astype(BF16)


def kernel(x, norm_w, w_in, pool_mix, pool_scale, cmp_pe_k, cmp_w1_k, cmp_w2_k, cmp_pe_v, cmp_w1_v,
           cmp_w2_v, w_pool_out, w_nsa_out, w_merge, b_merge, w_out, final_norm_w):
    assert x.shape == (1, SEQ, D_MODEL) and norm_w.shape[0] == 1
    x2 = x.reshape(SEQ, D_MODEL)
    proj, gates = _proj(x2, norm_w, w_merge[0].astype(BF16), w_in[0].astype(BF16), _gate_weight(w_in[0]),
                        b_merge)

    cos_tab, sin_tab, cos_c, sin_c = _rope_tables()
    q_r, ks_aug, vst, kw_aug, vwt = _prep(proj, cos_tab, sin_tab)

    pe = jnp.stack([cmp_pe_k[0], cmp_pe_v[0]])
    w1 = jnp.stack([cmp_w1_k[0], cmp_w1_v[0]]).astype(BF16)
    w2 = jnp.stack([cmp_w2_k[0], cmp_w2_v[0]]).astype(BF16)
    cmp_n, cmp_t = _compress(proj, pe, w1, w2, cos_c, sin_c)

    o_gated = _attention(q_r, cmp_n, cmp_t, ks_aug, vst, kw_aug, vwt, gates, proj)

    y_a = _pool(proj, pool_mix[0].astype(BF16), pool_scale, w_pool_out[0].astype(BF16))
    out = _output(x2, y_a, o_gated, proj, w_nsa_out[0].astype(BF16), w_out[0].astype(BF16),
                  final_norm_w[None, :])
    return out.reshape(1, SEQ, D_MODEL)
```

```python
import functools
import math

import numpy as np
import jax
import jax.numpy as jnp
from jax import lax
from jax.experimental import pallas as pl
from jax.experimental.pallas import tpu as pltpu

F32 = jnp.float32
BF16 = jnp.bfloat16

D_MODEL = 2048
SEQ = 8192
POOL_WIDTH = 1024
POOL_WINDOWS = (2, 4, 8, 16)
POOL_GROUP = 256
POOL_HALO = 16
N_Q_HEADS = 16
N_KV_HEADS = 4
GROUP = 4
HEAD_DIM = 128
NSA_WIDTH = 2048
KV_WIDTH = 512
CMP_BLOCK = 32
CMP_STRIDE = 16
CMP_HIDDEN = 256
N_CMP = (SEQ - CMP_BLOCK) // CMP_STRIDE + 1
N_CMP_PAD = 512
SEL_BLOCK = 64
N_BLK = SEQ // SEL_BLOCK
N_SEL = 16
WINDOW = 512
ROPE_THETA = 10000.0
EPS = 1e-6
NEG = -1e30
FORCE = 1e6
REMOVED = -3e38
LOG2E = 1.4426950408889634
Q_SCALE = HEAD_DIM ** -0.5 * LOG2E

COL_GM = 0
COL_U = 4096
COL_GP = 5120
COL_Q = 6144
COL_KC = 8192
COL_VC = 8704
COL_KS = 9216
COL_VS = 9728
COL_KW = 10240
COL_VW = 10752
COL_GN = 11264
N_PROJ = 13312
W_IN_MAIN = 9216

VMEM_LIMIT = 56 * 1024 * 1024
PROJ_VMEM_LIMIT = 58 * 1024 * 1024

PROJ_TM = 1024
PROJ_TN = 1024
N_GM_TILES = 4096 // PROJ_TN
PREP_TM = 512
SEL_TK = 512
AQ = 256
AR = GROUP * AQ
WIN_KEYS = WINDOW + AQ
CMP_MASK_ROWS = 256
V_ROWS = HEAD_DIM + 16
POOL_TM = 512
OUT_TM = 256


def _dot(a, b):
    return jnp.dot(a, b, preferred_element_type=F32)


def _proj_kernel(x_ref, nw_ref, wm_ref, wi_ref, wg_ref, b_ref, o_ref, g_ref, h_scr):
    j = pl.program_id(1)

    @pl.when(j == 0)
    def _():
        x = x_ref[...]
        ms = jnp.mean(x * x, axis=-1, keepdims=True)
        hb = (x * lax.rsqrt(ms + EPS) * nw_ref[...]).astype(BF16)
        h_scr[...] = hb
        g_ref[...] = jax.nn.sigmoid(_dot(hb, wg_ref[...]))

    @pl.when(j < N_GM_TILES)
    def _():
        o_ref[...] = jax.nn.sigmoid(_dot(h_scr[...], wm_ref[...]) + b_ref[...])

    @pl.when(j >= N_GM_TILES)
    def _():
        o_ref[...] = _dot(h_scr[...], wi_ref[...].astype(BF16))


def _proj(x2, norm_w, w_merge, w_in, w_gate, b_merge):
    grid = (SEQ // PROJ_TM, N_PROJ // PROJ_TN)
    return pl.pallas_call(
        _proj_kernel,
        grid=grid,
        in_specs=[
            pl.BlockSpec((PROJ_TM, D_MODEL), lambda i, j: (i, 0), pipeline_mode=pl.Buffered(1)),
            pl.BlockSpec((1, D_MODEL), lambda i, j: (0, 0)),
            pl.BlockSpec((D_MODEL, PROJ_TN), lambda i, j: (0, jnp.minimum(j, N_GM_TILES - 1))),
            pl.BlockSpec((D_MODEL, PROJ_TN), lambda i, j: (0, jnp.maximum(j - N_GM_TILES, 0))),
            pl.BlockSpec((D_MODEL, N_KV_HEADS * 128), lambda i, j: (0, 0), pipeline_mode=pl.Buffered(1)),
            pl.BlockSpec((1, PROJ_TN), lambda i, j: (0, jnp.minimum(j, N_GM_TILES - 1))),
        ],
        out_specs=[
            pl.BlockSpec((PROJ_TM, PROJ_TN), lambda i, j: (i, j)),
            pl.BlockSpec((PROJ_TM, N_KV_HEADS * 128), lambda i, j: (i, 0)),
        ],
        out_shape=[
            jax.ShapeDtypeStruct((SEQ, N_PROJ), F32),
            jax.ShapeDtypeStruct((SEQ, N_KV_HEADS * 128), F32),
        ],
        scratch_shapes=[pltpu.VMEM((PROJ_TM, D_MODEL), BF16)],
        compiler_params=pltpu.CompilerParams(
            dimension_semantics=("arbitrary", "arbitrary"), vmem_limit_bytes=PROJ_VMEM_LIMIT),
        name="proj",
    )(x2, norm_w, w_merge, w_in, w_gate, b_merge)


def _rope(x, cos, sin_signed):
    return x * cos + pltpu.roll(x, HEAD_DIM // 2, 1) * sin_signed


def _prep_kernel(q_ref, ks_ref, vs_ref, kw_ref, vw_ref, cos_ref, sin_ref,
                 qo_ref, kso_ref, vso_ref, kwo_ref, vwo_ref):
    i = pl.program_id(0)
    t0 = jnp.maximum(i - 1, 0) * PREP_TM
    cos = cos_ref[...]
    sin = sin_ref[...]
    for hq in range(N_Q_HEADS):
        sl = slice(hq * HEAD_DIM, (hq + 1) * HEAD_DIM)
        qo_ref[hq] = (_rope(q_ref[:, sl], cos, sin) * Q_SCALE).T.astype(BF16)
    blk = (t0 + lax.broadcasted_iota(jnp.int32, (PREP_TM, N_BLK), 0)) // SEL_BLOCK
    lane = lax.broadcasted_iota(jnp.int32, (PREP_TM, N_BLK), 1)
    onehot = jnp.where(blk == lane, 1.0, 0.0).astype(BF16)
    pad_flag = jnp.where(lane == 0, 1.0, 0.0)
    is_pad = i == 0
    ones_rows = jnp.where(lax.broadcasted_iota(jnp.int32, (V_ROWS - HEAD_DIM, PREP_TM), 0) == 0,
                          1.0, 0.0).astype(BF16)
    for h in range(N_KV_HEADS):
        sl = slice(h * HEAD_DIM, (h + 1) * HEAD_DIM)
        kso_ref[h, :, 0:HEAD_DIM] = _rope(ks_ref[:, sl], cos, sin).astype(BF16)
        kso_ref[h, :, HEAD_DIM:2 * HEAD_DIM] = onehot
        vso_ref[h, 0:HEAD_DIM, :] = vs_ref[:, sl].T.astype(BF16)
        vso_ref[h, HEAD_DIM:V_ROWS, :] = ones_rows
        kwo_ref[h, :, 0:HEAD_DIM] = jnp.where(is_pad, 0.0, _rope(kw_ref[:, sl], cos, sin)).astype(BF16)
        kwo_ref[h, :, HEAD_DIM:2 * HEAD_DIM] = jnp.where(is_pad, pad_flag, 0.0).astype(BF16)
        vwo_ref[h, 0:HEAD_DIM, :] = jnp.where(is_pad, 0.0, vw_ref[:, sl].T).astype(BF16)
        vwo_ref[h, HEAD_DIM:V_ROWS, :] = ones_rows


def _prep(proj, cos_tab, sin_tab):
    tm = PREP_TM
    assert tm == WINDOW
    tok = lambda i: jnp.maximum(i - 1, 0)
    kv_spec = lambda col: pl.BlockSpec((tm, KV_WIDTH), lambda i: (tok(i), col // KV_WIDTH))
    return pl.pallas_call(
        _prep_kernel,
        grid=(SEQ // tm + 1,),
        in_specs=[
            pl.BlockSpec((tm, NSA_WIDTH), lambda i: (tok(i), COL_Q // NSA_WIDTH)),
            kv_spec(COL_KS), kv_spec(COL_VS), kv_spec(COL_KW), kv_spec(COL_VW),
            pl.BlockSpec((tm, HEAD_DIM), lambda i: (tok(i), 0)),
            pl.BlockSpec((tm, HEAD_DIM), lambda i: (tok(i), 0)),
        ],
        out_specs=[
            pl.BlockSpec((N_Q_HEADS, HEAD_DIM, tm), lambda i: (0, 0, tok(i))),
            pl.BlockSpec((N_KV_HEADS, tm, 2 * HEAD_DIM), lambda i: (0, tok(i), 0)),
            pl.BlockSpec((N_KV_HEADS, V_ROWS, tm), lambda i: (0, 0, tok(i))),
            pl.BlockSpec((N_KV_HEADS, tm, 2 * HEAD_DIM), lambda i: (0, i, 0)),
            pl.BlockSpec((N_KV_HEADS, V_ROWS, tm), lambda i: (0, 0, i)),
        ],
        out_shape=[
            jax.ShapeDtypeStruct((N_Q_HEADS, HEAD_DIM, SEQ), BF16),
            jax.ShapeDtypeStruct((N_KV_HEADS, SEQ, 2 * HEAD_DIM), BF16),
            jax.ShapeDtypeStruct((N_KV_HEADS, V_ROWS, SEQ), BF16),
            jax.ShapeDtypeStruct((N_KV_HEADS, WINDOW + SEQ, 2 * HEAD_DIM), BF16),
            jax.ShapeDtypeStruct((N_KV_HEADS, V_ROWS, WINDOW + SEQ), BF16),
        ],
        compiler_params=pltpu.CompilerParams(
            dimension_semantics=("arbitrary",), vmem_limit_bytes=VMEM_LIMIT),
        name="prep",
    )(proj, proj, proj, proj, proj, cos_tab, sin_tab)


def _cmp_kernel(x_ref, pe_ref, w1_ref, w2_ref, cos_ref, sin_ref, o_ref, ot_ref, hi_scr):
    half = CMP_BLOCK // 2
    acc_lo = jnp.zeros((N_CMP_PAD, CMP_HIDDEN), F32)
    acc_hi = jnp.zeros((N_CMP_PAD, CMP_HIDDEN), F32)
    for l in range(half):
        xl = x_ref[pl.ds(l, N_CMP_PAD, stride=CMP_STRIDE), :]
        acc_lo += _dot((xl + pe_ref[0, l:l + 1, :]).astype(BF16), w1_ref[0, l])
        acc_hi += _dot((xl + pe_ref[0, half + l:half + l + 1, :]).astype(BF16), w1_ref[0, half + l])
    hi_scr[0:N_CMP_PAD, :] = acc_hi
    hi_scr[N_CMP_PAD:N_CMP_PAD + 8, :] = jnp.zeros((8, CMP_HIDDEN), F32)
    pre = acc_lo + hi_scr[pl.ds(1, N_CMP_PAD), :]
    hdn = pre * jax.nn.sigmoid(pre)
    out = _dot(hdn.astype(BF16), w2_ref[0])
    out = _rope(out, cos_ref[0], sin_ref[0])
    row = lax.broadcasted_iota(jnp.int32, (N_CMP_PAD, HEAD_DIM), 0)
    out = jnp.where(row < N_CMP, out, 0.0)
    o_ref[0, 0] = out.astype(BF16)
    ot_ref[0, 0] = out.T.astype(BF16)


def _compress(proj, pe, w1, w2, cos_c, sin_c):
    return pl.pallas_call(
        _cmp_kernel,
        grid=(2, N_KV_HEADS),
        in_specs=[
            pl.BlockSpec((SEQ, HEAD_DIM), lambda k, h: (0, COL_KC // HEAD_DIM + N_KV_HEADS * k + h)),
            pl.BlockSpec((1, CMP_BLOCK, HEAD_DIM), lambda k, h: (k, 0, 0)),
            pl.BlockSpec((1, CMP_BLOCK, HEAD_DIM, CMP_HIDDEN), lambda k, h: (k, 0, 0, 0)),
            pl.BlockSpec((1, CMP_HIDDEN, HEAD_DIM), lambda k, h: (k, 0, 0)),
            pl.BlockSpec((1, N_CMP_PAD, HEAD_DIM), lambda k, h: (k, 0, 0)),
            pl.BlockSpec((1, N_CMP_PAD, HEAD_DIM), lambda k, h: (k, 0, 0)),
        ],
        out_specs=[
            pl.BlockSpec((1, 1, N_CMP_PAD, HEAD_DIM), lambda k, h: (k, h, 0, 0)),
            pl.BlockSpec((1, 1, HEAD_DIM, N_CMP_PAD), lambda k, h: (k, h, 0, 0)),
        ],
        out_shape=[
            jax.ShapeDtypeStruct((2, N_KV_HEADS, N_CMP_PAD, HEAD_DIM), BF16),
            jax.ShapeDtypeStruct((2, N_KV_HEADS, HEAD_DIM, N_CMP_PAD), BF16),
        ],
        scratch_shapes=[pltpu.VMEM((N_CMP_PAD + 8, CMP_HIDDEN), F32)],
        compiler_params=pltpu.CompilerParams(
            dimension_semantics=("arbitrary", "arbitrary"), vmem_limit_bytes=VMEM_LIMIT),
        name="compress",
    )(proj, pe, w1, w2, cos_c, sin_c)


IMP_PAD = 8
NH = 2
N_STAT = 3


def _attn_kernel(q_ref, kc_ref, vct_ref, ks_ref, vst_ref, kw_ref, vwt_ref, g_ref, gn_ref,
                 o_ref, imp_scr, sel_scr, oc_scr, ow_scr, s_scr, p_scr, acc_scr, stat_scr):
    i = pl.program_id(1)
    s0 = i * AQ
    nrow = AR
    heads = range(NH)

    q_t = [jnp.concatenate([q_ref[hh * GROUP + g] for g in range(GROUP)], axis=1) for hh in heads]

    def mask_groups(s, mask):
        return jnp.concatenate(
            [jnp.where(mask, s[:, g * AQ:(g + 1) * AQ], NEG) for g in range(GROUP)], axis=1)

    def compressed(n_chunks):
        rows = 128 * n_chunks
        mrows = min(rows, CMP_MASK_ROWS)
        c0 = rows - mrows
        r_i = lax.broadcasted_iota(jnp.int32, (mrows, AQ), 0)
        t_i = lax.broadcasted_iota(jnp.int32, (mrows, AQ), 1)
        visible = CMP_STRIDE * r_i - t_i <= s0 - (CMP_BLOCK - 1) - CMP_STRIDE * c0
        rest = IMP_PAD + N_CMP_PAD + 8 - (IMP_PAD + rows)
        for hh in heads:
            sc = _dot(kc_ref[0, hh, 0:rows, :], q_t[hh])
            tail = mask_groups(sc[c0:rows], visible)
            sc = tail if c0 == 0 else jnp.concatenate([sc[0:c0], tail], axis=0)
            m_c = jnp.max(sc, axis=0, keepdims=True)
            m_c = jnp.where(m_c > 0.5 * NEG, m_c, 0.0)
            e_c = jnp.exp2(sc - m_c)
            l_c = jnp.sum(e_c, axis=0, keepdims=True)
            p_c = e_c * (1.0 / jnp.where(l_c > 0.0, l_c, 1.0))
            oc_scr[hh] = _dot(vct_ref[0, hh, :, 0:rows], p_c.astype(BF16))
            imp_c = (p_c[:, 0:AQ] + p_c[:, AQ:2 * AQ]) + (p_c[:, 2 * AQ:3 * AQ] + p_c[:, 3 * AQ:4 * AQ])
            for a in range(AQ // 128):
                imp_scr[hh, a, IMP_PAD:IMP_PAD + rows, :] = imp_c[:, a * 128:(a + 1) * 128]
                imp_scr[hh, a, IMP_PAD + rows:IMP_PAD + rows + rest, :] = jnp.zeros((rest, 128), F32)

    for hh in heads:
        for a in range(AQ // 128):
            imp_scr[hh, a, 0:IMP_PAD, :] = jnp.zeros((IMP_PAD, 128), F32)
    n_cmp_chunks = i // (128 * CMP_STRIDE // AQ) + 1
    for n in range(1, N_CMP_PAD // 128 + 1):
        pl.when(n_cmp_chunks == n)(functools.partial(compressed, n))

    ratio = SEL_BLOCK // CMP_STRIDE

    def importance(hh):
        tap = lambda k: jnp.concatenate(
            [imp_scr[hh, a, pl.ds(IMP_PAD + k, N_BLK, stride=ratio), :] for a in range(AQ // 128)], axis=1)
        return tap(-1) + 2.0 * (tap(0) + tap(1) + tap(2)) + tap(3)

    jidx = lax.broadcasted_iota(jnp.int32, (N_BLK, AQ), 0).astype(F32)
    tok_l = lax.broadcasted_iota(jnp.int32, (1, AQ), 1)
    jt = (i * (AQ // SEL_BLOCK) + tok_l // SEL_BLOCK).astype(F32)
    miscount = []
    for hh in heads:
        v = jnp.where(jidx == 0.0, REMOVED, importance(hh))
        v = jnp.where(jidx == jt, REMOVED, v)
        v = jnp.where(jidx == jt - 1.0, REMOVED, v)
        v = jnp.where(jidx > jt, NEG, v)
        for _ in range(N_SEL - 3):
            v = jnp.where(v == jnp.max(v, axis=0, keepdims=True), REMOVED, v)
        sel_fast = jnp.where(jidx > jt, 0.0, jnp.where(v == REMOVED, 1.0, 0.0))
        n_picked = jnp.sum(sel_fast, axis=0, keepdims=True)
        miscount.append(jnp.max(jnp.abs(n_picked - jnp.minimum(jt + 1.0, float(N_SEL)))))
        sel_scr[hh] = sel_fast

    flag_pen = jnp.where(lax.broadcasted_iota(jnp.int32, (HEAD_DIM, nrow), 0) == 0, NEG, 0.0).astype(BF16)
    wk0 = pl.multiple_of(s0, AQ)
    r_i = lax.broadcasted_iota(jnp.int32, (AQ, AQ), 0)
    t_i = lax.broadcasted_iota(jnp.int32, (AQ, AQ), 1)
    tri_old = jnp.concatenate([jnp.where(r_i > t_i, 0.0, NEG).astype(BF16)] * GROUP, axis=1)
    tri_causal = jnp.concatenate([jnp.where(r_i <= t_i, 0.0, NEG).astype(BF16)] * GROUP, axis=1)
    for hh in heads:
        q_win = jnp.concatenate([q_t[hh], flag_pen], axis=0)
        sw = _dot(kw_ref[hh, pl.ds(wk0, WIN_KEYS), :], q_win).astype(BF16)
        sw = jnp.concatenate([sw[0:AQ] + tri_old, sw[AQ:WINDOW], sw[WINDOW:WIN_KEYS] + tri_causal], axis=0)
        m_w = jnp.max(sw, axis=0, keepdims=True)
        p_w = jnp.exp2(sw - m_w)
        ow_scr[hh] = _dot(vwt_ref[hh, :, pl.ds(wk0, WIN_KEYS)], p_w)
        s_scr[hh] = _dot(ks_ref[hh, 0:SEL_TK, 0:HEAD_DIM], q_t[hh]).astype(BF16)

    @pl.when(functools.reduce(jnp.maximum, miscount) > 0.5)
    def _():
        for hh in heads:
            w = jnp.where(jidx == 0.0, FORCE, importance(hh))
            w = jnp.where(jidx == jt, FORCE, w)
            w = jnp.where(jidx == jt - 1.0, FORCE, w)
            w = jnp.where(jidx > jt, NEG, w)
            sel = jnp.zeros((N_BLK, AQ), F32)
            for _ in range(N_SEL):
                mx = jnp.max(w, axis=0, keepdims=True)
                cand = jnp.where(w == mx, jidx, float(N_BLK))
                pick = jidx == jnp.min(cand, axis=0, keepdims=True)
                sel = jnp.where(pick, 1.0, sel)
                w = jnp.where(pick, REMOVED, w)
            sel_scr[hh] = jnp.where(jidx > jt, 0.0, sel)

    q_aug = []
    for hh in heads:
        pen = jnp.where(sel_scr[hh] > 0.5, 0.0, NEG).astype(BF16)
        q_aug.append(jnp.concatenate([q_t[hh], jnp.concatenate([pen] * GROUP, axis=1)], axis=0))

    def key_start(kt):
        return pl.multiple_of(kt * SEL_TK, SEL_TK)

    def col_max(s):
        return jnp.max(s, axis=0, keepdims=True).astype(F32)

    def stat(hh, k):
        return stat_scr[hh * N_STAT + k:hh * N_STAT + k + 1, :]

    def set_stat(hh, k, value):
        stat_scr[hh * N_STAT + k:hh * N_STAT + k + 1, :] = value

    def add_pv(hh, kt, p, alpha):
        pv = _dot(vst_ref[hh, :, pl.ds(key_start(kt), SEL_TK)], p)
        acc_scr[hh] = alpha * acc_scr[hh] + pv

    def step(kt):
        for hh in heads:
            m, m_tile, alpha_prev = stat(hh, 0), stat(hh, 1), stat(hh, 2)
            p_prev = p_scr[hh]
            s_cur = s_scr[hh]
            s_nxt = _dot(ks_ref[hh, pl.ds(key_start(kt + 1), SEL_TK), :], q_aug[hh]).astype(BF16)
            add_pv(hh, jnp.maximum(kt - 1, 0), p_prev, alpha_prev)
            m_new = jnp.maximum(m, m_tile)
            p_scr[hh] = jnp.exp2(s_cur - m_new.astype(BF16))
            s_scr[hh] = s_nxt
            set_stat(hh, 0, m_new)
            set_stat(hh, 1, col_max(s_nxt))
            set_stat(hh, 2, jnp.exp2(m - m_new))

    n_full = (i * AQ) // SEL_TK
    for hh in heads:
        pen0 = jnp.where(sel_scr[hh, 0:SEL_TK // SEL_BLOCK, :] > 0.5, 0.0, NEG)
        pen0 = jnp.concatenate([jnp.broadcast_to(pen0[b:b + 1, :], (SEL_BLOCK, AQ))
                                for b in range(SEL_TK // SEL_BLOCK)], axis=0).astype(BF16)
        s_first = s_scr[hh] + jnp.concatenate([pen0] * GROUP, axis=1)
        s_scr[hh] = s_first
        p_scr[hh] = jnp.zeros((SEL_TK, nrow), BF16)
        acc_scr[hh] = jnp.zeros((V_ROWS, nrow), F32)
        set_stat(hh, 0, jnp.full((1, nrow), NEG, F32))
        set_stat(hh, 1, col_max(s_first))
        set_stat(hh, 2, jnp.ones((1, nrow), F32))

    @pl.loop(0, n_full // 2)
    def _(j):
        step(2 * j)
        step(2 * j + 1)

    @pl.when(n_full % 2 == 1)
    def _():
        step(n_full - 1)

    off = pl.multiple_of(s0 - n_full * SEL_TK, AQ)
    for hh in heads:
        add_pv(hh, jnp.maximum(n_full - 1, 0), p_scr[hh], stat(hh, 2))
        s_scr[hh, pl.ds(off, AQ), :] = s_scr[hh, pl.ds(off, AQ), :] + tri_causal
        s_diag = s_scr[hh]
        m_s = stat(hh, 0)
        m_fin = jnp.maximum(m_s, col_max(s_diag))
        add_pv(hh, n_full, jnp.exp2(s_diag - m_fin.astype(BF16)), jnp.exp2(m_s - m_fin))

    for hh in heads:
        gt = g_ref[:, hh * 128:(hh + 1) * 128].T
        grow = lambda c: jnp.concatenate([gt[c * GROUP + g:c * GROUP + g + 1, :] for g in range(GROUP)], axis=1)
        l_s = acc_scr[hh, HEAD_DIM:HEAD_DIM + 1, :]
        l_w = ow_scr[hh, HEAD_DIM:HEAD_DIM + 1, :]
        out_t = (oc_scr[hh] * grow(0) + acc_scr[hh, 0:HEAD_DIM, :] * (grow(1) / l_s)
                 + ow_scr[hh, 0:HEAD_DIM, :] * (grow(2) / l_w))
        out = jnp.concatenate([out_t[:, g * AQ:(g + 1) * AQ].T for g in range(GROUP)], axis=1)
        cols = slice(hh * GROUP * HEAD_DIM, (hh + 1) * GROUP * HEAD_DIM)
        gn = gn_ref[:, cols]
        o_ref[:, cols] = (out * (gn * jax.nn.sigmoid(gn))).astype(BF16)


def _attention(q_r, kc, vct, ks_aug, vst, kw_aug, vwt, gates, proj):
    head_res = lambda shape: pl.BlockSpec((NH,) + shape, lambda hp, i: (hp, 0, 0), pipeline_mode=pl.Buffered(1))
    width = NH * GROUP * HEAD_DIM
    return pl.pallas_call(
        _attn_kernel,
        grid=(N_KV_HEADS // NH, SEQ // AQ),
        in_specs=[
            pl.BlockSpec((NH * GROUP, HEAD_DIM, AQ), lambda hp, i: (hp, 0, i)),
            pl.BlockSpec((1, NH, N_CMP_PAD, HEAD_DIM), lambda hp, i: (0, hp, 0, 0)),
            pl.BlockSpec((1, NH, HEAD_DIM, N_CMP_PAD), lambda hp, i: (1, hp, 0, 0)),
            head_res((SEQ, 2 * HEAD_DIM)),
            head_res((V_ROWS, SEQ)),
            head_res((WINDOW + SEQ, 2 * HEAD_DIM)),
            head_res((V_ROWS, WINDOW + SEQ)),
            pl.BlockSpec((AQ, NH * 128), lambda hp, i: (i, hp)),
            pl.BlockSpec((AQ, width), lambda hp, i: (i, COL_GN // width + hp)),
        ],
        out_specs=pl.BlockSpec((AQ, width), lambda hp, i: (i, hp)),
        out_shape=jax.ShapeDtypeStruct((SEQ, NSA_WIDTH), BF16),
        scratch_shapes=[pltpu.VMEM((NH, AQ // 128, IMP_PAD + N_CMP_PAD + 8, 128), F32),
                        pltpu.VMEM((NH, N_BLK, AQ), F32),
                        pltpu.VMEM((NH, HEAD_DIM, AR), F32),
                        pltpu.VMEM((NH, V_ROWS, AR), F32),
                        pltpu.VMEM((NH, SEL_TK, AR), BF16),
                        pltpu.VMEM((NH, SEL_TK, AR), BF16),
                        pltpu.VMEM((NH, V_ROWS, AR), F32),
                        pltpu.VMEM((NH * N_STAT, AR), F32)],
        compiler_params=pltpu.CompilerParams(
            dimension_semantics=("arbitrary", "arbitrary"), vmem_limit_bytes=VMEM_LIMIT),
        name="attn",
    )(q_r, kc, vct, ks_aug, vst, kw_aug, vwt, gates, proj)


def _pool_kernel(u_ref, up_ref, gp_ref, mix_ref, sc_ref, wpo_ref, ya_ref, ext_scr):
    i = pl.program_id(0)
    tm = POOL_TM
    ext_scr[0:POOL_HALO, :] = jnp.where(i > 0, up_ref[...], 0.0)
    ext_scr[POOL_HALO:POOL_HALO + tm, :] = u_ref[...]
    t1 = i * tm + lax.broadcasted_iota(jnp.int32, (tm, 1), 0) + 1
    zs = []
    for g, w in enumerate(POOL_WINDOWS):
        cs = slice(g * POOL_GROUP, (g + 1) * POOL_GROUP)
        u = ext_scr[POOL_HALO:POOL_HALO + tm, cs]
        acc = u
        for k in range(1, w):
            acc = acc + ext_scr[POOL_HALO - k:POOL_HALO - k + tm, cs]
        cnt = jnp.minimum(t1, w).astype(F32)
        pooled = acc / cnt - u
        zs.append(_dot(pooled.astype(BF16), mix_ref[g]))
    z = jnp.concatenate(zs, axis=1) * sc_ref[...]
    gp = gp_ref[...]
    y = z * (gp * jax.nn.sigmoid(gp))
    ya_ref[...] = _dot(y.astype(BF16), wpo_ref[...])


def _pool(proj, mix, scale, w_pool_out):
    tm = POOL_TM
    return pl.pallas_call(
        _pool_kernel,
        grid=(SEQ // tm,),
        in_specs=[
            pl.BlockSpec((tm, POOL_WIDTH), lambda i: (i, COL_U // POOL_WIDTH)),
            pl.BlockSpec((POOL_HALO, POOL_WIDTH),
                         lambda i: (jnp.maximum(i * (tm // POOL_HALO) - 1, 0), COL_U // POOL_WIDTH)),
            pl.BlockSpec((tm, POOL_WIDTH), lambda i: (i, COL_GP // POOL_WIDTH)),
            pl.BlockSpec((len(POOL_WINDOWS), POOL_GROUP, POOL_GROUP), lambda i: (0, 0, 0)),
            pl.BlockSpec((1, POOL_WIDTH), lambda i: (0, 0)),
            pl.BlockSpec((POOL_WIDTH, D_MODEL), lambda i: (0, 0)),
        ],
        out_specs=pl.BlockSpec((tm, D_MODEL), lambda i: (i, 0)),
        out_shape=jax.ShapeDtypeStruct((SEQ, D_MODEL), F32),
        scratch_shapes=[pltpu.VMEM((POOL_HALO + tm, POOL_WIDTH), F32)],
        compiler_params=pltpu.CompilerParams(
            dimension_semantics=("arbitrary",), vmem_limit_bytes=VMEM_LIMIT),
        name="pool",
    )(proj, proj, proj, mix, scale, w_pool_out)


def _out_kernel(x_ref, ya_ref, o_ref, gm0_ref, gm1_ref, wn_ref, wo_ref, fw_ref, out_ref):
    y_b = _dot(o_ref[...], wn_ref[...])
    merged = gm0_ref[...] * ya_ref[...] + gm1_ref[...] * y_b
    r = x_ref[...] + _dot(merged.astype(BF16), wo_ref[...])
    ms = jnp.mean(r * r, axis=-1, keepdims=True)
    out_ref[...] = r * lax.rsqrt(ms + EPS) * fw_ref[...]


def _output(x2, y_a, o_gated, proj, w_nsa_out, w_out, final_norm_w):
    tm = OUT_TM
    row = lambda: pl.BlockSpec((tm, D_MODEL), lambda i: (i, 0))
    resident = lambda: pl.BlockSpec((D_MODEL, D_MODEL), lambda i: (0, 0), pipeline_mode=pl.Buffered(1))
    return pl.pallas_call(
        _out_kernel,
        grid=(SEQ // tm,),
        in_specs=[
            row(), row(), row(),
            pl.BlockSpec((tm, D_MODEL), lambda i: (i, 0)),
            pl.BlockSpec((tm, D_MODEL), lambda i: (i, 1)),
            resident(), resident(),
            pl.BlockSpec((1, D_MODEL), lambda i: (0, 0)),
        ],
        out_specs=row(),
        out_shape=jax.ShapeDtypeStruct((SEQ, D_MODEL), F32),
        compiler_params=pltpu.CompilerParams(
            dimension_semantics=("arbitrary",), vmem_limit_bytes=VMEM_LIMIT),
        name="out",
    )(x2, y_a, o_gated, proj, proj, w_nsa_out, w_out, final_norm_w)


def _rope_tables():
    half = HEAD_DIM // 2
    inv = ROPE_THETA ** (-np.arange(half, dtype=np.float64) / half)
    ang = np.arange(SEQ, dtype=np.float64)[:, None] * inv[None, :]
    cos = np.concatenate([np.cos(ang), np.cos(ang)], axis=-1).astype(np.float32)
    sin = np.concatenate([-np.sin(ang), np.sin(ang)], axis=-1).astype(np.float32)
    pad = lambda t: np.pad(t[CMP_BLOCK - 1::CMP_STRIDE], ((0, N_CMP_PAD - N_CMP), (0, 0)))
    cos_c = np.stack([pad(cos), np.ones((N_CMP_PAD, HEAD_DIM), np.float32)])
    sin_c = np.stack([pad(sin), np.zeros((N_CMP_PAD, HEAD_DIM), np.float32)])
    return cos, sin, cos_c, sin_c


def _gate_weight(w_in_l):
    wg = w_in_l[:, W_IN_MAIN:].reshape(D_MODEL, N_KV_HEADS, GROUP, 3)
    wg = wg.transpose(0, 1, 3, 2).reshape(D_MODEL, N_KV_HEADS, 3 * GROUP)
    wg = jnp.pad(wg, ((0, 0), (0, 0), (0, 128 - 3 * GROUP)))
    return wg.reshape(D_MODEL, N_KV_HEADS * 128).astype(BF16)


def kernel(x, norm_w, w_in, pool_mix, pool_scale, cmp_pe_k, cmp_w1_k, cmp_w2_k, cmp_pe_v, cmp_w1_v,
           cmp_w2_v, w_pool_out, w_nsa_out, w_merge, b_merge, w_out, final_norm_w):
    assert x.shape == (1, SEQ, D_MODEL) and norm_w.shape[0] == 1
    x2 = x.reshape(SEQ, D_MODEL)
    w_in_l = w_in.reshape(D_MODEL, -1)
    proj, gates = _proj(x2, norm_w, w_merge[0].astype(BF16), w_in_l, _gate_weight(w_in_l), b_merge)

    cos_tab, sin_tab, cos_c, sin_c = _rope_tables()
    q_r, ks_aug, vst, kw_aug, vwt = _prep(proj, cos_tab, sin_tab)

    pe = jnp.stack([cmp_pe_k[0], cmp_pe_v[0]])
    w1 = jnp.stack([cmp_w1_k[0], cmp_w1_v[0]]).astype(BF16)
    w2 = jnp.stack([cmp_w2_k[0], cmp_w2_v[0]]).astype(BF16)
    cmp_n, cmp_t = _compress(proj, pe, w1, w2, cos_c, sin_c)

    o_gated = _attention(q_r, cmp_n, cmp_t, ks_aug, vst, kw_aug, vwt, gates, proj)

    y_a = _pool(proj, pool_mix[0].astype(BF16), pool_scale, w_pool_out[0].astype(BF16))
    out = _output(x2, y_a, o_gated, proj, w_nsa_out[0].astype(BF16), w_out[0].astype(BF16),
                  final_norm_w[None, :])
    return out.reshape(1, SEQ, D_MODEL)
```

```python
import functools
import math

import numpy as np
import jax
import jax.numpy as jnp
from jax import lax
from jax.experimental import pallas as pl
from jax.experimental.pallas import tpu as pltpu

F32 = jnp.float32
BF16 = jnp.bfloat16

D_MODEL = 2048
SEQ = 8192
POOL_WIDTH = 1024
POOL_WINDOWS = (2, 4, 8, 16)
POOL_GROUP = 256
POOL_HALO = 16
N_Q_HEADS = 16
N_KV_HEADS = 4
GROUP = 4
HEAD_DIM = 128
NSA_WIDTH = 2048
KV_WIDTH = 512
CMP_BLOCK = 32
CMP_STRIDE = 16
CMP_HIDDEN = 256
N_CMP = (SEQ - CMP_BLOCK) // CMP_STRIDE + 1
N_CMP_PAD = 512
SEL_BLOCK = 64
N_BLK = SEQ // SEL_BLOCK
N_SEL = 16
WINDOW = 512
ROPE_THETA = 10000.0
EPS = 1e-6
NEG = -1e30
FORCE = 1e6
REMOVED = -3e38
LOG2E = 1.4426950408889634
Q_SCALE = HEAD_DIM ** -0.5 * LOG2E

COL_GM = 0
COL_U = 4096
COL_GP = 5120
COL_Q = 6144
COL_KC = 8192
COL_VC = 8704
COL_KS = 9216
COL_VS = 9728
COL_KW = 10240
COL_VW = 10752
COL_GN = 11264
N_PROJ = 13312
W_IN_MAIN = 9216
N_GATE = 3 * N_Q_HEADS

VMEM_LIMIT = 56 * 1024 * 1024
PROJ_VMEM_LIMIT = 60000 * 1024

PROJ_TM = 1024
PROJ_TN = 1024
N_GM_TILES = 4096 // PROJ_TN
PREP_TM = 512
SEL_TK = 512
AQ = 256
AR = GROUP * AQ
WIN_KEYS = WINDOW + AQ
CMP_MASK_ROWS = 256
V_ROWS = HEAD_DIM + 16
POOL_TM = 512
OUT_TM = 256


def _dot(a, b):
    return jnp.dot(a, b, preferred_element_type=F32)


def _nt_dot(a, b):
    return lax.dot_general(a, b, (((1,), (1,)), ((), ())), preferred_element_type=F32)


def _proj_kernel(x_ref, nw_ref, wm_ref, wi_ref, wg_ref, b_ref, o_ref, g_ref, h_scr):
    j = pl.program_id(1)

    @pl.when(j == 0)
    def _():
        x = x_ref[...]
        ms = jnp.mean(x * x, axis=-1, keepdims=True)
        hb = (x * lax.rsqrt(ms + EPS) * nw_ref[...]).astype(BF16)
        h_scr[...] = hb
        row = lax.broadcasted_iota(jnp.int32, (128, D_MODEL), 0)
        wg = jnp.where(row < N_GATE, wg_ref[...], 0.0).astype(BF16)
        g_ref[...] = jax.nn.sigmoid(_nt_dot(hb, wg))

    @pl.when(j < N_GM_TILES)
    def _():
        o_ref[...] = jax.nn.sigmoid(_dot(h_scr[...], wm_ref[...]) + b_ref[...])

    @pl.when(j >= N_GM_TILES)
    def _():
        o_ref[...] = _nt_dot(h_scr[...], wi_ref[...].astype(BF16))


def _proj(x2, norm_w, w_merge, w_in, b_merge):
    grid = (SEQ // PROJ_TM, N_PROJ // PROJ_TN)
    return pl.pallas_call(
        _proj_kernel,
        grid=grid,
        in_specs=[
            pl.BlockSpec((PROJ_TM, D_MODEL), lambda i, j: (i, 0)),
            pl.BlockSpec((1, D_MODEL), lambda i, j: (0, 0)),
            pl.BlockSpec((D_MODEL, PROJ_TN), lambda i, j: (0, jnp.minimum(j, N_GM_TILES - 1))),
            pl.BlockSpec((PROJ_TN, D_MODEL), lambda i, j: (jnp.maximum(j - N_GM_TILES, 0), 0)),
            pl.BlockSpec((128, D_MODEL), lambda i, j: (W_IN_MAIN // 128, 0)),
            pl.BlockSpec((1, PROJ_TN), lambda i, j: (0, jnp.minimum(j, N_GM_TILES - 1))),
        ],
        out_specs=[
            pl.BlockSpec((PROJ_TM, PROJ_TN), lambda i, j: (i, j)),
            pl.BlockSpec((PROJ_TM, 128), lambda i, j: (i, 0)),
        ],
        out_shape=[
            jax.ShapeDtypeStruct((SEQ, N_PROJ), F32),
            jax.ShapeDtypeStruct((SEQ, 128), F32),
        ],
        scratch_shapes=[pltpu.VMEM((PROJ_TM, D_MODEL), BF16)],
        compiler_params=pltpu.CompilerParams(
            dimension_semantics=("arbitrary", "arbitrary"), vmem_limit_bytes=PROJ_VMEM_LIMIT),
        name="proj",
    )(x2, norm_w, w_merge, w_in, w_in, b_merge)


def _rope(x, cos, sin_signed):
    return x * cos + pltpu.roll(x, HEAD_DIM // 2, 1) * sin_signed


def _prep_kernel(q_ref, ks_ref, vs_ref, kw_ref, vw_ref, cos_ref, sin_ref,
                 qo_ref, kso_ref, vso_ref, kwo_ref, vwo_ref):
    i = pl.program_id(0)
    t0 = jnp.maximum(i - 1, 0) * PREP_TM
    cos = cos_ref[...]
    sin = sin_ref[...]
    for hq in range(N_Q_HEADS):
        sl = slice(hq * HEAD_DIM, (hq + 1) * HEAD_DIM)
        qo_ref[hq] = (_rope(q_ref[:, sl], cos, sin) * Q_SCALE).T.astype(BF16)
    blk = (t0 + lax.broadcasted_iota(jnp.int32, (PREP_TM, N_BLK), 0)) // SEL_BLOCK
    lane = lax.broadcasted_iota(jnp.int32, (PREP_TM, N_BLK), 1)
    onehot = jnp.where(blk == lane, 1.0, 0.0).astype(BF16)
    pad_flag = jnp.where(lane == 0, 1.0, 0.0)
    is_pad = i == 0
    ones_rows = jnp.where(lax.broadcasted_iota(jnp.int32, (V_ROWS - HEAD_DIM, PREP_TM), 0) == 0,
                          1.0, 0.0).astype(BF16)
    for h in range(N_KV_HEADS):
        sl = slice(h * HEAD_DIM, (h + 1) * HEAD_DIM)
        kso_ref[h, :, 0:HEAD_DIM] = _rope(ks_ref[:, sl], cos, sin).astype(BF16)
        kso_ref[h, :, HEAD_DIM:2 * HEAD_DIM] = onehot
        vso_ref[h, 0:HEAD_DIM, :] = vs_ref[:, sl].T.astype(BF16)
        vso_ref[h, HEAD_DIM:V_ROWS, :] = ones_rows
        kwo_ref[h, :, 0:HEAD_DIM] = jnp.where(is_pad, 0.0, _rope(kw_ref[:, sl], cos, sin)).astype(BF16)
        kwo_ref[h, :, HEAD_DIM:2 * HEAD_DIM] = jnp.where(is_pad, pad_flag, 0.0).astype(BF16)
        vwo_ref[h, 0:HEAD_DIM, :] = jnp.where(is_pad, 0.0, vw_ref[:, sl].T).astype(BF16)
        vwo_ref[h, HEAD_DIM:V_ROWS, :] = ones_rows


def _prep(proj, cos_tab, sin_tab):
    tm = PREP_TM
    assert tm == WINDOW
    tok = lambda i: jnp.maximum(i - 1, 0)
    kv_spec = lambda col: pl.BlockSpec((tm, KV_WIDTH), lambda i: (tok(i), col // KV_WIDTH))
    return pl.pallas_call(
        _prep_kernel,
        grid=(SEQ // tm + 1,),
        in_specs=[
            pl.BlockSpec((tm, NSA_WIDTH), lambda i: (tok(i), COL_Q // NSA_WIDTH)),
            kv_spec(COL_KS), kv_spec(COL_VS), kv_spec(COL_KW), kv_spec(COL_VW),
            pl.BlockSpec((tm, HEAD_DIM), lambda i: (tok(i), 0)),
            pl.BlockSpec((tm, HEAD_DIM), lambda i: (tok(i), 0)),
        ],
        out_specs=[
            pl.BlockSpec((N_Q_HEADS, HEAD_DIM, tm), lambda i: (0, 0, tok(i))),
            pl.BlockSpec((N_KV_HEADS, tm, 2 * HEAD_DIM), lambda i: (0, tok(i), 0)),
            pl.BlockSpec((N_KV_HEADS, V_ROWS, tm), lambda i: (0, 0, tok(i))),
            pl.BlockSpec((N_KV_HEADS, tm, 2 * HEAD_DIM), lambda i: (0, i, 0)),
            pl.BlockSpec((N_KV_HEADS, V_ROWS, tm), lambda i: (0, 0, i)),
        ],
        out_shape=[
            jax.ShapeDtypeStruct((N_Q_HEADS, HEAD_DIM, SEQ), BF16),
            jax.ShapeDtypeStruct((N_KV_HEADS, SEQ, 2 * HEAD_DIM), BF16),
            jax.ShapeDtypeStruct((N_KV_HEADS, V_ROWS, SEQ), BF16),
            jax.ShapeDtypeStruct((N_KV_HEADS, WINDOW + SEQ, 2 * HEAD_DIM), BF16),
            jax.ShapeDtypeStruct((N_KV_HEADS, V_ROWS, WINDOW + SEQ), BF16),
        ],
        compiler_params=pltpu.CompilerParams(
            dimension_semantics=("arbitrary",), vmem_limit_bytes=VMEM_LIMIT),
        name="prep",
    )(proj, proj, proj, proj, proj, cos_tab, sin_tab)


def _cmp_kernel(x_ref, pe_ref, w1_ref, w2_ref, cos_ref, sin_ref, o_ref, ot_ref, hi_scr):
    half = CMP_BLOCK // 2
    acc_lo = jnp.zeros((N_CMP_PAD, CMP_HIDDEN), F32)
    acc_hi = jnp.zeros((N_CMP_PAD, CMP_HIDDEN), F32)
    for l in range(half):
        xl = x_ref[pl.ds(l, N_CMP_PAD, stride=CMP_STRIDE), :]
        acc_lo += _dot((xl + pe_ref[0, l:l + 1, :]).astype(BF16), w1_ref[0, l])
        acc_hi += _dot((xl + pe_ref[0, half + l:half + l + 1, :]).astype(BF16), w1_ref[0, half + l])
    hi_scr[0:N_CMP_PAD, :] = acc_hi
    hi_scr[N_CMP_PAD:N_CMP_PAD + 8, :] = jnp.zeros((8, CMP_HIDDEN), F32)
    pre = acc_lo + hi_scr[pl.ds(1, N_CMP_PAD), :]
    hdn = pre * jax.nn.sigmoid(pre)
    out = _dot(hdn.astype(BF16), w2_ref[0])
    out = _rope(out, cos_ref[0], sin_ref[0])
    row = lax.broadcasted_iota(jnp.int32, (N_CMP_PAD, HEAD_DIM), 0)
    out = jnp.where(row < N_CMP, out, 0.0)
    o_ref[0, 0] = out.astype(BF16)
    ot_ref[0, 0] = out.T.astype(BF16)


def _compress(proj, pe, w1, w2, cos_c, sin_c):
    return pl.pallas_call(
        _cmp_kernel,
        grid=(2, N_KV_HEADS),
        in_specs=[
            pl.BlockSpec((SEQ, HEAD_DIM), lambda k, h: (0, COL_KC // HEAD_DIM + N_KV_HEADS * k + h)),
            pl.BlockSpec((1, CMP_BLOCK, HEAD_DIM), lambda k, h: (k, 0, 0)),
            pl.BlockSpec((1, CMP_BLOCK, HEAD_DIM, CMP_HIDDEN), lambda k, h: (k, 0, 0, 0)),
            pl.BlockSpec((1, CMP_HIDDEN, HEAD_DIM), lambda k, h: (k, 0, 0)),
            pl.BlockSpec((1, N_CMP_PAD, HEAD_DIM), lambda k, h: (k, 0, 0)),
            pl.BlockSpec((1, N_CMP_PAD, HEAD_DIM), lambda k, h: (k, 0, 0)),
        ],
        out_specs=[
            pl.BlockSpec((1, 1, N_CMP_PAD, HEAD_DIM), lambda k, h: (k, h, 0, 0)),
            pl.BlockSpec((1, 1, HEAD_DIM, N_CMP_PAD), lambda k, h: (k, h, 0, 0)),
        ],
        out_shape=[
            jax.ShapeDtypeStruct((2, N_KV_HEADS, N_CMP_PAD, HEAD_DIM), BF16),
            jax.ShapeDtypeStruct((2, N_KV_HEADS, HEAD_DIM, N_CMP_PAD), BF16),
        ],
        scratch_shapes=[pltpu.VMEM((N_CMP_PAD + 8, CMP_HIDDEN), F32)],
        compiler_params=pltpu.CompilerParams(
            dimension_semantics=("arbitrary", "arbitrary"), vmem_limit_bytes=VMEM_LIMIT),
        name="compress",
    )(proj, pe, w1, w2, cos_c, sin_c)


IMP_PAD = 8
NH = 2
N_STAT = 3


def _attn_kernel(q_ref, kc_ref, vct_ref, ks_ref, vst_ref, kw_ref, vwt_ref, g_ref, gn_ref,
                 o_ref, imp_scr, sel_scr, oc_scr, ow_scr, s_scr, p_scr, acc_scr, stat_scr, g_scr):
    i = pl.program_id(1)
    s0 = i * AQ
    nrow = AR
    heads = range(NH)

    q_t = [jnp.concatenate([q_ref[hh * GROUP + g] for g in range(GROUP)], axis=1) for hh in heads]

    def mask_groups(s, mask):
        return jnp.concatenate(
            [jnp.where(mask, s[:, g * AQ:(g + 1) * AQ], NEG) for g in range(GROUP)], axis=1)

    def compressed(n_chunks):
        rows = 128 * n_chunks
        mrows = min(rows, CMP_MASK_ROWS)
        c0 = rows - mrows
        r_i = lax.broadcasted_iota(jnp.int32, (mrows, AQ), 0)
        t_i = lax.broadcasted_iota(jnp.int32, (mrows, AQ), 1)
        visible = CMP_STRIDE * r_i - t_i <= s0 - (CMP_BLOCK - 1) - CMP_STRIDE * c0
        rest = IMP_PAD + N_CMP_PAD + 8 - (IMP_PAD + rows)
        for hh in heads:
            sc = _dot(kc_ref[0, hh, 0:rows, :], q_t[hh])
            tail = mask_groups(sc[c0:rows], visible)
            sc = tail if c0 == 0 else jnp.concatenate([sc[0:c0], tail], axis=0)
            m_c = jnp.max(sc, axis=0, keepdims=True)
            m_c = jnp.where(m_c > 0.5 * NEG, m_c, 0.0)
            e_c = jnp.exp2(sc - m_c)
            l_c = jnp.sum(e_c, axis=0, keepdims=True)
            p_c = e_c * (1.0 / jnp.where(l_c > 0.0, l_c, 1.0))
            oc_scr[hh] = _dot(vct_ref[0, hh, :, 0:rows], p_c.astype(BF16))
            imp_c = (p_c[:, 0:AQ] + p_c[:, AQ:2 * AQ]) + (p_c[:, 2 * AQ:3 * AQ] + p_c[:, 3 * AQ:4 * AQ])
            for a in range(AQ // 128):
                imp_scr[hh, a, IMP_PAD:IMP_PAD + rows, :] = imp_c[:, a * 128:(a + 1) * 128]
                imp_scr[hh, a, IMP_PAD + rows:IMP_PAD + rows + rest, :] = jnp.zeros((rest, 128), F32)

    for hh in heads:
        for a in range(AQ // 128):
            imp_scr[hh, a, 0:IMP_PAD, :] = jnp.zeros((IMP_PAD, 128), F32)
    n_cmp_chunks = i // (128 * CMP_STRIDE // AQ) + 1
    for n in range(1, N_CMP_PAD // 128 + 1):
        pl.when(n_cmp_chunks == n)(functools.partial(compressed, n))

    ratio = SEL_BLOCK // CMP_STRIDE

    def importance(hh):
        tap = lambda k: jnp.concatenate(
            [imp_scr[hh, a, pl.ds(IMP_PAD + k, N_BLK, stride=ratio), :] for a in range(AQ // 128)], axis=1)
        return tap(-1) + 2.0 * (tap(0) + tap(1) + tap(2)) + tap(3)

    jidx = lax.broadcasted_iota(jnp.int32, (N_BLK, AQ), 0).astype(F32)
    tok_l = lax.broadcasted_iota(jnp.int32, (1, AQ), 1)
    jt = (i * (AQ // SEL_BLOCK) + tok_l // SEL_BLOCK).astype(F32)
    miscount = []
    for hh in heads:
        v = jnp.where(jidx == 0.0, REMOVED, importance(hh))
        v = jnp.where(jidx == jt, REMOVED, v)
        v = jnp.where(jidx == jt - 1.0, REMOVED, v)
        v = jnp.where(jidx > jt, NEG, v)
        for _ in range(N_SEL - 3):
            v = jnp.where(v == jnp.max(v, axis=0, keepdims=True), REMOVED, v)
        sel_fast = jnp.where(jidx > jt, 0.0, jnp.where(v == REMOVED, 1.0, 0.0))
        n_picked = jnp.sum(sel_fast, axis=0, keepdims=True)
        miscount.append(jnp.max(jnp.abs(n_picked - jnp.minimum(jt + 1.0, float(N_SEL)))))
        sel_scr[hh] = sel_fast

    flag_pen = jnp.where(lax.broadcasted_iota(jnp.int32, (HEAD_DIM, nrow), 0) == 0, NEG, 0.0).astype(BF16)
    wk0 = pl.multiple_of(s0, AQ)
    r_i = lax.broadcasted_iota(jnp.int32, (AQ, AQ), 0)
    t_i = lax.broadcasted_iota(jnp.int32, (AQ, AQ), 1)
    tri_old = jnp.concatenate([jnp.where(r_i > t_i, 0.0, NEG).astype(BF16)] * GROUP, axis=1)
    tri_causal = jnp.concatenate([jnp.where(r_i <= t_i, 0.0, NEG).astype(BF16)] * GROUP, axis=1)
    for hh in heads:
        q_win = jnp.concatenate([q_t[hh], flag_pen], axis=0)
        sw = _dot(kw_ref[hh, pl.ds(wk0, WIN_KEYS), :], q_win).astype(BF16)
        sw = jnp.concatenate([sw[0:AQ] + tri_old, sw[AQ:WINDOW], sw[WINDOW:WIN_KEYS] + tri_causal], axis=0)
        m_w = jnp.max(sw, axis=0, keepdims=True)
        p_w = jnp.exp2(sw - m_w)
        ow_scr[hh] = _dot(vwt_ref[hh, :, pl.ds(wk0, WIN_KEYS)], p_w)
        s_scr[hh] = _dot(ks_ref[hh, 0:SEL_TK, 0:HEAD_DIM], q_t[hh]).astype(BF16)

    @pl.when(functools.reduce(jnp.maximum, miscount) > 0.5)
    def _():
        for hh in heads:
            w = jnp.where(jidx == 0.0, FORCE, importance(hh))
            w = jnp.where(jidx == jt, FORCE, w)
            w = jnp.where(jidx == jt - 1.0, FORCE, w)
            w = jnp.where(jidx > jt, NEG, w)
            sel = jnp.zeros((N_BLK, AQ), F32)
            for _ in range(N_SEL):
                mx = jnp.max(w, axis=0, keepdims=True)
                cand = jnp.where(w == mx, jidx, float(N_BLK))
                pick = jidx == jnp.min(cand, axis=0, keepdims=True)
                sel = jnp.where(pick, 1.0, sel)
                w = jnp.where(pick, REMOVED, w)
            sel_scr[hh] = jnp.where(jidx > jt, 0.0, sel)

    q_aug = []
    for hh in heads:
        pen = jnp.where(sel_scr[hh] > 0.5, 0.0, NEG).astype(BF16)
        q_aug.append(jnp.concatenate([q_t[hh], jnp.concatenate([pen] * GROUP, axis=1)], axis=0))

    def key_start(kt):
        return pl.multiple_of(kt * SEL_TK, SEL_TK)

    def col_max(s):
        return jnp.max(s, axis=0, keepdims=True).astype(F32)

    def stat(hh, k):
        return stat_scr[hh * N_STAT + k:hh * N_STAT + k + 1, :]

    def set_stat(hh, k, value):
        stat_scr[hh * N_STAT + k:hh * N_STAT + k + 1, :] = value

    def add_pv(hh, kt, p, alpha):
        pv = _dot(vst_ref[hh, :, pl.ds(key_start(kt), SEL_TK)], p)
        acc_scr[hh] = alpha * acc_scr[hh] + pv

    def step(kt):
        for hh in heads:
            m, m_tile, alpha_prev = stat(hh, 0), stat(hh, 1), stat(hh, 2)
            p_prev = p_scr[hh]
            s_cur = s_scr[hh]
            s_nxt = _dot(ks_ref[hh, pl.ds(key_start(kt + 1), SEL_TK), :], q_aug[hh]).astype(BF16)
            add_pv(hh, jnp.maximum(kt - 1, 0), p_prev, alpha_prev)
            m_new = jnp.maximum(m, m_tile)
            p_scr[hh] = jnp.exp2(s_cur - m_new.astype(BF16))
            s_scr[hh] = s_nxt
            set_stat(hh, 0, m_new)
            set_stat(hh, 1, col_max(s_nxt))
            set_stat(hh, 2, jnp.exp2(m - m_new))

    n_full = (i * AQ) // SEL_TK
    for hh in heads:
        pen0 = jnp.where(sel_scr[hh, 0:SEL_TK // SEL_BLOCK, :] > 0.5, 0.0, NEG)
        pen0 = jnp.concatenate([jnp.broadcast_to(pen0[b:b + 1, :], (SEL_BLOCK, AQ))
                                for b in range(SEL_TK // SEL_BLOCK)], axis=0).astype(BF16)
        s_first = s_scr[hh] + jnp.concatenate([pen0] * GROUP, axis=1)
        s_scr[hh] = s_first
        p_scr[hh] = jnp.zeros((SEL_TK, nrow), BF16)
        acc_scr[hh] = jnp.zeros((V_ROWS, nrow), F32)
        set_stat(hh, 0, jnp.full((1, nrow), NEG, F32))
        set_stat(hh, 1, col_max(s_first))
        set_stat(hh, 2, jnp.ones((1, nrow), F32))

    @pl.loop(0, n_full // 2)
    def _(j):
        step(2 * j)
        step(2 * j + 1)

    @pl.when(n_full % 2 == 1)
    def _():
        step(n_full - 1)

    off = pl.multiple_of(s0 - n_full * SEL_TK, AQ)
    for hh in heads:
        add_pv(hh, jnp.maximum(n_full - 1, 0), p_scr[hh], stat(hh, 2))
        s_scr[hh, pl.ds(off, AQ), :] = s_scr[hh, pl.ds(off, AQ), :] + tri_causal
        s_diag = s_scr[hh]
        m_s = stat(hh, 0)
        m_fin = jnp.maximum(m_s, col_max(s_diag))
        add_pv(hh, n_full, jnp.exp2(s_diag - m_fin.astype(BF16)), jnp.exp2(m_s - m_fin))

    g_scr[...] = g_ref[...].T
    for hh in heads:
        row0 = ((pl.program_id(0) * NH + hh) * GROUP) * 3
        grow = lambda c: jnp.concatenate(
            [g_scr[pl.ds(row0 + g * 3 + c, 1), :] for g in range(GROUP)], axis=1)
        l_s = acc_scr[hh, HEAD_DIM:HEAD_DIM + 1, :]
        l_w = ow_scr[hh, HEAD_DIM:HEAD_DIM + 1, :]
        out_t = (oc_scr[hh] * grow(0) + acc_scr[hh, 0:HEAD_DIM, :] * (grow(1) / l_s)
                 + ow_scr[hh, 0:HEAD_DIM, :] * (grow(2) / l_w))
        out = jnp.concatenate([out_t[:, g * AQ:(g + 1) * AQ].T for g in range(GROUP)], axis=1)
        cols = slice(hh * GROUP * HEAD_DIM, (hh + 1) * GROUP * HEAD_DIM)
        gn = gn_ref[:, cols]
        o_ref[:, cols] = (out * (gn * jax.nn.sigmoid(gn))).astype(BF16)


def _attention(q_r, kc, vct, ks_aug, vst, kw_aug, vwt, gates, proj):
    head_res = lambda shape: pl.BlockSpec((NH,) + shape, lambda hp, i: (hp, 0, 0), pipeline_mode=pl.Buffered(1))
    width = NH * GROUP * HEAD_DIM
    return pl.pallas_call(
        _attn_kernel,
        grid=(N_KV_HEADS // NH, SEQ // AQ),
        in_specs=[
            pl.BlockSpec((NH * GROUP, HEAD_DIM, AQ), lambda hp, i: (hp, 0, i)),
            pl.BlockSpec((1, NH, N_CMP_PAD, HEAD_DIM), lambda hp, i: (0, hp, 0, 0)),
            pl.BlockSpec((1, NH, HEAD_DIM, N_CMP_PAD), lambda hp, i: (1, hp, 0, 0)),
            head_res((SEQ, 2 * HEAD_DIM)),
            head_res((V_ROWS, SEQ)),
            head_res((WINDOW + SEQ, 2 * HEAD_DIM)),
            head_res((V_ROWS, WINDOW + SEQ)),
            pl.BlockSpec((AQ, 128), lambda hp, i: (i, 0)),
            pl.BlockSpec((AQ, width), lambda hp, i: (i, COL_GN // width + hp)),
        ],
        out_specs=pl.BlockSpec((AQ, width), lambda hp, i: (i, hp)),
        out_shape=jax.ShapeDtypeStruct((SEQ, NSA_WIDTH), BF16),
        scratch_shapes=[pltpu.VMEM((NH, AQ // 128, IMP_PAD + N_CMP_PAD + 8, 128), F32),
                        pltpu.VMEM((NH, N_BLK, AQ), F32),
                        pltpu.VMEM((NH, HEAD_DIM, AR), F32),
                        pltpu.VMEM((NH, V_ROWS, AR), F32),
                        pltpu.VMEM((NH, SEL_TK, AR), BF16),
                        pltpu.VMEM((NH, SEL_TK, AR), BF16),
                        pltpu.VMEM((NH, V_ROWS, AR), F32),
                        pltpu.VMEM((NH * N_STAT, AR), F32),
                        pltpu.VMEM((128, AQ), F32)],
        compiler_params=pltpu.CompilerParams(
            dimension_semantics=("arbitrary", "arbitrary"), vmem_limit_bytes=VMEM_LIMIT),
        name="attn",
    )(q_r, kc, vct, ks_aug, vst, kw_aug, vwt, gates, proj)


def _pool_kernel(u_ref, up_ref, gp_ref, mix_ref, sc_ref, wpo_ref, ya_ref, ext_scr):
    i = pl.program_id(0)
    tm = POOL_TM
    ext_scr[0:POOL_HALO, :] = jnp.where(i > 0, up_ref[...], 0.0)
    ext_scr[POOL_HALO:POOL_HALO + tm, :] = u_ref[...]
    t1 = i * tm + lax.broadcasted_iota(jnp.int32, (tm, 1), 0) + 1
    zs = []
    for g, w in enumerate(POOL_WINDOWS):
        cs = slice(g * POOL_GROUP, (g + 1) * POOL_GROUP)
        u = ext_scr[POOL_HALO:POOL_HALO + tm, cs]
        acc = u
        for k in range(1, w):
            acc = acc + ext_scr[POOL_HALO - k:POOL_HALO - k + tm, cs]
        cnt = jnp.minimum(t1, w).astype(F32)
        pooled = acc / cnt - u
        zs.append(_dot(pooled.astype(BF16), mix_ref[g]))
    z = jnp.concatenate(zs, axis=1) * sc_ref[...]
    gp = gp_ref[...]
    y = z * (gp * jax.nn.sigmoid(gp))
    ya_ref[...] = _dot(y.astype(BF16), wpo_ref[...])


def _pool(proj, mix, scale, w_pool_out):
    tm = POOL_TM
    return pl.pallas_call(
        _pool_kernel,
        grid=(SEQ // tm,),
        in_specs=[
            pl.BlockSpec((tm, POOL_WIDTH), lambda i: (i, COL_U // POOL_WIDTH)),
            pl.BlockSpec((POOL_HALO, POOL_WIDTH),
                         lambda i: (jnp.maximum(i * (tm // POOL_HALO) - 1, 0), COL_U // POOL_WIDTH)),
            pl.BlockSpec((tm, POOL_WIDTH), lambda i: (i, COL_GP // POOL_WIDTH)),
            pl.BlockSpec((len(POOL_WINDOWS), POOL_GROUP, POOL_GROUP), lambda i: (0, 0, 0)),
            pl.BlockSpec((1, POOL_WIDTH), lambda i: (0, 0)),
            pl.BlockSpec((POOL_WIDTH, D_MODEL), lambda i: (0, 0)),
        ],
        out_specs=pl.BlockSpec((tm, D_MODEL), lambda i: (i, 0)),
        out_shape=jax.ShapeDtypeStruct((SEQ, D_MODEL), F32),
        scratch_shapes=[pltpu.VMEM((POOL_HALO + tm, POOL_WIDTH), F32)],
        compiler_params=pltpu.CompilerParams(
            dimension_semantics=("arbitrary",), vmem_limit_bytes=VMEM_LIMIT),
        name="pool",
    )(proj, proj, proj, mix, scale, w_pool_out)


def _out_kernel(x_ref, ya_ref, o_ref, gm0_ref, gm1_ref, wn_ref, wo_ref, fw_ref, out_ref):
    y_b = _dot(o_ref[...], wn_ref[...])
    merged = gm0_ref[...] * ya_ref[...] + gm1_ref[...] * y_b
    r = x_ref[...] + _dot(merged.astype(BF16), wo_ref[...])
    ms = jnp.mean(r * r, axis=-1, keepdims=True)
    out_ref[...] = r * lax.rsqrt(ms + EPS) * fw_ref[...]


def _output(x2, y_a, o_gated, proj, w_nsa_out, w_out, final_norm_w):
    tm = OUT_TM
    row = lambda: pl.BlockSpec((tm, D_MODEL), lambda i: (i, 0))
    resident = lambda: pl.BlockSpec((D_MODEL, D_MODEL), lambda i: (0, 0), pipeline_mode=pl.Buffered(1))
    return pl.pallas_call(
        _out_kernel,
        grid=(SEQ // tm,),
        in_specs=[
            row(), row(), row(),
            pl.BlockSpec((tm, D_MODEL), lambda i: (i, 0)),
            pl.BlockSpec((tm, D_MODEL), lambda i: (i, 1)),
            resident(), resident(),
            pl.BlockSpec((1, D_MODEL), lambda i: (0, 0)),
        ],
        out_specs=row(),
        out_shape=jax.ShapeDtypeStruct((SEQ, D_MODEL), F32),
        compiler_params=pltpu.CompilerParams(
            dimension_semantics=("arbitrary",), vmem_limit_bytes=VMEM_LIMIT),
        name="out",
    )(x2, y_a, o_gated, proj, proj, w_nsa_out, w_out, final_norm_w)


def _rope_tables():
    half = HEAD_DIM // 2
    inv = ROPE_THETA ** (-np.arange(half, dtype=np.float64) / half)
    ang = np.arange(SEQ, dtype=np.float64)[:, None] * inv[None, :]
    cos = np.concatenate([np.cos(ang), np.cos(ang)], axis=-1).astype(np.float32)
    sin = np.concatenate([-np.sin(ang), np.sin(ang)], axis=-1).astype(np.float32)
    pad = lambda t: np.pad(t[CMP_BLOCK - 1::CMP_STRIDE], ((0, N_CMP_PAD - N_CMP), (0, 0)))
    cos_c = np.stack([pad(cos), np.ones((N_CMP_PAD, HEAD_DIM), np.float32)])
    sin_c = np.stack([pad(sin), np.zeros((N_CMP_PAD, HEAD_DIM), np.float32)])
    return cos, sin, cos_c, sin_c


def kernel(x, norm_w, w_in, pool_mix, pool_scale, cmp_pe_k, cmp_w1_k, cmp_w2_k, cmp_pe_v, cmp_w1_v,
           cmp_w2_v, w_pool_out, w_nsa_out, w_merge, b_merge, w_out, final_norm_w):
    assert x.shape == (1, SEQ, D_MODEL) and norm_w.shape[0] == 1
    x2 = x.reshape(SEQ, D_MODEL)
    proj, gates = _proj(x2, norm_w, w_merge[0].astype(BF16), jnp.transpose(w_in[0]), b_merge)

    cos_tab, sin_tab, cos_c, sin_c = _rope_tables()
    q_r, ks_aug, vst, kw_aug, vwt = _prep(proj, cos_tab, sin_tab)

    pe = jnp.stack([cmp_pe_k[0], cmp_pe_v[0]])
    w1 = jnp.stack([cmp_w1_k[0], cmp_w1_v[0]]).astype(BF16)
    w2 = jnp.stack([cmp_w2_k[0], cmp_w2_v[0]]).astype(BF16)
    cmp_n, cmp_t = _compress(proj, pe, w1, w2, cos_c, sin_c)

    o_gated = _attention(q_r, cmp_n, cmp_t, ks_aug, vst, kw_aug, vwt, gates, proj)

    y_a = _pool(proj, pool_mix[0].astype(BF16), pool_scale, w_pool_out[0].astype(BF16))
    out = _output(x2, y_a, o_gated, proj, w_nsa_out[0].astype(BF16), w_out[0].astype(BF16),
                  final_norm_w[None, :])
    return out.reshape(1, SEQ, D_MODEL)
```

```python
import functools
import math

import numpy as np
import jax
import jax.numpy as jnp
from jax import lax
from jax.experimental import pallas as pl
from jax.experimental.pallas import tpu as pltpu

F32 = jnp.float32
BF16 = jnp.bfloat16

D_MODEL = 2048
SEQ = 8192
POOL_WIDTH = 1024
POOL_WINDOWS = (2, 4, 8, 16)
POOL_GROUP = 256
POOL_HALO = 16
N_Q_HEADS = 16
N_KV_HEADS = 4
GROUP = 4
HEAD_DIM = 128
NSA_WIDTH = 2048
KV_WIDTH = 512
CMP_BLOCK = 32
CMP_STRIDE = 16
CMP_HIDDEN = 256
N_CMP = (SEQ - CMP_BLOCK) // CMP_STRIDE + 1
N_CMP_PAD = 512
SEL_BLOCK = 64
N_BLK = SEQ // SEL_BLOCK
N_SEL = 16
WINDOW = 512
ROPE_THETA = 10000.0
EPS = 1e-6
NEG = -1e30
FORCE = 1e6
REMOVED = -3e38
LOG2E = 1.4426950408889634
Q_SCALE = HEAD_DIM ** -0.5 * LOG2E

COL_GM = 0
COL_U = 4096
COL_GP = 5120
COL_Q = 6144
COL_KC = 8192
COL_VC = 8704
COL_KS = 9216
COL_VS = 9728
COL_KW = 10240
COL_VW = 10752
COL_GN = 11264
N_PROJ = 13312
W_IN_MAIN = 9216
N_GATE = 3 * N_Q_HEADS

VMEM_LIMIT = 56 * 1024 * 1024
PROJ_VMEM_LIMIT = 60000 * 1024

PROJ_TM = 1024
PROJ_TN = 1024
N_GM_TILES = 4096 // PROJ_TN
PREP_TM = 512
SEL_TK = 512
AQ = 256
AR = GROUP * AQ
WIN_KEYS = WINDOW + AQ
CMP_MASK_ROWS = 256
V_ROWS = HEAD_DIM + 16
POOL_TM = 512
OUT_TM = 256


def _dot(a, b):
    return jnp.dot(a, b, preferred_element_type=F32)


def _nt_dot(a, b):
    return lax.dot_general(a, b, (((1,), (1,)), ((), ())), preferred_element_type=F32)


def _proj_kernel(x_ref, nw_ref, wm_ref, wi_ref, wg_ref, b_ref, o_ref, g_ref, h_scr):
    j = pl.program_id(1)

    @pl.when(j == 0)
    def _():
        x = x_ref[...]
        ms = jnp.mean(x * x, axis=-1, keepdims=True)
        hb = (x * lax.rsqrt(ms + EPS) * nw_ref[...]).astype(BF16)
        h_scr[...] = hb
        row = lax.broadcasted_iota(jnp.int32, (128, D_MODEL), 0)
        wg = jnp.where(row < N_GATE, wg_ref[...], 0.0).astype(BF16)
        g_ref[...] = jax.nn.sigmoid(_nt_dot(hb, wg))

    @pl.when(j < N_GM_TILES)
    def _():
        o_ref[...] = jax.nn.sigmoid(_dot(h_scr[...], wm_ref[...]) + b_ref[...])

    @pl.when(j >= N_GM_TILES)
    def _():
        o_ref[...] = _nt_dot(h_scr[...], wi_ref[...].astype(BF16))


def _proj(x2, norm_w, w_merge, w_in, b_merge):
    grid = (SEQ // PROJ_TM, N_PROJ // PROJ_TN)
    return pl.pallas_call(
        _proj_kernel,
        grid=grid,
        in_specs=[
            pl.BlockSpec((PROJ_TM, D_MODEL), lambda i, j: (i, 0)),
            pl.BlockSpec((1, D_MODEL), lambda i, j: (0, 0)),
            pl.BlockSpec((D_MODEL, PROJ_TN), lambda i, j: (0, jnp.minimum(j, N_GM_TILES - 1))),
            pl.BlockSpec((PROJ_TN, D_MODEL), lambda i, j: (jnp.maximum(j - N_GM_TILES, 0), 0)),
            pl.BlockSpec((128, D_MODEL), lambda i, j: (W_IN_MAIN // 128, 0)),
            pl.BlockSpec((1, PROJ_TN), lambda i, j: (0, jnp.minimum(j, N_GM_TILES - 1))),
        ],
        out_specs=[
            pl.BlockSpec((PROJ_TM, PROJ_TN), lambda i, j: (i, j)),
            pl.BlockSpec((PROJ_TM, 128), lambda i, j: (i, 0)),
        ],
        out_shape=[
            jax.ShapeDtypeStruct((SEQ, N_PROJ), F32),
            jax.ShapeDtypeStruct((SEQ, 128), F32),
        ],
        scratch_shapes=[pltpu.VMEM((PROJ_TM, D_MODEL), BF16)],
        compiler_params=pltpu.CompilerParams(
            dimension_semantics=("arbitrary", "arbitrary"), vmem_limit_bytes=PROJ_VMEM_LIMIT),
        name="proj",
    )(x2, norm_w, w_merge, w_in, w_in, b_merge)


def _rope(x, cos, sin_signed):
    return x * cos + pltpu.roll(x, HEAD_DIM // 2, 1) * sin_signed


def _prep_kernel(q_ref, ks_ref, vs_ref, kw_ref, vw_ref, cos_ref, sin_ref,
                 qo_ref, kso_ref, vso_ref, kwo_ref, vwo_ref):
    i = pl.program_id(0)
    t0 = jnp.maximum(i - 1, 0) * PREP_TM
    cos = cos_ref[...]
    sin = sin_ref[...]
    eye = jnp.where(lax.broadcasted_iota(jnp.int32, (HEAD_DIM, HEAD_DIM), 0)
                    == lax.broadcasted_iota(jnp.int32, (HEAD_DIM, HEAD_DIM), 1), 1.0, 0.0).astype(BF16)
    for hq in range(N_Q_HEADS):
        sl = slice(hq * HEAD_DIM, (hq + 1) * HEAD_DIM)
        q_rot = (_rope(q_ref[:, sl], cos, sin) * Q_SCALE).astype(BF16)
        qo_ref[hq] = _nt_dot(eye, q_rot).astype(BF16)
    blk = (t0 + lax.broadcasted_iota(jnp.int32, (PREP_TM, N_BLK), 0)) // SEL_BLOCK
    lane = lax.broadcasted_iota(jnp.int32, (PREP_TM, N_BLK), 1)
    onehot = jnp.where(blk == lane, 1.0, 0.0).astype(BF16)
    pad_flag = jnp.where(lane == 0, 1.0, 0.0)
    is_pad = i == 0
    ones_rows = jnp.where(lax.broadcasted_iota(jnp.int32, (V_ROWS - HEAD_DIM, PREP_TM), 0) == 0,
                          1.0, 0.0).astype(BF16)
    for h in range(N_KV_HEADS):
        sl = slice(h * HEAD_DIM, (h + 1) * HEAD_DIM)
        kso_ref[h, :, 0:HEAD_DIM] = _rope(ks_ref[:, sl], cos, sin).astype(BF16)
        kso_ref[h, :, HEAD_DIM:2 * HEAD_DIM] = onehot
        vso_ref[h, 0:HEAD_DIM, :] = _nt_dot(eye, vs_ref[:, sl].astype(BF16)).astype(BF16)
        vso_ref[h, HEAD_DIM:V_ROWS, :] = ones_rows
        kwo_ref[h, :, 0:HEAD_DIM] = jnp.where(is_pad, 0.0, _rope(kw_ref[:, sl], cos, sin)).astype(BF16)
        kwo_ref[h, :, HEAD_DIM:2 * HEAD_DIM] = jnp.where(is_pad, pad_flag, 0.0).astype(BF16)
        vw_t = _nt_dot(eye, vw_ref[:, sl].astype(BF16))
        vwo_ref[h, 0:HEAD_DIM, :] = jnp.where(is_pad, 0.0, vw_t).astype(BF16)
        vwo_ref[h, HEAD_DIM:V_ROWS, :] = ones_rows


def _prep(proj, cos_tab, sin_tab):
    tm = PREP_TM
    assert tm == WINDOW
    tok = lambda i: jnp.maximum(i - 1, 0)
    kv_spec = lambda col: pl.BlockSpec((tm, KV_WIDTH), lambda i: (tok(i), col // KV_WIDTH))
    return pl.pallas_call(
        _prep_kernel,
        grid=(SEQ // tm + 1,),
        in_specs=[
            pl.BlockSpec((tm, NSA_WIDTH), lambda i: (tok(i), COL_Q // NSA_WIDTH)),
            kv_spec(COL_KS), kv_spec(COL_VS), kv_spec(COL_KW), kv_spec(COL_VW),
            pl.BlockSpec((tm, HEAD_DIM), lambda i: (tok(i), 0)),
            pl.BlockSpec((tm, HEAD_DIM), lambda i: (tok(i), 0)),
        ],
        out_specs=[
            pl.BlockSpec((N_Q_HEADS, HEAD_DIM, tm), lambda i: (0, 0, tok(i))),
            pl.BlockSpec((N_KV_HEADS, tm, 2 * HEAD_DIM), lambda i: (0, tok(i), 0)),
            pl.BlockSpec((N_KV_HEADS, V_ROWS, tm), lambda i: (0, 0, tok(i))),
            pl.BlockSpec((N_KV_HEADS, tm, 2 * HEAD_DIM), lambda i: (0, i, 0)),
            pl.BlockSpec((N_KV_HEADS, V_ROWS, tm), lambda i: (0, 0, i)),
        ],
        out_shape=[
            jax.ShapeDtypeStruct((N_Q_HEADS, HEAD_DIM, SEQ), BF16),
            jax.ShapeDtypeStruct((N_KV_HEADS, SEQ, 2 * HEAD_DIM), BF16),
            jax.ShapeDtypeStruct((N_KV_HEADS, V_ROWS, SEQ), BF16),
            jax.ShapeDtypeStruct((N_KV_HEADS, WINDOW + SEQ, 2 * HEAD_DIM), BF16),
            jax.ShapeDtypeStruct((N_KV_HEADS, V_ROWS, WINDOW + SEQ), BF16),
        ],
        compiler_params=pltpu.CompilerParams(
            dimension_semantics=("arbitrary",), vmem_limit_bytes=VMEM_LIMIT),
        name="prep",
    )(proj, proj, proj, proj, proj, cos_tab, sin_tab)


def _cmp_kernel(x_ref, pe_ref, w1_ref, w2_ref, cos_ref, sin_ref, o_ref, ot_ref, hi_scr):
    half = CMP_BLOCK // 2
    acc_lo = jnp.zeros((N_CMP_PAD, CMP_HIDDEN), F32)
    acc_hi = jnp.zeros((N_CMP_PAD, CMP_HIDDEN), F32)
    for l in range(half):
        xl = x_ref[pl.ds(l, N_CMP_PAD, stride=CMP_STRIDE), :]
        acc_lo += _dot((xl + pe_ref[0, l:l + 1, :]).astype(BF16), w1_ref[0, l])
        acc_hi += _dot((xl + pe_ref[0, half + l:half + l + 1, :]).astype(BF16), w1_ref[0, half + l])
    hi_scr[0:N_CMP_PAD, :] = acc_hi
    hi_scr[N_CMP_PAD:N_CMP_PAD + 8, :] = jnp.zeros((8, CMP_HIDDEN), F32)
    pre = acc_lo + hi_scr[pl.ds(1, N_CMP_PAD), :]
    hdn = pre * jax.nn.sigmoid(pre)
    out = _dot(hdn.astype(BF16), w2_ref[0])
    out = _rope(out, cos_ref[0], sin_ref[0])
    row = lax.broadcasted_iota(jnp.int32, (N_CMP_PAD, HEAD_DIM), 0)
    out = jnp.where(row < N_CMP, out, 0.0)
    o_ref[0, 0] = out.astype(BF16)
    ot_ref[0, 0] = out.T.astype(BF16)


def _compress(proj, pe, w1, w2, cos_c, sin_c):
    return pl.pallas_call(
        _cmp_kernel,
        grid=(2, N_KV_HEADS),
        in_specs=[
            pl.BlockSpec((SEQ, HEAD_DIM), lambda k, h: (0, COL_KC // HEAD_DIM + N_KV_HEADS * k + h)),
            pl.BlockSpec((1, CMP_BLOCK, HEAD_DIM), lambda k, h: (k, 0, 0)),
            pl.BlockSpec((1, CMP_BLOCK, HEAD_DIM, CMP_HIDDEN), lambda k, h: (k, 0, 0, 0)),
            pl.BlockSpec((1, CMP_HIDDEN, HEAD_DIM), lambda k, h: (k, 0, 0)),
            pl.BlockSpec((1, N_CMP_PAD, HEAD_DIM), lambda k, h: (k, 0, 0)),
            pl.BlockSpec((1, N_CMP_PAD, HEAD_DIM), lambda k, h: (k, 0, 0)),
        ],
        out_specs=[
            pl.BlockSpec((1, 1, N_CMP_PAD, HEAD_DIM), lambda k, h: (k, h, 0, 0)),
            pl.BlockSpec((1, 1, HEAD_DIM, N_CMP_PAD), lambda k, h: (k, h, 0, 0)),
        ],
        out_shape=[
            jax.ShapeDtypeStruct((2, N_KV_HEADS, N_CMP_PAD, HEAD_DIM), BF16),
            jax.ShapeDtypeStruct((2, N_KV_HEADS, HEAD_DIM, N_CMP_PAD), BF16),
        ],
        scratch_shapes=[pltpu.VMEM((N_CMP_PAD + 8, CMP_HIDDEN), F32)],
        compiler_params=pltpu.CompilerParams(
            dimension_semantics=("arbitrary", "arbitrary"), vmem_limit_bytes=VMEM_LIMIT),
        name="compress",
    )(proj, pe, w1, w2, cos_c, sin_c)


IMP_PAD = 8
NH = 2
N_STAT = 3


def _attn_kernel(q_ref, kc_ref, vct_ref, ks_ref, vst_ref, kw_ref, vwt_ref, g_ref, gn_ref,
                 o_ref, imp_scr, sel_scr, oc_scr, ow_scr, s_scr, p_scr, acc_scr, stat_scr, g_scr):
    i = pl.program_id(1)
    s0 = i * AQ
    nrow = AR
    heads = range(NH)

    q_t = [jnp.concatenate([q_ref[hh * GROUP + g] for g in range(GROUP)], axis=1) for hh in heads]

    def mask_groups(s, mask):
        return jnp.concatenate(
            [jnp.where(mask, s[:, g * AQ:(g + 1) * AQ], NEG) for g in range(GROUP)], axis=1)

    def compressed(n_chunks):
        rows = 128 * n_chunks
        mrows = min(rows, CMP_MASK_ROWS)
        c0 = rows - mrows
        r_i = lax.broadcasted_iota(jnp.int32, (mrows, AQ), 0)
        t_i = lax.broadcasted_iota(jnp.int32, (mrows, AQ), 1)
        visible = CMP_STRIDE * r_i - t_i <= s0 - (CMP_BLOCK - 1) - CMP_STRIDE * c0
        rest = IMP_PAD + N_CMP_PAD + 8 - (IMP_PAD + rows)
        for hh in heads:
            sc = _dot(kc_ref[0, hh, 0:rows, :], q_t[hh])
            tail = mask_groups(sc[c0:rows], visible)
            sc = tail if c0 == 0 else jnp.concatenate([sc[0:c0], tail], axis=0)
            m_c = jnp.max(sc, axis=0, keepdims=True)
            m_c = jnp.where(m_c > 0.5 * NEG, m_c, 0.0)
            e_c = jnp.exp2(sc - m_c)
            l_c = jnp.sum(e_c, axis=0, keepdims=True)
            p_c = e_c * (1.0 / jnp.where(l_c > 0.0, l_c, 1.0))
            oc_scr[hh] = _dot(vct_ref[0, hh, :, 0:rows], p_c.astype(BF16))
            imp_c = (p_c[:, 0:AQ] + p_c[:, AQ:2 * AQ]) + (p_c[:, 2 * AQ:3 * AQ] + p_c[:, 3 * AQ:4 * AQ])
            for a in range(AQ // 128):
                imp_scr[hh, a, IMP_PAD:IMP_PAD + rows, :] = imp_c[:, a * 128:(a + 1) * 128]
                imp_scr[hh, a, IMP_PAD + rows:IMP_PAD + rows + rest, :] = jnp.zeros((rest, 128), F32)

    for hh in heads:
        for a in range(AQ // 128):
            imp_scr[hh, a, 0:IMP_PAD, :] = jnp.zeros((IMP_PAD, 128), F32)
    n_cmp_chunks = i // (128 * CMP_STRIDE // AQ) + 1
    for n in range(1, N_CMP_PAD // 128 + 1):
        pl.when(n_cmp_chunks == n)(functools.partial(compressed, n))

    ratio = SEL_BLOCK // CMP_STRIDE

    def importance(hh):
        tap = lambda k: jnp.concatenate(
            [imp_scr[hh, a, pl.ds(IMP_PAD + k, N_BLK, stride=ratio), :] for a in range(AQ // 128)], axis=1)
        return tap(-1) + 2.0 * (tap(0) + tap(1) + tap(2)) + tap(3)

    jidx = lax.broadcasted_iota(jnp.int32, (N_BLK, AQ), 0).astype(F32)
    tok_l = lax.broadcasted_iota(jnp.int32, (1, AQ), 1)
    jt = (i * (AQ // SEL_BLOCK) + tok_l // SEL_BLOCK).astype(F32)
    miscount = []
    for hh in heads:
        v = jnp.where(jidx == 0.0, REMOVED, importance(hh))
        v = jnp.where(jidx == jt, REMOVED, v)
        v = jnp.where(jidx == jt - 1.0, REMOVED, v)
        v = jnp.where(jidx > jt, NEG, v)
        for _ in range(N_SEL - 3):
            v = jnp.where(v == jnp.max(v, axis=0, keepdims=True), REMOVED, v)
        sel_fast = jnp.where(jidx > jt, 0.0, jnp.where(v == REMOVED, 1.0, 0.0))
        n_picked = jnp.sum(sel_fast, axis=0, keepdims=True)
        miscount.append(jnp.max(jnp.abs(n_picked - jnp.minimum(jt + 1.0, float(N_SEL)))))
        sel_scr[hh] = sel_fast

    flag_pen = jnp.where(lax.broadcasted_iota(jnp.int32, (HEAD_DIM, nrow), 0) == 0, NEG, 0.0).astype(BF16)
    wk0 = pl.multiple_of(s0, AQ)
    r_i = lax.broadcasted_iota(jnp.int32, (AQ, AQ), 0)
    t_i = lax.broadcasted_iota(jnp.int32, (AQ, AQ), 1)
    tri_old = jnp.concatenate([jnp.where(r_i > t_i, 0.0, NEG).astype(BF16)] * GROUP, axis=1)
    tri_causal = jnp.concatenate([jnp.where(r_i <= t_i, 0.0, NEG).astype(BF16)] * GROUP, axis=1)
    for hh in heads:
        q_win = jnp.concatenate([q_t[hh], flag_pen], axis=0)
        sw = _dot(kw_ref[hh, pl.ds(wk0, WIN_KEYS), :], q_win).astype(BF16)
        sw = jnp.concatenate([sw[0:AQ] + tri_old, sw[AQ:WINDOW], sw[WINDOW:WIN_KEYS] + tri_causal], axis=0)
        m_w = jnp.max(sw, axis=0, keepdims=True)
        p_w = jnp.exp2(sw - m_w)
        ow_scr[hh] = _dot(vwt_ref[hh, :, pl.ds(wk0, WIN_KEYS)], p_w)
        s_scr[hh] = _dot(ks_ref[hh, 0:SEL_TK, 0:HEAD_DIM], q_t[hh]).astype(BF16)

    @pl.when(functools.reduce(jnp.maximum, miscount) > 0.5)
    def _():
        for hh in heads:
            w = jnp.where(jidx == 0.0, FORCE, importance(hh))
            w = jnp.where(jidx == jt, FORCE, w)
            w = jnp.where(jidx == jt - 1.0, FORCE, w)
            w = jnp.where(jidx > jt, NEG, w)
            sel = jnp.zeros((N_BLK, AQ), F32)
            for _ in range(N_SEL):
                mx = jnp.max(w, axis=0, keepdims=True)
                cand = jnp.where(w == mx, jidx, float(N_BLK))
                pick = jidx == jnp.min(cand, axis=0, keepdims=True)
                sel = jnp.where(pick, 1.0, sel)
                w = jnp.where(pick, REMOVED, w)
            sel_scr[hh] = jnp.where(jidx > jt, 0.0, sel)

    q_aug = []
    for hh in heads:
        pen = jnp.where(sel_scr[hh] > 0.5, 0.0, NEG).astype(BF16)
        q_aug.append(jnp.concatenate([q_t[hh], jnp.concatenate([pen] * GROUP, axis=1)], axis=0))

    def key_start(kt):
        return pl.multiple_of(kt * SEL_TK, SEL_TK)

    def col_max(s):
        return jnp.max(s, axis=0, keepdims=True).astype(F32)

    def stat(hh, k):
        return stat_scr[hh * N_STAT + k:hh * N_STAT + k + 1, :]

    def set_stat(hh, k, value):
        stat_scr[hh * N_STAT + k:hh * N_STAT + k + 1, :] = value

    def add_pv(hh, kt, p, alpha):
        pv = _dot(vst_ref[hh, :, pl.ds(key_start(kt), SEL_TK)], p)
        acc_scr[hh] = alpha * acc_scr[hh] + pv

    def step(kt):
        for hh in heads:
            m, m_tile, alpha_prev = stat(hh, 0), stat(hh, 1), stat(hh, 2)
            p_prev = p_scr[hh]
            s_cur = s_scr[hh]
            s_nxt = _dot(ks_ref[hh, pl.ds(key_start(kt + 1), SEL_TK), :], q_aug[hh]).astype(BF16)
            add_pv(hh, jnp.maximum(kt - 1, 0), p_prev, alpha_prev)
            m_new = jnp.maximum(m, m_tile)
            p_scr[hh] = jnp.exp2(s_cur - m_new.astype(BF16))
            s_scr[hh] = s_nxt
            set_stat(hh, 0, m_new)
            set_stat(hh, 1, col_max(s_nxt))
            set_stat(hh, 2, jnp.exp2(m - m_new))

    n_full = (i * AQ) // SEL_TK
    for hh in heads:
        pen0 = jnp.where(sel_scr[hh, 0:SEL_TK // SEL_BLOCK, :] > 0.5, 0.0, NEG)
        pen0 = jnp.concatenate([jnp.broadcast_to(pen0[b:b + 1, :], (SEL_BLOCK, AQ))
                                for b in range(SEL_TK // SEL_BLOCK)], axis=0).astype(BF16)
        s_first = s_scr[hh] + jnp.concatenate([pen0] * GROUP, axis=1)
        s_scr[hh] = s_first
        p_scr[hh] = jnp.zeros((SEL_TK, nrow), BF16)
        acc_scr[hh] = jnp.zeros((V_ROWS, nrow), F32)
        set_stat(hh, 0, jnp.full((1, nrow), NEG, F32))
        set_stat(hh, 1, col_max(s_first))
        set_stat(hh, 2, jnp.ones((1, nrow), F32))

    @pl.loop(0, n_full // 2)
    def _(j):
        step(2 * j)
        step(2 * j + 1)

    @pl.when(n_full % 2 == 1)
    def _():
        step(n_full - 1)

    off = pl.multiple_of(s0 - n_full * SEL_TK, AQ)
    for hh in heads:
        add_pv(hh, jnp.maximum(n_full - 1, 0), p_scr[hh], stat(hh, 2))
        s_scr[hh, pl.ds(off, AQ), :] = s_scr[hh, pl.ds(off, AQ), :] + tri_causal
        s_diag = s_scr[hh]
        m_s = stat(hh, 0)
        m_fin = jnp.maximum(m_s, col_max(s_diag))
        add_pv(hh, n_full, jnp.exp2(s_diag - m_fin.astype(BF16)), jnp.exp2(m_s - m_fin))

    g_scr[...] = g_ref[...].T
    for hh in heads:
        row0 = ((pl.program_id(0) * NH + hh) * GROUP) * 3
        grow = lambda c: jnp.concatenate(
            [g_scr[pl.ds(row0 + g * 3 + c, 1), :] for g in range(GROUP)], axis=1)
        l_s = acc_scr[hh, HEAD_DIM:HEAD_DIM + 1, :]
        l_w = ow_scr[hh, HEAD_DIM:HEAD_DIM + 1, :]
        out_t = (oc_scr[hh] * grow(0) + acc_scr[hh, 0:HEAD_DIM, :] * (grow(1) / l_s)
                 + ow_scr[hh, 0:HEAD_DIM, :] * (grow(2) / l_w))
        out = jnp.concatenate([out_t[:, g * AQ:(g + 1) * AQ].T for g in range(GROUP)], axis=1)
        cols = slice(hh * GROUP * HEAD_DIM, (hh + 1) * GROUP * HEAD_DIM)
        gn = gn_ref[:, cols]
        o_ref[:, cols] = (out * (gn * jax.nn.sigmoid(gn))).astype(BF16)


def _attention(q_r, kc, vct, ks_aug, vst, kw_aug, vwt, gates, proj):
    head_res = lambda shape: pl.BlockSpec((NH,) + shape, lambda hp, i: (hp, 0, 0), pipeline_mode=pl.Buffered(1))
    width = NH * GROUP * HEAD_DIM
    return pl.pallas_call(
        _attn_kernel,
        grid=(N_KV_HEADS // NH, SEQ // AQ),
        in_specs=[
            pl.BlockSpec((NH * GROUP, HEAD_DIM, AQ), lambda hp, i: (hp, 0, i)),
            pl.BlockSpec((1, NH, N_CMP_PAD, HEAD_DIM), lambda hp, i: (0, hp, 0, 0)),
            pl.BlockSpec((1, NH, HEAD_DIM, N_CMP_PAD), lambda hp, i: (1, hp, 0, 0)),
            head_res((SEQ, 2 * HEAD_DIM)),
            head_res((V_ROWS, SEQ)),
            head_res((WINDOW + SEQ, 2 * HEAD_DIM)),
            head_res((V_ROWS, WINDOW + SEQ)),
            pl.BlockSpec((AQ, 128), lambda hp, i: (i, 0)),
            pl.BlockSpec((AQ, width), lambda hp, i: (i, COL_GN // width + hp)),
        ],
        out_specs=pl.BlockSpec((AQ, width), lambda hp, i: (i, hp)),
        out_shape=jax.ShapeDtypeStruct((SEQ, NSA_WIDTH), BF16),
        scratch_shapes=[pltpu.VMEM((NH, AQ // 128, IMP_PAD + N_CMP_PAD + 8, 128), F32),
                        pltpu.VMEM((NH, N_BLK, AQ), F32),
                        pltpu.VMEM((NH, HEAD_DIM, AR), F32),
                        pltpu.VMEM((NH, V_ROWS, AR), F32),
                        pltpu.VMEM((NH, SEL_TK, AR), BF16),
                        pltpu.VMEM((NH, SEL_TK, AR), BF16),
                        pltpu.VMEM((NH, V_ROWS, AR), F32),
                        pltpu.VMEM((NH * N_STAT, AR), F32),
                        pltpu.VMEM((128, AQ), F32)],
        compiler_params=pltpu.CompilerParams(
            dimension_semantics=("arbitrary", "arbitrary"), vmem_limit_bytes=VMEM_LIMIT),
        name="attn",
    )(q_r, kc, vct, ks_aug, vst, kw_aug, vwt, gates, proj)


def _pool_kernel(u_ref, up_ref, gp_ref, mix_ref, sc_ref, wpo_ref, ya_ref, ext_scr):
    i = pl.program_id(0)
    tm = POOL_TM
    ext_scr[0:POOL_HALO, :] = jnp.where(i > 0, up_ref[...], 0.0)
    ext_scr[POOL_HALO:POOL_HALO + tm, :] = u_ref[...]
    t1 = i * tm + lax.broadcasted_iota(jnp.int32, (tm, 1), 0) + 1
    zs = []
    for g, w in enumerate(POOL_WINDOWS):
        cs = slice(g * POOL_GROUP, (g + 1) * POOL_GROUP)
        u = ext_scr[POOL_HALO:POOL_HALO + tm, cs]
        acc = u
        for k in range(1, w):
            acc = acc + ext_scr[POOL_HALO - k:POOL_HALO - k + tm, cs]
        cnt = jnp.minimum(t1, w).astype(F32)
        pooled = acc / cnt - u
        zs.append(_dot(pooled.astype(BF16), mix_ref[g]))
    z = jnp.concatenate(zs, axis=1) * sc_ref[...]
    gp = gp_ref[...]
    y = z * (gp * jax.nn.sigmoid(gp))
    ya_ref[...] = _dot(y.astype(BF16), wpo_ref[...])


def _pool(proj, mix, scale, w_pool_out):
    tm = POOL_TM
    return pl.pallas_call(
        _pool_kernel,
        grid=(SEQ // tm,),
        in_specs=[
            pl.BlockSpec((tm, POOL_WIDTH), lambda i: (i, COL_U // POOL_WIDTH)),
            pl.BlockSpec((POOL_HALO, POOL_WIDTH),
                         lambda i: (jnp.maximum(i * (tm // POOL_HALO) - 1, 0), COL_U // POOL_WIDTH)),
            pl.BlockSpec((tm, POOL_WIDTH), lambda i: (i, COL_GP // POOL_WIDTH)),
            pl.BlockSpec((len(POOL_WINDOWS), POOL_GROUP, POOL_GROUP), lambda i: (0, 0, 0)),
            pl.BlockSpec((1, POOL_WIDTH), lambda i: (0, 0)),
            pl.BlockSpec((POOL_WIDTH, D_MODEL), lambda i: (0, 0)),
        ],
        out_specs=pl.BlockSpec((tm, D_MODEL), lambda i: (i, 0)),
        out_shape=jax.ShapeDtypeStruct((SEQ, D_MODEL), F32),
        scratch_shapes=[pltpu.VMEM((POOL_HALO + tm, POOL_WIDTH), F32)],
        compiler_params=pltpu.CompilerParams(
            dimension_semantics=("arbitrary",), vmem_limit_bytes=VMEM_LIMIT),
        name="pool",
    )(proj, proj, proj, mix, scale, w_pool_out)


def _out_kernel(x_ref, ya_ref, o_ref, gm0_ref, gm1_ref, wn_ref, wo_ref, fw_ref, out_ref):
    y_b = _dot(o_ref[...], wn_ref[...])
    merged = gm0_ref[...] * ya_ref[...] + gm1_ref[...] * y_b
    r = x_ref[...] + _dot(merged.astype(BF16), wo_ref[...])
    ms = jnp.mean(r * r, axis=-1, keepdims=True)
    out_ref[...] = r * lax.rsqrt(ms + EPS) * fw_ref[...]


def _output(x2, y_a, o_gated, proj, w_nsa_out, w_out, final_norm_w):
    tm = OUT_TM
    row = lambda: pl.BlockSpec((tm, D_MODEL), lambda i: (i, 0))
    resident = lambda: pl.BlockSpec((D_MODEL, D_MODEL), lambda i: (0, 0), pipeline_mode=pl.Buffered(1))
    return pl.pallas_call(
        _out_kernel,
        grid=(SEQ // tm,),
        in_specs=[
            row(), row(), row(),
            pl.BlockSpec((tm, D_MODEL), lambda i: (i, 0)),
            pl.BlockSpec((tm, D_MODEL), lambda i: (i, 1)),
            resident(), resident(),
            pl.BlockSpec((1, D_MODEL), lambda i: (0, 0)),
        ],
        out_specs=row(),
        out_shape=jax.ShapeDtypeStruct((SEQ, D_MODEL), F32),
        compiler_params=pltpu.CompilerParams(
            dimension_semantics=("arbitrary",), vmem_limit_bytes=VMEM_LIMIT),
        name="out",
    )(x2, y_a, o_gated, proj, proj, w_nsa_out, w_out, final_norm_w)


def _rope_tables():
    half = HEAD_DIM // 2
    inv = ROPE_THETA ** (-np.arange(half, dtype=np.float64) / half)
    ang = np.arange(SEQ, dtype=np.float64)[:, None] * inv[None, :]
    cos = np.concatenate([np.cos(ang), np.cos(ang)], axis=-1).astype(np.float32)
    sin = np.concatenate([-np.sin(ang), np.sin(ang)], axis=-1).astype(np.float32)
    pad = lambda t: np.pad(t[CMP_BLOCK - 1::CMP_STRIDE], ((0, N_CMP_PAD - N_CMP), (0, 0)))
    cos_c = np.stack([pad(cos), np.ones((N_CMP_PAD, HEAD_DIM), np.float32)])
    sin_c = np.stack([pad(sin), np.zeros((N_CMP_PAD, HEAD_DIM), np.float32)])
    return cos, sin, cos_c, sin_c


def kernel(x, norm_w, w_in, pool_mix, pool_scale, cmp_pe_k, cmp_w1_k, cmp_w2_k, cmp_pe_v, cmp_w1_v,
           cmp_w2_v, w_pool_out, w_nsa_out, w_merge, b_merge, w_out, final_norm_w):
    assert x.shape == (1, SEQ, D_MODEL) and norm_w.shape[0] == 1
    x2 = x.reshape(SEQ, D_MODEL)
    proj, gates = _proj(x2, norm_w, w_merge[0].astype(BF16), jnp.transpose(w_in[0]), b_merge)

    cos_tab, sin_tab, cos_c, sin_c = _rope_tables()
    q_r, ks_aug, vst, kw_aug, vwt = _prep(proj, cos_tab, sin_tab)

    pe = jnp.stack([cmp_pe_k[0], cmp_pe_v[0]])
    w1 = jnp.stack([cmp_w1_k[0], cmp_w1_v[0]]).astype(BF16)
    w2 = jnp.stack([cmp_w2_k[0], cmp_w2_v[0]]).astype(BF16)
    cmp_n, cmp_t = _compress(proj, pe, w1, w2, cos_c, sin_c)

    o_gated = _attention(q_r, cmp_n, cmp_t, ks_aug, vst, kw_aug, vwt, gates, proj)

    y_a = _pool(proj, pool_mix[0].astype(BF16), pool_scale, w_pool_out[0].astype(BF16))
    out = _output(x2, y_a, o_gated, proj, w_nsa_out[0].astype(BF16), w_out[0].astype(BF16),
                  final_norm_w[None, :])
    return out.reshape(1, SEQ, D_MODEL)
```

```python
import functools

import numpy as np
import jax
import jax.numpy as jnp
from jax import lax
from jax.experimental import pallas as pl
from jax.experimental.pallas import tpu as pltpu

F32 = jnp.float32
BF16 = jnp.bfloat16

LANES = 128

D_MODEL = 2048
SEQ = 8192
POOL_WIDTH = 1024
POOL_WINDOWS = (2, 4, 8, 16)
POOL_GROUP = 256
POOL_HALO = 16
N_Q_HEADS = 16
N_KV_HEADS = 4
GROUP = 4
HEAD_DIM = 128
NSA_WIDTH = 2048
KV_WIDTH = 512
CMP_BLOCK = 32
CMP_STRIDE = 16
CMP_HIDDEN = 256
N_CMP = (SEQ - CMP_BLOCK) // CMP_STRIDE + 1
N_CMP_PAD = 512
SEL_BLOCK = 64
N_BLK = SEQ // SEL_BLOCK
N_SEL = 16
WINDOW = 512
ROPE_THETA = 10000.0
EPS = 1e-6
NEG = -1e30
FORCE = 1e6
REMOVED = -3e38
LOG2E = 1.4426950408889634
Q_SCALE = HEAD_DIM ** -0.5 * LOG2E

COL_GM = 0
COL_U = 4096
COL_GP = 5120
COL_Q = 6144
COL_KC = 8192
COL_VC = 8704
COL_KS = 9216
COL_VS = 9728
COL_KW = 10240
COL_VW = 10752
COL_GN = 11264
N_PROJ = 13312
W_IN_MAIN = 9216
N_GATE = 3 * N_Q_HEADS

VMEM_LIMIT = 56 * 1024 * 1024
PROJ_VMEM_LIMIT = 60000 * 1024

PROJ_TM = 1024
PROJ_TN = 1024
N_GM_TILES = 4096 // PROJ_TN
PREP_TM = 512
SEL_TK = 512
AQ = 256
AR = GROUP * AQ
WIN_KEYS = WINDOW + AQ
CMP_CHUNK = 128
CMP_MASK_ROWS = 256
V_ROWS = HEAD_DIM + 16
POOL_TM = 512
OUT_TM = 256


def _dot(a, b):
    return jnp.dot(a, b, preferred_element_type=F32)


def _nt_dot(a, b):
    return lax.dot_general(a, b, (((1,), (1,)), ((), ())), preferred_element_type=F32)


def _proj_kernel(x_ref, nw_ref, wm_ref, wi_ref, wg_ref, b_ref, o_ref, g_ref, h_scr):
    j = pl.program_id(1)

    @pl.when(j == 0)
    def _():
        x = x_ref[...]
        ms = jnp.mean(x * x, axis=-1, keepdims=True)
        hb = (x * lax.rsqrt(ms + EPS) * nw_ref[...]).astype(BF16)
        h_scr[...] = hb
        row = lax.broadcasted_iota(jnp.int32, (LANES, D_MODEL), 0)
        wg = jnp.where(row < N_GATE, wg_ref[...], 0.0).astype(BF16)
        g_ref[...] = jax.nn.sigmoid(_nt_dot(hb, wg))

    @pl.when(j < N_GM_TILES)
    def _():
        o_ref[...] = jax.nn.sigmoid(_dot(h_scr[...], wm_ref[...]) + b_ref[...])

    @pl.when(j >= N_GM_TILES)
    def _():
        o_ref[...] = _nt_dot(h_scr[...], wi_ref[...].astype(BF16))


def _proj(x2, norm_w, w_merge, w_in, b_merge):
    grid = (SEQ // PROJ_TM, N_PROJ // PROJ_TN)
    return pl.pallas_call(
        _proj_kernel,
        grid=grid,
        in_specs=[
            pl.BlockSpec((PROJ_TM, D_MODEL), lambda i, j: (i, 0)),
            pl.BlockSpec((1, D_MODEL), lambda i, j: (0, 0)),
            pl.BlockSpec((D_MODEL, PROJ_TN), lambda i, j: (0, jnp.minimum(j, N_GM_TILES - 1))),
            pl.BlockSpec((PROJ_TN, D_MODEL), lambda i, j: (jnp.maximum(j - N_GM_TILES, 0), 0)),
            pl.BlockSpec((LANES, D_MODEL), lambda i, j: (W_IN_MAIN // LANES, 0)),
            pl.BlockSpec((1, PROJ_TN), lambda i, j: (0, jnp.minimum(j, N_GM_TILES - 1))),
        ],
        out_specs=[
            pl.BlockSpec((PROJ_TM, PROJ_TN), lambda i, j: (i, j)),
            pl.BlockSpec((PROJ_TM, LANES), lambda i, j: (i, 0)),
        ],
        out_shape=[
            jax.ShapeDtypeStruct((SEQ, N_PROJ), F32),
            jax.ShapeDtypeStruct((SEQ, LANES), F32),
        ],
        scratch_shapes=[pltpu.VMEM((PROJ_TM, D_MODEL), BF16)],
        compiler_params=pltpu.CompilerParams(
            dimension_semantics=("arbitrary", "arbitrary"), vmem_limit_bytes=PROJ_VMEM_LIMIT),
        name="proj",
    )(x2, norm_w, w_merge, w_in, w_in, b_merge)


def _rope(x, cos, sin_signed):
    return x * cos + pltpu.roll(x, HEAD_DIM // 2, 1) * sin_signed


def _prep_kernel(q_ref, ks_ref, vs_ref, kw_ref, vw_ref, cos_ref, sin_ref,
                 qo_ref, kso_ref, vso_ref, kwo_ref, vwo_ref):
    i = pl.program_id(0)
    t0 = jnp.maximum(i - 1, 0) * PREP_TM
    cos = cos_ref[...]
    sin = sin_ref[...]
    for hq in range(N_Q_HEADS):
        sl = slice(hq * HEAD_DIM, (hq + 1) * HEAD_DIM)
        qo_ref[hq] = (_rope(q_ref[:, sl], cos, sin) * Q_SCALE).T.astype(BF16)
    blk = (t0 + lax.broadcasted_iota(jnp.int32, (PREP_TM, N_BLK), 0)) // SEL_BLOCK
    lane = lax.broadcasted_iota(jnp.int32, (PREP_TM, N_BLK), 1)
    onehot = jnp.where(blk == lane, 1.0, 0.0).astype(BF16)
    pad_flag = jnp.where(lane == 0, 1.0, 0.0)
    is_pad = i == 0
    ones_rows = jnp.where(lax.broadcasted_iota(jnp.int32, (V_ROWS - HEAD_DIM, PREP_TM), 0) == 0,
                          1.0, 0.0).astype(BF16)
    for h in range(N_KV_HEADS):
        sl = slice(h * HEAD_DIM, (h + 1) * HEAD_DIM)
        kso_ref[h, :, 0:HEAD_DIM] = _rope(ks_ref[:, sl], cos, sin).astype(BF16)
        kso_ref[h, :, HEAD_DIM:2 * HEAD_DIM] = onehot
        vso_ref[h, 0:HEAD_DIM, :] = vs_ref[:, sl].T.astype(BF16)
        vso_ref[h, HEAD_DIM:V_ROWS, :] = ones_rows
        kwo_ref[h, :, 0:HEAD_DIM] = jnp.where(is_pad, 0.0, _rope(kw_ref[:, sl], cos, sin)).astype(BF16)
        kwo_ref[h, :, HEAD_DIM:2 * HEAD_DIM] = jnp.where(is_pad, pad_flag, 0.0).astype(BF16)
        vwo_ref[h, 0:HEAD_DIM, :] = jnp.where(is_pad, 0.0, vw_ref[:, sl].T).astype(BF16)
        vwo_ref[h, HEAD_DIM:V_ROWS, :] = ones_rows


def _prep(proj, cos_tab, sin_tab):
    tm = PREP_TM
    assert tm == WINDOW
    tok = lambda i: jnp.maximum(i - 1, 0)
    kv_spec = lambda col: pl.BlockSpec((tm, KV_WIDTH), lambda i: (tok(i), col // KV_WIDTH))
    return pl.pallas_call(
        _prep_kernel,
        grid=(SEQ // tm + 1,),
        in_specs=[
            pl.BlockSpec((tm, NSA_WIDTH), lambda i: (tok(i), COL_Q // NSA_WIDTH)),
            kv_spec(COL_KS), kv_spec(COL_VS), kv_spec(COL_KW), kv_spec(COL_VW),
            pl.BlockSpec((tm, HEAD_DIM), lambda i: (tok(i), 0)),
            pl.BlockSpec((tm, HEAD_DIM), lambda i: (tok(i), 0)),
        ],
        out_specs=[
            pl.BlockSpec((N_Q_HEADS, HEAD_DIM, tm), lambda i: (0, 0, tok(i))),
            pl.BlockSpec((N_KV_HEADS, tm, 2 * HEAD_DIM), lambda i: (0, tok(i), 0)),
            pl.BlockSpec((N_KV_HEADS, V_ROWS, tm), lambda i: (0, 0, tok(i))),
            pl.BlockSpec((N_KV_HEADS, tm, 2 * HEAD_DIM), lambda i: (0, i, 0)),
            pl.BlockSpec((N_KV_HEADS, V_ROWS, tm), lambda i: (0, 0, i)),
        ],
        out_shape=[
            jax.ShapeDtypeStruct((N_Q_HEADS, HEAD_DIM, SEQ), BF16),
            jax.ShapeDtypeStruct((N_KV_HEADS, SEQ, 2 * HEAD_DIM), BF16),
            jax.ShapeDtypeStruct((N_KV_HEADS, V_ROWS, SEQ), BF16),
            jax.ShapeDtypeStruct((N_KV_HEADS, WINDOW + SEQ, 2 * HEAD_DIM), BF16),
            jax.ShapeDtypeStruct((N_KV_HEADS, V_ROWS, WINDOW + SEQ), BF16),
        ],
        compiler_params=pltpu.CompilerParams(
            dimension_semantics=("arbitrary",), vmem_limit_bytes=VMEM_LIMIT),
        name="prep",
    )(proj, proj, proj, proj, proj, cos_tab, sin_tab)


def _cmp_kernel(x_ref, pe_ref, w1_ref, w2_ref, cos_ref, sin_ref, o_ref, ot_ref, hi_scr):
    half = CMP_BLOCK // 2
    acc_lo = jnp.zeros((N_CMP_PAD, CMP_HIDDEN), F32)
    acc_hi = jnp.zeros((N_CMP_PAD, CMP_HIDDEN), F32)
    for l in range(half):
        xl = x_ref[pl.ds(l, N_CMP_PAD, stride=CMP_STRIDE), :]
        acc_lo += _dot((xl + pe_ref[0, l:l + 1, :]).astype(BF16), w1_ref[0, l])
        acc_hi += _dot((xl + pe_ref[0, half + l:half + l + 1, :]).astype(BF16), w1_ref[0, half + l])
    hi_scr[0:N_CMP_PAD, :] = acc_hi
    hi_scr[N_CMP_PAD:N_CMP_PAD + 8, :] = jnp.zeros((8, CMP_HIDDEN), F32)
    pre = acc_lo + hi_scr[pl.ds(1, N_CMP_PAD), :]
    hdn = pre * jax.nn.sigmoid(pre)
    out = _dot(hdn.astype(BF16), w2_ref[0])
    out = _rope(out, cos_ref[0], sin_ref[0])
    row = lax.broadcasted_iota(jnp.int32, (N_CMP_PAD, HEAD_DIM), 0)
    out = jnp.where(row < N_CMP, out, 0.0)
    o_ref[0, 0] = out.astype(BF16)
    ot_ref[0, 0] = out.T.astype(BF16)


def _compress(proj, pe, w1, w2, cos_c, sin_c):
    return pl.pallas_call(
        _cmp_kernel,
        grid=(2, N_KV_HEADS),
        in_specs=[
            pl.BlockSpec((SEQ, HEAD_DIM), lambda k, h: (0, COL_KC // HEAD_DIM + N_KV_HEADS * k + h)),
            pl.BlockSpec((1, CMP_BLOCK, HEAD_DIM), lambda k, h: (k, 0, 0)),
            pl.BlockSpec((1, CMP_BLOCK, HEAD_DIM, CMP_HIDDEN), lambda k, h: (k, 0, 0, 0)),
            pl.BlockSpec((1, CMP_HIDDEN, HEAD_DIM), lambda k, h: (k, 0, 0)),
            pl.BlockSpec((1, N_CMP_PAD, HEAD_DIM), lambda k, h: (k, 0, 0)),
            pl.BlockSpec((1, N_CMP_PAD, HEAD_DIM), lambda k, h: (k, 0, 0)),
        ],
        out_specs=[
            pl.BlockSpec((1, 1, N_CMP_PAD, HEAD_DIM), lambda k, h: (k, h, 0, 0)),
            pl.BlockSpec((1, 1, HEAD_DIM, N_CMP_PAD), lambda k, h: (k, h, 0, 0)),
        ],
        out_shape=[
            jax.ShapeDtypeStruct((2, N_KV_HEADS, N_CMP_PAD, HEAD_DIM), BF16),
            jax.ShapeDtypeStruct((2, N_KV_HEADS, HEAD_DIM, N_CMP_PAD), BF16),
        ],
        scratch_shapes=[pltpu.VMEM((N_CMP_PAD + 8, CMP_HIDDEN), F32)],
        compiler_params=pltpu.CompilerParams(
            dimension_semantics=("arbitrary", "arbitrary"), vmem_limit_bytes=VMEM_LIMIT),
        name="compress",
    )(proj, pe, w1, w2, cos_c, sin_c)


IMP_PAD = 8
NH = 2
N_STAT = 3


def _attn_kernel(q_ref, kc_ref, vct_ref, ks_ref, vst_ref, kw_ref, vwt_ref, g_ref, gn_ref,
                 o_ref, imp_scr, sel_scr, oc_scr, ow_scr, s_scr, p_scr, acc_scr, stat_scr, g_scr):
    i = pl.program_id(1)
    s0 = i * AQ
    nrow = AR
    heads = range(NH)

    q_t = [jnp.concatenate([q_ref[hh * GROUP + g] for g in range(GROUP)], axis=1) for hh in heads]

    def mask_groups(s, mask):
        return jnp.concatenate(
            [jnp.where(mask, s[:, g * AQ:(g + 1) * AQ], NEG) for g in range(GROUP)], axis=1)

    def compressed(n_chunks):
        rows = CMP_CHUNK * n_chunks
        mrows = min(rows, CMP_MASK_ROWS)
        c0 = rows - mrows
        r_i = lax.broadcasted_iota(jnp.int32, (mrows, AQ), 0)
        t_i = lax.broadcasted_iota(jnp.int32, (mrows, AQ), 1)
        visible = CMP_STRIDE * r_i - t_i <= s0 - (CMP_BLOCK - 1) - CMP_STRIDE * c0
        rest = IMP_PAD + N_CMP_PAD + 8 - (IMP_PAD + rows)
        for hh in heads:
            sc = _dot(kc_ref[0, hh, 0:rows, :], q_t[hh])
            tail = mask_groups(sc[c0:rows], visible)
            sc = tail if c0 == 0 else jnp.concatenate([sc[0:c0], tail], axis=0)
            m_c = jnp.max(sc, axis=0, keepdims=True)
            m_c = jnp.where(m_c > 0.5 * NEG, m_c, 0.0)
            e_c = jnp.exp2(sc - m_c)
            l_c = jnp.sum(e_c, axis=0, keepdims=True)
            p_c = e_c * (1.0 / jnp.where(l_c > 0.0, l_c, 1.0))
            oc_scr[hh] = _dot(vct_ref[0, hh, :, 0:rows], p_c.astype(BF16))
            imp_c = (p_c[:, 0:AQ] + p_c[:, AQ:2 * AQ]) + (p_c[:, 2 * AQ:3 * AQ] + p_c[:, 3 * AQ:4 * AQ])
            for a in range(AQ // LANES):
                imp_scr[hh, a, IMP_PAD:IMP_PAD + rows, :] = imp_c[:, a * LANES:(a + 1) * LANES]
                imp_scr[hh, a, IMP_PAD + rows:IMP_PAD + rows + rest, :] = jnp.zeros((rest, LANES), F32)

    for hh in heads:
        for a in range(AQ // LANES):
            imp_scr[hh, a, 0:IMP_PAD, :] = jnp.zeros((IMP_PAD, LANES), F32)
    n_cmp_chunks = i // (CMP_CHUNK * CMP_STRIDE // AQ) + 1
    for n in range(1, N_CMP_PAD // CMP_CHUNK + 1):
        pl.when(n_cmp_chunks == n)(functools.partial(compressed, n))

    ratio = SEL_BLOCK // CMP_STRIDE

    def importance(hh):
        tap = lambda k: jnp.concatenate(
            [imp_scr[hh, a, pl.ds(IMP_PAD + k, N_BLK, stride=ratio), :] for a in range(AQ // LANES)], axis=1)
        return tap(-1) + 2.0 * (tap(0) + tap(1) + tap(2)) + tap(3)

    jidx = lax.broadcasted_iota(jnp.int32, (N_BLK, AQ), 0).astype(F32)
    tok_l = lax.broadcasted_iota(jnp.int32, (1, AQ), 1)
    jt = (i * (AQ // SEL_BLOCK) + tok_l // SEL_BLOCK).astype(F32)
    miscount = []
    for hh in heads:
        v = jnp.where(jidx == 0.0, REMOVED, importance(hh))
        v = jnp.where(jidx == jt, REMOVED, v)
        v = jnp.where(jidx == jt - 1.0, REMOVED, v)
        v = jnp.where(jidx > jt, NEG, v)
        for _ in range(N_SEL - 3):
            v = jnp.where(v == jnp.max(v, axis=0, keepdims=True), REMOVED, v)
        sel_fast = jnp.where(jidx > jt, 0.0, jnp.where(v == REMOVED, 1.0, 0.0))
        n_picked = jnp.sum(sel_fast, axis=0, keepdims=True)
        miscount.append(jnp.max(jnp.abs(n_picked - jnp.minimum(jt + 1.0, float(N_SEL)))))
        sel_scr[hh] = sel_fast

    flag_pen = jnp.where(lax.broadcasted_iota(jnp.int32, (HEAD_DIM, nrow), 0) == 0, NEG, 0.0).astype(BF16)
    wk0 = pl.multiple_of(s0, AQ)
    r_i = lax.broadcasted_iota(jnp.int32, (AQ, AQ), 0)
    t_i = lax.broadcasted_iota(jnp.int32, (AQ, AQ), 1)
    tri_old = jnp.concatenate([jnp.where(r_i > t_i, 0.0, NEG).astype(BF16)] * GROUP, axis=1)
    tri_causal = jnp.concatenate([jnp.where(r_i <= t_i, 0.0, NEG).astype(BF16)] * GROUP, axis=1)
    for hh in heads:
        q_win = jnp.concatenate([q_t[hh], flag_pen], axis=0)
        sw = _dot(kw_ref[hh, pl.ds(wk0, WIN_KEYS), :], q_win).astype(BF16)
        sw = jnp.concatenate([sw[0:AQ] + tri_old, sw[AQ:WINDOW], sw[WINDOW:WIN_KEYS] + tri_causal], axis=0)
        m_w = jnp.max(sw, axis=0, keepdims=True)
        p_w = jnp.exp2(sw - m_w)
        ow_scr[hh] = _dot(vwt_ref[hh, :, pl.ds(wk0, WIN_KEYS)], p_w)
        s_scr[hh] = _dot(ks_ref[hh, 0:SEL_TK, 0:HEAD_DIM], q_t[hh]).astype(BF16)

    @pl.when(functools.reduce(jnp.maximum, miscount) > 0.5)
    def _():
        for hh in heads:
            w = jnp.where(jidx == 0.0, FORCE, importance(hh))
            w = jnp.where(jidx == jt, FORCE, w)
            w = jnp.where(jidx == jt - 1.0, FORCE, w)
            w = jnp.where(jidx > jt, NEG, w)
            sel = jnp.zeros((N_BLK, AQ), F32)
            for _ in range(N_SEL):
                mx = jnp.max(w, axis=0, keepdims=True)
                cand = jnp.where(w == mx, jidx, float(N_BLK))
                pick = jidx == jnp.min(cand, axis=0, keepdims=True)
                sel = jnp.where(pick, 1.0, sel)
                w = jnp.where(pick, REMOVED, w)
            sel_scr[hh] = jnp.where(jidx > jt, 0.0, sel)

    q_aug = []
    for hh in heads:
        pen = jnp.where(sel_scr[hh] > 0.5, 0.0, NEG).astype(BF16)
        q_aug.append(jnp.concatenate([q_t[hh], jnp.concatenate([pen] * GROUP, axis=1)], axis=0))

    def key_start(kt):
        return pl.multiple_of(kt * SEL_TK, SEL_TK)

    def col_max(s):
        return jnp.max(s, axis=0, keepdims=True).astype(F32)

    def stat(hh, k):
        return stat_scr[hh * N_STAT + k:hh * N_STAT + k + 1, :]

    def set_stat(hh, k, value):
        stat_scr[hh * N_STAT + k:hh * N_STAT + k + 1, :] = value

    def add_pv(hh, kt, p, alpha):
        pv = _dot(vst_ref[hh, :, pl.ds(key_start(kt), SEL_TK)], p)
        acc_scr[hh] = alpha * acc_scr[hh] + pv

    def step(kt):
        for hh in heads:
            m, m_tile, alpha_prev = stat(hh, 0), stat(hh, 1), stat(hh, 2)
            p_prev = p_scr[hh]
            s_cur = s_scr[hh]
            s_nxt = _dot(ks_ref[hh, pl.ds(key_start(kt + 1), SEL_TK), :], q_aug[hh]).astype(BF16)
            add_pv(hh, jnp.maximum(kt - 1, 0), p_prev, alpha_prev)
            m_new = jnp.maximum(m, m_tile)
            p_scr[hh] = jnp.exp2(s_cur - m_new.astype(BF16))
            s_scr[hh] = s_nxt
            set_stat(hh, 0, m_new)
            set_stat(hh, 1, col_max(s_nxt))
            set_stat(hh, 2, jnp.exp2(m - m_new))

    n_full = (i * AQ) // SEL_TK
    for hh in heads:
        pen0 = jnp.where(sel_scr[hh, 0:SEL_TK // SEL_BLOCK, :] > 0.5, 0.0, NEG)
        pen0 = jnp.concatenate([jnp.broadcast_to(pen0[b:b + 1, :], (SEL_BLOCK, AQ))
                                for b in range(SEL_TK // SEL_BLOCK)], axis=0).astype(BF16)
        s_first = s_scr[hh] + jnp.concatenate([pen0] * GROUP, axis=1)
        s_scr[hh] = s_first
        p_scr[hh] = jnp.zeros((SEL_TK, nrow), BF16)
        acc_scr[hh] = jnp.zeros((V_ROWS, nrow), F32)
        set_stat(hh, 0, jnp.full((1, nrow), NEG, F32))
        set_stat(hh, 1, col_max(s_first))
        set_stat(hh, 2, jnp.ones((1, nrow), F32))

    @pl.loop(0, n_full // 2)
    def _(j):
        step(2 * j)
        step(2 * j + 1)

    @pl.when(n_full % 2 == 1)
    def _():
        step(n_full - 1)

    off = pl.multiple_of(s0 - n_full * SEL_TK, AQ)
    for hh in heads:
        add_pv(hh, jnp.maximum(n_full - 1, 0), p_scr[hh], stat(hh, 2))
        s_scr[hh, pl.ds(off, AQ), :] = s_scr[hh, pl.ds(off, AQ), :] + tri_causal
        s_diag = s_scr[hh]
        m_s = stat(hh, 0)
        m_fin = jnp.maximum(m_s, col_max(s_diag))
        add_pv(hh, n_full, jnp.exp2(s_diag - m_fin.astype(BF16)), jnp.exp2(m_s - m_fin))

    g_scr[...] = g_ref[...].T
    for hh in heads:
        row0 = ((pl.program_id(0) * NH + hh) * GROUP) * 3
        grow = lambda c: jnp.concatenate(
            [g_scr[pl.ds(row0 + g * 3 + c, 1), :] for g in range(GROUP)], axis=1)
        l_s = acc_scr[hh, HEAD_DIM:HEAD_DIM + 1, :]
        l_w = ow_scr[hh, HEAD_DIM:HEAD_DIM + 1, :]
        out_t = (oc_scr[hh] * grow(0) + acc_scr[hh, 0:HEAD_DIM, :] * (grow(1) / l_s)
                 + ow_scr[hh, 0:HEAD_DIM, :] * (grow(2) / l_w))
        out = jnp.concatenate([out_t[:, g * AQ:(g + 1) * AQ].T for g in range(GROUP)], axis=1)
        cols = slice(hh * GROUP * HEAD_DIM, (hh + 1) * GROUP * HEAD_DIM)
        gn = gn_ref[:, cols]
        o_ref[:, cols] = (out * (gn * jax.nn.sigmoid(gn))).astype(BF16)


def _attention(q_r, kc, vct, ks_aug, vst, kw_aug, vwt, gates, proj):
    head_res = lambda shape: pl.BlockSpec((NH,) + shape, lambda hp, i: (hp, 0, 0), pipeline_mode=pl.Buffered(1))
    width = NH * GROUP * HEAD_DIM
    return pl.pallas_call(
        _attn_kernel,
        grid=(N_KV_HEADS // NH, SEQ // AQ),
        in_specs=[
            pl.BlockSpec((NH * GROUP, HEAD_DIM, AQ), lambda hp, i: (hp, 0, i)),
            pl.BlockSpec((1, NH, N_CMP_PAD, HEAD_DIM), lambda hp, i: (0, hp, 0, 0)),
            pl.BlockSpec((1, NH, HEAD_DIM, N_CMP_PAD), lambda hp, i: (1, hp, 0, 0)),
            head_res((SEQ, 2 * HEAD_DIM)),
            head_res((V_ROWS, SEQ)),
            head_res((WINDOW + SEQ, 2 * HEAD_DIM)),
            head_res((V_ROWS, WINDOW + SEQ)),
            pl.BlockSpec((AQ, LANES), lambda hp, i: (i, 0)),
            pl.BlockSpec((AQ, width), lambda hp, i: (i, COL_GN // width + hp)),
        ],
        out_specs=pl.BlockSpec((AQ, width), lambda hp, i: (i, hp)),
        out_shape=jax.ShapeDtypeStruct((SEQ, NSA_WIDTH), BF16),
        scratch_shapes=[pltpu.VMEM((NH, AQ // LANES, IMP_PAD + N_CMP_PAD + 8, LANES), F32),
                        pltpu.VMEM((NH, N_BLK, AQ), F32),
                        pltpu.VMEM((NH, HEAD_DIM, AR), F32),
                        pltpu.VMEM((NH, V_ROWS, AR), F32),
                        pltpu.VMEM((NH, SEL_TK, AR), BF16),
                        pltpu.VMEM((NH, SEL_TK, AR), BF16),
                        pltpu.VMEM((NH, V_ROWS, AR), F32),
                        pltpu.VMEM((NH * N_STAT, AR), F32),
                        pltpu.VMEM((LANES, AQ), F32)],
        compiler_params=pltpu.CompilerParams(
            dimension_semantics=("arbitrary", "arbitrary"), vmem_limit_bytes=VMEM_LIMIT),
        name="attn",
    )(q_r, kc, vct, ks_aug, vst, kw_aug, vwt, gates, proj)


def _pool_kernel(u_ref, up_ref, gp_ref, mix_ref, sc_ref, wpo_ref, ya_ref, ext_scr):
    i = pl.program_id(0)
    tm = POOL_TM
    ext_scr[0:POOL_HALO, :] = jnp.where(i > 0, up_ref[...], 0.0)
    ext_scr[POOL_HALO:POOL_HALO + tm, :] = u_ref[...]
    t1 = i * tm + lax.broadcasted_iota(jnp.int32, (tm, 1), 0) + 1
    zs = []
    for g, w in enumerate(POOL_WINDOWS):
        cs = slice(g * POOL_GROUP, (g + 1) * POOL_GROUP)
        u = ext_scr[POOL_HALO:POOL_HALO + tm, cs]
        acc = u
        for k in range(1, w):
            acc = acc + ext_scr[POOL_HALO - k:POOL_HALO - k + tm, cs]
        cnt = jnp.minimum(t1, w).astype(F32)
        pooled = acc / cnt - u
        zs.append(_dot(pooled.astype(BF16), mix_ref[g]))
    z = jnp.concatenate(zs, axis=1) * sc_ref[...]
    gp = gp_ref[...]
    y = z * (gp * jax.nn.sigmoid(gp))
    ya_ref[...] = _dot(y.astype(BF16), wpo_ref[...])


def _pool(proj, mix, scale, w_pool_out):
    tm = POOL_TM
    return pl.pallas_call(
        _pool_kernel,
        grid=(SEQ // tm,),
        in_specs=[
            pl.BlockSpec((tm, POOL_WIDTH), lambda i: (i, COL_U // POOL_WIDTH)),
            pl.BlockSpec((POOL_HALO, POOL_WIDTH),
                         lambda i: (jnp.maximum(i * (tm // POOL_HALO) - 1, 0), COL_U // POOL_WIDTH)),
            pl.BlockSpec((tm, POOL_WIDTH), lambda i: (i, COL_GP // POOL_WIDTH)),
            pl.BlockSpec((len(POOL_WINDOWS), POOL_GROUP, POOL_GROUP), lambda i: (0, 0, 0)),
            pl.BlockSpec((1, POOL_WIDTH), lambda i: (0, 0)),
            pl.BlockSpec((POOL_WIDTH, D_MODEL), lambda i: (0, 0)),
        ],
        out_specs=pl.BlockSpec((tm, D_MODEL), lambda i: (i, 0)),
        out_shape=jax.ShapeDtypeStruct((SEQ, D_MODEL), F32),
        scratch_shapes=[pltpu.VMEM((POOL_HALO + tm, POOL_WIDTH), F32)],
        compiler_params=pltpu.CompilerParams(
            dimension_semantics=("arbitrary",), vmem_limit_bytes=VMEM_LIMIT),
        name="pool",
    )(proj, proj, proj, mix, scale, w_pool_out)


def _out_kernel(x_ref, ya_ref, o_ref, gm0_ref, gm1_ref, wn_ref, wo_ref, fw_ref, out_ref):
    y_b = _dot(o_ref[...], wn_ref[...])
    merged = gm0_ref[...] * ya_ref[...] + gm1_ref[...] * y_b
    r = x_ref[...] + _dot(merged.astype(BF16), wo_ref[...])
    ms = jnp.mean(r * r, axis=-1, keepdims=True)
    out_ref[...] = r * lax.rsqrt(ms + EPS) * fw_ref[...]


def _output(x2, y_a, o_gated, proj, w_nsa_out, w_out, final_norm_w):
    tm = OUT_TM
    row = lambda: pl.BlockSpec((tm, D_MODEL), lambda i: (i, 0))
    resident = lambda: pl.BlockSpec((D_MODEL, D_MODEL), lambda i: (0, 0), pipeline_mode=pl.Buffered(1))
    return pl.pallas_call(
        _out_kernel,
        grid=(SEQ // tm,),
        in_specs=[
            row(), row(), row(),
            pl.BlockSpec((tm, D_MODEL), lambda i: (i, 0)),
            pl.BlockSpec((tm, D_MODEL), lambda i: (i, 1)),
            resident(), resident(),
            pl.BlockSpec((1, D_MODEL), lambda i: (0, 0)),
        ],
        out_specs=row(),
        out_shape=jax.ShapeDtypeStruct((SEQ, D_MODEL), F32),
        compiler_params=pltpu.CompilerParams(
            dimension_semantics=("arbitrary",), vmem_limit_bytes=VMEM_LIMIT),
        name="out",
    )(x2, y_a, o_gated, proj, proj, w_nsa_out, w_out, final_norm_w)


def _rope_tables():
    half = HEAD_DIM // 2
    inv = ROPE_THETA ** (-np.arange(half, dtype=np.float64) / half)
    ang = np.arange(SEQ, dtype=np.float64)[:, None] * inv[None, :]
    cos = np.concatenate([np.cos(ang), np.cos(ang)], axis=-1).astype(np.float32)
    sin = np.concatenate([-np.sin(ang), np.sin(ang)], axis=-1).astype(np.float32)
    pad = lambda t: np.pad(t[CMP_BLOCK - 1::CMP_STRIDE], ((0, N_CMP_PAD - N_CMP), (0, 0)))
    cos_c = np.stack([pad(cos), np.ones((N_CMP_PAD, HEAD_DIM), np.float32)])
    sin_c = np.stack([pad(sin), np.zeros((N_CMP_PAD, HEAD_DIM), np.float32)])
    return cos, sin, cos_c, sin_c


def kernel(x, norm_w, w_in, pool_mix, pool_scale, cmp_pe_k, cmp_w1_k, cmp_w2_k, cmp_pe_v, cmp_w1_v,
           cmp_w2_v, w_pool_out, w_nsa_out, w_merge, b_merge, w_out, final_norm_w):
    assert x.shape == (1, SEQ, D_MODEL) and norm_w.shape[0] == 1
    x2 = x.reshape(SEQ, D_MODEL)
    proj, gates = _proj(x2, norm_w, w_merge[0].astype(BF16), jnp.transpose(w_in[0]), b_merge)

    cos_tab, sin_tab, cos_c, sin_c = _rope_tables()
    q_r, ks_aug, vst, kw_aug, vwt = _prep(proj, cos_tab, sin_tab)

    pe = jnp.stack([cmp_pe_k[0], cmp_pe_v[0]])
    w1 = jnp.stack([cmp_w1_k[0], cmp_w1_v[0]]).astype(BF16)
    w2 = jnp.stack([cmp_w2_k[0], cmp_w2_v[0]]).astype(BF16)
    cmp_n, cmp_t = _compress(proj, pe, w1, w2, cos_c, sin_c)

    o_gated = _attention(q_r, cmp_n, cmp_t, ks_aug, vst, kw_aug, vwt, gates, proj)

    y_a = _pool(proj, pool_mix[0].astype(BF16), pool_scale, w_pool_out[0].astype(BF16))
    out = _output(x2, y_a, o_gated, proj, w_nsa_out[0].astype(BF16), w_out[0].astype(BF16),
                  final_norm_w[None, :])
    return out.reshape(1, SEQ, D_MODEL)
```

```python
import functools

import numpy as np
import jax
import jax.numpy as jnp
from jax import lax
from jax.experimental import pallas as pl
from jax.experimental.pallas import tpu as pltpu

F32 = jnp.float32
BF16 = jnp.bfloat16

LANES = 128

D_MODEL = 2048
SEQ = 8192
POOL_WIDTH = 1024
POOL_WINDOWS = (2, 4, 8, 16)
POOL_GROUP = 256
POOL_HALO = 16
N_Q_HEADS = 16
N_KV_HEADS = 4
GROUP = 4
HEAD_DIM = 128
NSA_WIDTH = 2048
KV_WIDTH = 512
CMP_BLOCK = 32
CMP_STRIDE = 16
CMP_HIDDEN = 256
N_CMP = (SEQ - CMP_BLOCK) // CMP_STRIDE + 1
N_CMP_PAD = 512
SEL_BLOCK = 64
N_BLK = SEQ // SEL_BLOCK
N_SEL = 16
WINDOW = 512
ROPE_THETA = 10000.0
EPS = 1e-6
NEG = -1e30
FORCE = 1e6
REMOVED = -3e38
LOG2E = 1.4426950408889634
Q_SCALE = HEAD_DIM ** -0.5 * LOG2E

COL_GM = 0
COL_U = 4096
COL_GP = 5120
COL_Q = 6144
COL_KC = 8192
COL_VC = 8704
COL_KS = 9216
COL_VS = 9728
COL_KW = 10240
COL_VW = 10752
COL_GN = 11264
N_PROJ = 13312
W_IN_MAIN = 9216
N_GATE = 3 * N_Q_HEADS

VMEM_LIMIT = 56 * 1024 * 1024
PROJ_VMEM_LIMIT = 60000 * 1024

PROJ_TM = 1024
PROJ_TN = 1024
N_GM_TILES = 4096 // PROJ_TN
PREP_TM = 512
SEL_TK = 512
AQ = 256
AR = GROUP * AQ
WIN_KEYS = WINDOW + AQ
CMP_CHUNK = 128
CMP_MASK_ROWS = 256
V_ROWS = HEAD_DIM + 16
POOL_TM = 512
OUT_TM = 256


def _dot(a, b):
    return jnp.dot(a, b, preferred_element_type=F32)


def _nt_dot(a, b):
    return lax.dot_general(a, b, (((1,), (1,)), ((), ())), preferred_element_type=F32)


def _proj_kernel(x_ref, nw_ref, wm_ref, wi_ref, wg_ref, b_ref, o_ref, g_ref, h_scr):
    j = pl.program_id(1)

    @pl.when(j == 0)
    def _():
        x = x_ref[...]
        ms = jnp.mean(x * x, axis=-1, keepdims=True)
        hb = (x * lax.rsqrt(ms + EPS) * nw_ref[...]).astype(BF16)
        h_scr[...] = hb
        row = lax.broadcasted_iota(jnp.int32, (LANES, D_MODEL), 0)
        wg = jnp.where(row < N_GATE, wg_ref[...], 0.0).astype(BF16)
        g_ref[...] = jax.nn.sigmoid(_nt_dot(hb, wg))

    @pl.when(j < N_GM_TILES)
    def _():
        o_ref[...] = jax.nn.sigmoid(_dot(h_scr[...], wm_ref[...]) + b_ref[...])

    @pl.when(j >= N_GM_TILES)
    def _():
        o_ref[...] = _nt_dot(h_scr[...], wi_ref[...].astype(BF16))


def _proj(x2, norm_w, w_merge, w_in, b_merge):
    grid = (SEQ // PROJ_TM, N_PROJ // PROJ_TN)
    return pl.pallas_call(
        _proj_kernel,
        grid=grid,
        in_specs=[
            pl.BlockSpec((PROJ_TM, D_MODEL), lambda i, j: (i, 0)),
            pl.BlockSpec((1, D_MODEL), lambda i, j: (0, 0)),
            pl.BlockSpec((D_MODEL, PROJ_TN), lambda i, j: (0, jnp.minimum(j, N_GM_TILES - 1))),
            pl.BlockSpec((PROJ_TN, D_MODEL), lambda i, j: (jnp.maximum(j - N_GM_TILES, 0), 0)),
            pl.BlockSpec((LANES, D_MODEL), lambda i, j: (W_IN_MAIN // LANES, 0)),
            pl.BlockSpec((1, PROJ_TN), lambda i, j: (0, jnp.minimum(j, N_GM_TILES - 1))),
        ],
        out_specs=[
            pl.BlockSpec((PROJ_TM, PROJ_TN), lambda i, j: (i, j)),
            pl.BlockSpec((PROJ_TM, LANES), lambda i, j: (i, 0)),
        ],
        out_shape=[
            jax.ShapeDtypeStruct((SEQ, N_PROJ), F32),
            jax.ShapeDtypeStruct((SEQ, LANES), F32),
        ],
        scratch_shapes=[pltpu.VMEM((PROJ_TM, D_MODEL), BF16)],
        compiler_params=pltpu.CompilerParams(
            dimension_semantics=("arbitrary", "arbitrary"), vmem_limit_bytes=PROJ_VMEM_LIMIT),
        name="proj",
    )(x2, norm_w, w_merge, w_in, w_in, b_merge)


def _rope(x, cos, sin_signed):
    return x * cos + pltpu.roll(x, HEAD_DIM // 2, 1) * sin_signed


def _prep_kernel(q_ref, ks_ref, vs_ref, kw_ref, vw_ref, cos_ref, sin_ref,
                 qo_ref, kso_ref, vso_ref, kwo_ref, vwo_ref):
    i = pl.program_id(0)
    t0 = jnp.maximum(i - 1, 0) * PREP_TM
    cos = cos_ref[...]
    sin = sin_ref[...]
    for hq in range(N_Q_HEADS):
        sl = slice(hq * HEAD_DIM, (hq + 1) * HEAD_DIM)
        qo_ref[hq] = (_rope(q_ref[:, sl], cos, sin) * Q_SCALE).T.astype(BF16)
    blk = (t0 + lax.broadcasted_iota(jnp.int32, (PREP_TM, N_BLK), 0)) // SEL_BLOCK
    lane = lax.broadcasted_iota(jnp.int32, (PREP_TM, N_BLK), 1)
    onehot = jnp.where(blk == lane, 1.0, 0.0).astype(BF16)
    pad_flag = jnp.where(lane == 0, 1.0, 0.0)
    is_pad = i == 0
    ones_rows = jnp.where(lax.broadcasted_iota(jnp.int32, (V_ROWS - HEAD_DIM, PREP_TM), 0) == 0,
                          1.0, 0.0).astype(BF16)
    for h in range(N_KV_HEADS):
        sl = slice(h * HEAD_DIM, (h + 1) * HEAD_DIM)
        kso_ref[h, :, 0:HEAD_DIM] = _rope(ks_ref[:, sl], cos, sin).astype(BF16)
        kso_ref[h, :, HEAD_DIM:2 * HEAD_DIM] = onehot
        vso_ref[h, 0:HEAD_DIM, :] = vs_ref[:, sl].T.astype(BF16)
        vso_ref[h, HEAD_DIM:V_ROWS, :] = ones_rows
        kwo_ref[h, :, 0:HEAD_DIM] = jnp.where(is_pad, 0.0, _rope(kw_ref[:, sl], cos, sin)).astype(BF16)
        kwo_ref[h, :, HEAD_DIM:2 * HEAD_DIM] = jnp.where(is_pad, pad_flag, 0.0).astype(BF16)
        vwo_ref[h, 0:HEAD_DIM, :] = jnp.where(is_pad, 0.0, vw_ref[:, sl].T).astype(BF16)
        vwo_ref[h, HEAD_DIM:V_ROWS, :] = ones_rows


def _prep(proj, cos_tab, sin_tab):
    tm = PREP_TM
    assert tm == WINDOW
    tok = lambda i: jnp.maximum(i - 1, 0)
    kv_spec = lambda col: pl.BlockSpec((tm, KV_WIDTH), lambda i: (tok(i), col // KV_WIDTH))
    return pl.pallas_call(
        _prep_kernel,
        grid=(SEQ // tm + 1,),
        in_specs=[
            pl.BlockSpec((tm, NSA_WIDTH), lambda i: (tok(i), COL_Q // NSA_WIDTH)),
            kv_spec(COL_KS), kv_spec(COL_VS), kv_spec(COL_KW), kv_spec(COL_VW),
            pl.BlockSpec((tm, HEAD_DIM), lambda i: (tok(i), 0)),
            pl.BlockSpec((tm, HEAD_DIM), lambda i: (tok(i), 0)),
        ],
        out_specs=[
            pl.BlockSpec((N_Q_HEADS, HEAD_DIM, tm), lambda i: (0, 0, tok(i))),
            pl.BlockSpec((N_KV_HEADS, tm, 2 * HEAD_DIM), lambda i: (0, tok(i), 0)),
            pl.BlockSpec((N_KV_HEADS, V_ROWS, tm), lambda i: (0, 0, tok(i))),
            pl.BlockSpec((N_KV_HEADS, tm, 2 * HEAD_DIM), lambda i: (0, i, 0)),
            pl.BlockSpec((N_KV_HEADS, V_ROWS, tm), lambda i: (0, 0, i)),
        ],
        out_shape=[
            jax.ShapeDtypeStruct((N_Q_HEADS, HEAD_DIM, SEQ), BF16),
            jax.ShapeDtypeStruct((N_KV_HEADS, SEQ, 2 * HEAD_DIM), BF16),
            jax.ShapeDtypeStruct((N_KV_HEADS, V_ROWS, SEQ), BF16),
            jax.ShapeDtypeStruct((N_KV_HEADS, WINDOW + SEQ, 2 * HEAD_DIM), BF16),
            jax.ShapeDtypeStruct((N_KV_HEADS, V_ROWS, WINDOW + SEQ), BF16),
        ],
        compiler_params=pltpu.CompilerParams(
            dimension_semantics=("arbitrary",), vmem_limit_bytes=VMEM_LIMIT),
        name="prep",
    )(proj, proj, proj, proj, proj, cos_tab, sin_tab)


def _cmp_kernel(x_ref, pe_ref, w1_ref, w2_ref, cos_ref, sin_ref, o_ref, ot_ref, hi_scr):
    half = CMP_BLOCK // 2
    acc_lo = jnp.zeros((N_CMP_PAD, CMP_HIDDEN), F32)
    acc_hi = jnp.zeros((N_CMP_PAD, CMP_HIDDEN), F32)
    for l in range(half):
        xl = x_ref[pl.ds(l, N_CMP_PAD, stride=CMP_STRIDE), :]
        acc_lo += _dot((xl + pe_ref[0, l:l + 1, :]).astype(BF16), w1_ref[0, l])
        acc_hi += _dot((xl + pe_ref[0, half + l:half + l + 1, :]).astype(BF16), w1_ref[0, half + l])
    hi_scr[0:N_CMP_PAD, :] = acc_hi
    hi_scr[N_CMP_PAD:N_CMP_PAD + 8, :] = jnp.zeros((8, CMP_HIDDEN), F32)
    pre = acc_lo + hi_scr[pl.ds(1, N_CMP_PAD), :]
    hdn = pre * jax.nn.sigmoid(pre)
    out = _dot(hdn.astype(BF16), w2_ref[0])
    out = _rope(out, cos_ref[0], sin_ref[0])
    row = lax.broadcasted_iota(jnp.int32, (N_CMP_PAD, HEAD_DIM), 0)
    out = jnp.where(row < N_CMP, out, 0.0)
    o_ref[0, 0] = out.astype(BF16)
    ot_ref[0, 0] = out.T.astype(BF16)


def _compress(proj, pe, w1, w2, cos_c, sin_c):
    return pl.pallas_call(
        _cmp_kernel,
        grid=(2, N_KV_HEADS),
        in_specs=[
            pl.BlockSpec((SEQ, HEAD_DIM), lambda k, h: (0, COL_KC // HEAD_DIM + N_KV_HEADS * k + h)),
            pl.BlockSpec((1, CMP_BLOCK, HEAD_DIM), lambda k, h: (k, 0, 0)),
            pl.BlockSpec((1, CMP_BLOCK, HEAD_DIM, CMP_HIDDEN), lambda k, h: (k, 0, 0, 0)),
            pl.BlockSpec((1, CMP_HIDDEN, HEAD_DIM), lambda k, h: (k, 0, 0)),
            pl.BlockSpec((1, N_CMP_PAD, HEAD_DIM), lambda k, h: (k, 0, 0)),
            pl.BlockSpec((1, N_CMP_PAD, HEAD_DIM), lambda k, h: (k, 0, 0)),
        ],
        out_specs=[
            pl.BlockSpec((1, 1, N_CMP_PAD, HEAD_DIM), lambda k, h: (k, h, 0, 0)),
            pl.BlockSpec((1, 1, HEAD_DIM, N_CMP_PAD), lambda k, h: (k, h, 0, 0)),
        ],
        out_shape=[
            jax.ShapeDtypeStruct((2, N_KV_HEADS, N_CMP_PAD, HEAD_DIM), BF16),
            jax.ShapeDtypeStruct((2, N_KV_HEADS, HEAD_DIM, N_CMP_PAD), BF16),
        ],
        scratch_shapes=[pltpu.VMEM((N_CMP_PAD + 8, CMP_HIDDEN), F32)],
        compiler_params=pltpu.CompilerParams(
            dimension_semantics=("arbitrary", "arbitrary"), vmem_limit_bytes=VMEM_LIMIT),
        name="compress",
    )(proj, pe, w1, w2, cos_c, sin_c)


IMP_PAD = 8
NH = 2
N_STAT = 3


def _attn_kernel(q_ref, kc_ref, vct_ref, ks_ref, vst_ref, kw_ref, vwt_ref, g_ref, gn_ref,
                 o_ref, imp_scr, sel_scr, oc_scr, ow_scr, s_scr, p_scr, acc_scr, stat_scr, g_scr):
    i = pl.program_id(1)
    s0 = i * AQ
    nrow = AR
    heads = range(NH)

    q_t = [jnp.concatenate([q_ref[hh * GROUP + g] for g in range(GROUP)], axis=1) for hh in heads]

    def mask_groups(s, mask):
        return jnp.concatenate(
            [jnp.where(mask, s[:, g * AQ:(g + 1) * AQ], NEG) for g in range(GROUP)], axis=1)

    def compressed(n_chunks):
        rows = CMP_CHUNK * n_chunks
        mrows = min(rows, CMP_MASK_ROWS)
        c0 = rows - mrows
        r_i = lax.broadcasted_iota(jnp.int32, (mrows, AQ), 0)
        t_i = lax.broadcasted_iota(jnp.int32, (mrows, AQ), 1)
        visible = CMP_STRIDE * r_i - t_i <= s0 - (CMP_BLOCK - 1) - CMP_STRIDE * c0
        rest = IMP_PAD + N_CMP_PAD + 8 - (IMP_PAD + rows)
        for hh in heads:
            sc = _dot(kc_ref[0, hh, 0:rows, :], q_t[hh])
            tail = mask_groups(sc[c0:rows], visible)
            sc = tail if c0 == 0 else jnp.concatenate([sc[0:c0], tail], axis=0)
            m_c = jnp.max(sc, axis=0, keepdims=True)
            m_c = jnp.where(m_c > 0.5 * NEG, m_c, 0.0)
            e_c = jnp.exp2(sc - m_c)
            l_c = jnp.sum(e_c, axis=0, keepdims=True)
            p_c = e_c * (1.0 / jnp.where(l_c > 0.0, l_c, 1.0))
            oc_scr[hh] = _dot(vct_ref[0, hh, :, 0:rows], p_c.astype(BF16))
            imp_c = (p_c[:, 0:AQ] + p_c[:, AQ:2 * AQ]) + (p_c[:, 2 * AQ:3 * AQ] + p_c[:, 3 * AQ:4 * AQ])
            for a in range(AQ // LANES):
                imp_scr[hh, a, IMP_PAD:IMP_PAD + rows, :] = imp_c[:, a * LANES:(a + 1) * LANES]
                imp_scr[hh, a, IMP_PAD + rows:IMP_PAD + rows + rest, :] = jnp.zeros((rest, LANES), F32)

    for hh in heads:
        for a in range(AQ // LANES):
            imp_scr[hh, a, 0:IMP_PAD, :] = jnp.zeros((IMP_PAD, LANES), F32)
    n_cmp_chunks = i // (CMP_CHUNK * CMP_STRIDE // AQ) + 1
    for n in range(1, N_CMP_PAD // CMP_CHUNK + 1):
        pl.when(n_cmp_chunks == n)(functools.partial(compressed, n))

    ratio = SEL_BLOCK // CMP_STRIDE

    def importance(hh):
        tap = lambda k: jnp.concatenate(
            [imp_scr[hh, a, pl.ds(IMP_PAD + k, N_BLK, stride=ratio), :] for a in range(AQ // LANES)], axis=1)
        return tap(-1) + 2.0 * (tap(0) + tap(1) + tap(2)) + tap(3)

    jidx = lax.broadcasted_iota(jnp.int32, (N_BLK, AQ), 0).astype(F32)
    tok_l = lax.broadcasted_iota(jnp.int32, (1, AQ), 1)
    jt = (i * (AQ // SEL_BLOCK) + tok_l // SEL_BLOCK).astype(F32)
    flag_pen = jnp.where(lax.broadcasted_iota(jnp.int32, (HEAD_DIM, nrow), 0) == 0, NEG, 0.0).astype(BF16)
    wk0 = pl.multiple_of(s0, AQ)
    r_i = lax.broadcasted_iota(jnp.int32, (AQ, AQ), 0)
    t_i = lax.broadcasted_iota(jnp.int32, (AQ, AQ), 1)
    tri_old = jnp.concatenate([jnp.where(r_i > t_i, 0.0, NEG).astype(BF16)] * GROUP, axis=1)
    tri_causal = jnp.concatenate([jnp.where(r_i <= t_i, 0.0, NEG).astype(BF16)] * GROUP, axis=1)

    def window(hh):
        q_win = jnp.concatenate([q_t[hh], flag_pen], axis=0)
        sw = _dot(kw_ref[hh, pl.ds(wk0, WIN_KEYS), :], q_win).astype(BF16)
        sw = jnp.concatenate([sw[0:AQ] + tri_old, sw[AQ:WINDOW], sw[WINDOW:WIN_KEYS] + tri_causal], axis=0)
        m_w = jnp.max(sw, axis=0, keepdims=True)
        p_w = jnp.exp2(sw - m_w)
        ow_scr[hh] = _dot(vwt_ref[hh, :, pl.ds(wk0, WIN_KEYS)], p_w)
        s_scr[hh] = _dot(ks_ref[hh, 0:SEL_TK, 0:HEAD_DIM], q_t[hh]).astype(BF16)

    miscount = []
    vals = []
    for hh in heads:
        v = jnp.where(jidx == 0.0, REMOVED, importance(hh))
        v = jnp.where(jidx == jt, REMOVED, v)
        v = jnp.where(jidx == jt - 1.0, REMOVED, v)
        vals.append(jnp.where(jidx > jt, NEG, v))
    for _ in range(N_SEL - 3):
        for hh in heads:
            vals[hh] = jnp.where(vals[hh] == jnp.max(vals[hh], axis=0, keepdims=True), REMOVED, vals[hh])
    for hh in heads:
        sel_fast = jnp.where(jidx > jt, 0.0, jnp.where(vals[hh] == REMOVED, 1.0, 0.0))
        n_picked = jnp.sum(sel_fast, axis=0, keepdims=True)
        miscount.append(jnp.max(jnp.abs(n_picked - jnp.minimum(jt + 1.0, float(N_SEL)))))
        sel_scr[hh] = sel_fast
    for hh in heads:
        window(hh)

    @pl.when(functools.reduce(jnp.maximum, miscount) > 0.5)
    def _():
        for hh in heads:
            w = jnp.where(jidx == 0.0, FORCE, importance(hh))
            w = jnp.where(jidx == jt, FORCE, w)
            w = jnp.where(jidx == jt - 1.0, FORCE, w)
            w = jnp.where(jidx > jt, NEG, w)
            sel = jnp.zeros((N_BLK, AQ), F32)
            for _ in range(N_SEL):
                mx = jnp.max(w, axis=0, keepdims=True)
                cand = jnp.where(w == mx, jidx, float(N_BLK))
                pick = jidx == jnp.min(cand, axis=0, keepdims=True)
                sel = jnp.where(pick, 1.0, sel)
                w = jnp.where(pick, REMOVED, w)
            sel_scr[hh] = jnp.where(jidx > jt, 0.0, sel)

    q_aug = []
    for hh in heads:
        pen = jnp.where(sel_scr[hh] > 0.5, 0.0, NEG).astype(BF16)
        q_aug.append(jnp.concatenate([q_t[hh], jnp.concatenate([pen] * GROUP, axis=1)], axis=0))

    def key_start(kt):
        return pl.multiple_of(kt * SEL_TK, SEL_TK)

    def col_max(s):
        return jnp.max(s, axis=0, keepdims=True).astype(F32)

    def set_stat(hh, k, value):
        stat_scr[hh * N_STAT + k:hh * N_STAT + k + 1, :] = value

    def step(kt):
        k_prev = key_start(jnp.maximum(kt - 1, 0))
        k_next = key_start(kt + 1)
        for hh in heads:
            for g in range(GROUP):
                cs = slice(g * AQ, (g + 1) * AQ)
                row = lambda k: stat_scr[hh * N_STAT + k:hh * N_STAT + k + 1, cs]
                m, m_tile, alpha_prev = row(0), row(1), row(2)
                p_prev = p_scr[hh, :, cs]
                s_cur = s_scr[hh, :, cs]
                s_nxt = _dot(ks_ref[hh, pl.ds(k_next, SEL_TK), :], q_aug[hh][:, cs]).astype(BF16)
                pv = _dot(vst_ref[hh, :, pl.ds(k_prev, SEL_TK)], p_prev)
                acc_scr[hh, :, cs] = alpha_prev * acc_scr[hh, :, cs] + pv
                m_new = jnp.maximum(m, m_tile)
                p_scr[hh, :, cs] = jnp.exp2(s_cur - m_new.astype(BF16))
                s_scr[hh, :, cs] = s_nxt
                stat_scr[hh * N_STAT + 0:hh * N_STAT + 1, cs] = m_new
                stat_scr[hh * N_STAT + 1:hh * N_STAT + 2, cs] = col_max(s_nxt)
                stat_scr[hh * N_STAT + 2:hh * N_STAT + 3, cs] = jnp.exp2(m - m_new)

    n_full = (i * AQ) // SEL_TK
    for hh in heads:
        pen0 = jnp.where(sel_scr[hh, 0:SEL_TK // SEL_BLOCK, :] > 0.5, 0.0, NEG)
        pen0 = jnp.concatenate([jnp.broadcast_to(pen0[b:b + 1, :], (SEL_BLOCK, AQ))
                                for b in range(SEL_TK // SEL_BLOCK)], axis=0).astype(BF16)
        s_first = s_scr[hh] + jnp.concatenate([pen0] * GROUP, axis=1)
        s_scr[hh] = s_first
        p_scr[hh] = jnp.zeros((SEL_TK, nrow), BF16)
        acc_scr[hh] = jnp.zeros((V_ROWS, nrow), F32)
        set_stat(hh, 0, jnp.full((1, nrow), NEG, F32))
        set_stat(hh, 1, col_max(s_first))
        set_stat(hh, 2, jnp.ones((1, nrow), F32))

    @pl.loop(0, n_full // 2)
    def _(j):
        step(2 * j)
        step(2 * j + 1)

    @pl.when(n_full % 2 == 1)
    def _():
        step(n_full - 1)

    off = pl.multiple_of(s0 - n_full * SEL_TK, AQ)
    k_last = key_start(jnp.maximum(n_full - 1, 0))
    k_diag = key_start(n_full)
    tri_g = tri_causal[:, 0:AQ]
    g_scr[...] = g_ref[...].T
    for hh in heads:
        for g in range(GROUP):
            cs = slice(g * AQ, (g + 1) * AQ)
            row = lambda k: stat_scr[hh * N_STAT + k:hh * N_STAT + k + 1, cs]
            acc = row(2) * acc_scr[hh, :, cs] + _dot(vst_ref[hh, :, pl.ds(k_last, SEL_TK)], p_scr[hh, :, cs])
            s_scr[hh, pl.ds(off, AQ), cs] = s_scr[hh, pl.ds(off, AQ), cs] + tri_g
            s_diag = s_scr[hh, :, cs]
            m_s = row(0)
            m_fin = jnp.maximum(m_s, col_max(s_diag))
            pv = _dot(vst_ref[hh, :, pl.ds(k_diag, SEL_TK)], jnp.exp2(s_diag - m_fin.astype(BF16)))
            acc = jnp.exp2(m_s - m_fin) * acc + pv
            gate = lambda c: g_scr[pl.ds(((pl.program_id(0) * NH + hh) * GROUP + g) * 3 + c, 1), :]
            out_t = (oc_scr[hh, :, cs] * gate(0)
                     + acc[0:HEAD_DIM, :] * (gate(1) / acc[HEAD_DIM:HEAD_DIM + 1, :])
                     + ow_scr[hh, 0:HEAD_DIM, cs] * (gate(2) / ow_scr[hh, HEAD_DIM:HEAD_DIM + 1, cs]))
            cols = slice((hh * GROUP + g) * HEAD_DIM, (hh * GROUP + g + 1) * HEAD_DIM)
            gn = gn_ref[:, cols]
            o_ref[:, cols] = (out_t.T * (gn * jax.nn.sigmoid(gn))).astype(BF16)


def _attention(q_r, kc, vct, ks_aug, vst, kw_aug, vwt, gates, proj):
    head_res = lambda shape: pl.BlockSpec((NH,) + shape, lambda hp, i: (hp, 0, 0), pipeline_mode=pl.Buffered(1))
    width = NH * GROUP * HEAD_DIM
    return pl.pallas_call(
        _attn_kernel,
        grid=(N_KV_HEADS // NH, SEQ // AQ),
        in_specs=[
            pl.BlockSpec((NH * GROUP, HEAD_DIM, AQ), lambda hp, i: (hp, 0, i)),
            pl.BlockSpec((1, NH, N_CMP_PAD, HEAD_DIM), lambda hp, i: (0, hp, 0, 0)),
            pl.BlockSpec((1, NH, HEAD_DIM, N_CMP_PAD), lambda hp, i: (1, hp, 0, 0)),
            head_res((SEQ, 2 * HEAD_DIM)),
            head_res((V_ROWS, SEQ)),
            head_res((WINDOW + SEQ, 2 * HEAD_DIM)),
            head_res((V_ROWS, WINDOW + SEQ)),
            pl.BlockSpec((AQ, LANES), lambda hp, i: (i, 0)),
            pl.BlockSpec((AQ, width), lambda hp, i: (i, COL_GN // width + hp)),
        ],
        out_specs=pl.BlockSpec((AQ, width), lambda hp, i: (i, hp)),
        out_shape=jax.ShapeDtypeStruct((SEQ, NSA_WIDTH), BF16),
        scratch_shapes=[pltpu.VMEM((NH, AQ // LANES, IMP_PAD + N_CMP_PAD + 8, LANES), F32),
                        pltpu.VMEM((NH, N_BLK, AQ), F32),
                        pltpu.VMEM((NH, HEAD_DIM, AR), F32),
                        pltpu.VMEM((NH, V_ROWS, AR), F32),
                        pltpu.VMEM((NH, SEL_TK, AR), BF16),
                        pltpu.VMEM((NH, SEL_TK, AR), BF16),
                        pltpu.VMEM((NH, V_ROWS, AR), F32),
                        pltpu.VMEM((NH * N_STAT, AR), F32),
                        pltpu.VMEM((LANES, AQ), F32)],
        compiler_params=pltpu.CompilerParams(
            dimension_semantics=("arbitrary", "arbitrary"), vmem_limit_bytes=VMEM_LIMIT),
        name="attn",
    )(q_r, kc, vct, ks_aug, vst, kw_aug, vwt, gates, proj)


def _pool_kernel(u_ref, up_ref, gp_ref, mix_ref, sc_ref, wpo_ref, ya_ref, ext_scr):
    i = pl.program_id(0)
    tm = POOL_TM
    ext_scr[0:POOL_HALO, :] = jnp.where(i > 0, up_ref[...], 0.0)
    ext_scr[POOL_HALO:POOL_HALO + tm, :] = u_ref[...]
    t1 = i * tm + lax.broadcasted_iota(jnp.int32, (tm, 1), 0) + 1
    zs = []
    for g, w in enumerate(POOL_WINDOWS):
        cs = slice(g * POOL_GROUP, (g + 1) * POOL_GROUP)
        u = ext_scr[POOL_HALO:POOL_HALO + tm, cs]
        acc = u
        for k in range(1, w):
            acc = acc + ext_scr[POOL_HALO - k:POOL_HALO - k + tm, cs]
        cnt = jnp.minimum(t1, w).astype(F32)
        pooled = acc / cnt - u
        zs.append(_dot(pooled.astype(BF16), mix_ref[g]))
    z = jnp.concatenate(zs, axis=1) * sc_ref[...]
    gp = gp_ref[...]
    y = z * (gp * jax.nn.sigmoid(gp))
    ya_ref[...] = _dot(y.astype(BF16), wpo_ref[...])


def _pool(proj, mix, scale, w_pool_out):
    tm = POOL_TM
    return pl.pallas_call(
        _pool_kernel,
        grid=(SEQ // tm,),
        in_specs=[
            pl.BlockSpec((tm, POOL_WIDTH), lambda i: (i, COL_U // POOL_WIDTH)),
            pl.BlockSpec((POOL_HALO, POOL_WIDTH),
                         lambda i: (jnp.maximum(i * (tm // POOL_HALO) - 1, 0), COL_U // POOL_WIDTH)),
            pl.BlockSpec((tm, POOL_WIDTH), lambda i: (i, COL_GP // POOL_WIDTH)),
            pl.BlockSpec((len(POOL_WINDOWS), POOL_GROUP, POOL_GROUP), lambda i: (0, 0, 0)),
            pl.BlockSpec((1, POOL_WIDTH), lambda i: (0, 0)),
            pl.BlockSpec((POOL_WIDTH, D_MODEL), lambda i: (0, 0)),
        ],
        out_specs=pl.BlockSpec((tm, D_MODEL), lambda i: (i, 0)),
        out_shape=jax.ShapeDtypeStruct((SEQ, D_MODEL), F32),
        scratch_shapes=[pltpu.VMEM((POOL_HALO + tm, POOL_WIDTH), F32)],
        compiler_params=pltpu.CompilerParams(
            dimension_semantics=("arbitrary",), vmem_limit_bytes=VMEM_LIMIT),
        name="pool",
    )(proj, proj, proj, mix, scale, w_pool_out)


def _out_kernel(x_ref, ya_ref, o_ref, gm0_ref, gm1_ref, wn_ref, wo_ref, fw_ref, out_ref):
    y_b = _dot(o_ref[...], wn_ref[...])
    merged = gm0_ref[...] * ya_ref[...] + gm1_ref[...] * y_b
    r = x_ref[...] + _dot(merged.astype(BF16), wo_ref[...])
    ms = jnp.mean(r * r, axis=-1, keepdims=True)
    out_ref[...] = r * lax.rsqrt(ms + EPS) * fw_ref[...]


def _output(x2, y_a, o_gated, proj, w_nsa_out, w_out, final_norm_w):
    tm = OUT_TM
    row = lambda: pl.BlockSpec((tm, D_MODEL), lambda i: (i, 0))
    resident = lambda: pl.BlockSpec((D_MODEL, D_MODEL), lambda i: (0, 0), pipeline_mode=pl.Buffered(1))
    return pl.pallas_call(
        _out_kernel,
        grid=(SEQ // tm,),
        in_specs=[
            row(), row(), row(),
            pl.BlockSpec((tm, D_MODEL), lambda i: (i, 0)),
            pl.BlockSpec((tm, D_MODEL), lambda i: (i, 1)),
            resident(), resident(),
            pl.BlockSpec((1, D_MODEL), lambda i: (0, 0)),
        ],
        out_specs=row(),
        out_shape=jax.ShapeDtypeStruct((SEQ, D_MODEL), F32),
        compiler_params=pltpu.CompilerParams(
            dimension_semantics=("arbitrary",), vmem_limit_bytes=VMEM_LIMIT),
        name="out",
    )(x2, y_a, o_gated, proj, proj, w_nsa_out, w_out, final_norm_w)


def _rope_tables():
    half = HEAD_DIM // 2
    inv = ROPE_THETA ** (-np.arange(half, dtype=np.float64) / half)
    ang = np.arange(SEQ, dtype=np.float64)[:, None] * inv[None, :]
    cos = np.concatenate([np.cos(ang), np.cos(ang)], axis=-1).astype(np.float32)
    sin = np.concatenate([-np.sin(ang), np.sin(ang)], axis=-1).astype(np.float32)
    pad = lambda t: np.pad(t[CMP_BLOCK - 1::CMP_STRIDE], ((0, N_CMP_PAD - N_CMP), (0, 0)))
    cos_c = np.stack([pad(cos), np.ones((N_CMP_PAD, HEAD_DIM), np.float32)])
    sin_c = np.stack([pad(sin), np.zeros((N_CMP_PAD, HEAD_DIM), np.float32)])
    return cos, sin, cos_c, sin_c


def kernel(x, norm_w, w_in, pool_mix, pool_scale, cmp_pe_k, cmp_w1_k, cmp_w2_k, cmp_pe_v, cmp_w1_v,
           cmp_w2_v, w_pool_out, w_nsa_out, w_merge, b_merge, w_out, final_norm_w):
    assert x.shape == (1, SEQ, D_MODEL) and norm_w.shape[0] == 1
    x2 = x.reshape(SEQ, D_MODEL)
    proj, gates = _proj(x2, norm_w, w_merge[0].astype(BF16), jnp.transpose(w_in[0]), b_merge)

    cos_tab, sin_tab, cos_c, sin_c = _rope_tables()
    q_r, ks_aug, vst, kw_aug, vwt = _prep(proj, cos_tab, sin_tab)

    pe = jnp.stack([cmp_pe_k[0], cmp_pe_v[0]])
    w1 = jnp.stack([cmp_w1_k[0], cmp_w1_v[0]]).astype(BF16)
    w2 = jnp.stack([cmp_w2_k[0], cmp_w2_v[0]]).astype(BF16)
    cmp_n, cmp_t = _compress(proj, pe, w1, w2, cos_c, sin_c)

    o_gated = _attention(q_r, cmp_n, cmp_t, ks_aug, vst, kw_aug, vwt, gates, proj)

    y_a = _pool(proj, pool_mix[0].astype(BF16), pool_scale, w_pool_out[0].astype(BF16))
    out = _output(x2, y_a, o_gated, proj, w_nsa_out[0].astype(BF16), w_out[0].astype(BF16),
                  final_norm_w[None, :])
    return out.reshape(1, SEQ, D_MODEL)
```

```python
import functools

import numpy as np
import jax
import jax.numpy as jnp
from jax import lax
from jax.experimental import pallas as pl
from jax.experimental.pallas import tpu as pltpu

F32 = jnp.float32
BF16 = jnp.bfloat16

LANES = 128

D_MODEL = 2048
SEQ = 8192
POOL_WIDTH = 1024
POOL_WINDOWS = (2, 4, 8, 16)
POOL_GROUP = 256
POOL_HALO = 16
N_Q_HEADS = 16
N_KV_HEADS = 4
GROUP = 4
HEAD_DIM = 128
NSA_WIDTH = 2048
KV_WIDTH = 512
CMP_BLOCK = 32
CMP_STRIDE = 16
CMP_HIDDEN = 256
N_CMP = (SEQ - CMP_BLOCK) // CMP_STRIDE + 1
N_CMP_PAD = 512
SEL_BLOCK = 64
N_BLK = SEQ // SEL_BLOCK
N_SEL = 16
WINDOW = 512
ROPE_THETA = 10000.0
EPS = 1e-6
NEG = -1e30
FORCE = 1e6
REMOVED = -3e38
LOG2E = 1.4426950408889634
Q_SCALE = HEAD_DIM ** -0.5 * LOG2E

COL_GM = 0
COL_U = 4096
COL_GP = 5120
COL_Q = 6144
COL_KC = 8192
COL_VC = 8704
COL_KS = 9216
COL_VS = 9728
COL_KW = 10240
COL_VW = 10752
COL_GN = 11264
N_PROJ = 13312
W_IN_MAIN = 9216
N_GATE = 3 * N_Q_HEADS

VMEM_LIMIT = 56 * 1024 * 1024
PROJ_VMEM_LIMIT = 60000 * 1024

PROJ_TM = 1024
PROJ_TN = 1024
N_GM_TILES = 4096 // PROJ_TN
PREP_TM = 512
SEL_TK = 512
AQ = 256
AR = GROUP * AQ
WIN_KEYS = WINDOW + AQ
CMP_CHUNK = 128
CMP_MASK_ROWS = 256
V_ROWS = HEAD_DIM + 16
POOL_TM = 512
OUT_TM = 256


def _dot(a, b):
    return jnp.dot(a, b, preferred_element_type=F32)


def _nt_dot(a, b):
    return lax.dot_general(a, b, (((1,), (1,)), ((), ())), preferred_element_type=F32)


def _proj_kernel(x_ref, nw_ref, wm_ref, wi_ref, wg_ref, b_ref, o_ref, g_ref, h_scr):
    j = pl.program_id(1)

    @pl.when(j == 0)
    def _():
        x = x_ref[...]
        ms = jnp.mean(x * x, axis=-1, keepdims=True)
        hb = (x * lax.rsqrt(ms + EPS) * nw_ref[...]).astype(BF16)
        h_scr[...] = hb
        row = lax.broadcasted_iota(jnp.int32, (LANES, D_MODEL), 0)
        wg = jnp.where(row < N_GATE, wg_ref[...], 0.0).astype(BF16)
        g_ref[...] = jax.nn.sigmoid(_nt_dot(hb, wg))

    @pl.when(j < N_GM_TILES)
    def _():
        o_ref[...] = jax.nn.sigmoid(_dot(h_scr[...], wm_ref[...]) + b_ref[...])

    @pl.when(j >= N_GM_TILES)
    def _():
        o_ref[...] = _nt_dot(h_scr[...], wi_ref[...].astype(BF16))


def _proj(x2, norm_w, w_merge, w_in, b_merge):
    grid = (SEQ // PROJ_TM, N_PROJ // PROJ_TN)
    return pl.pallas_call(
        _proj_kernel,
        grid=grid,
        in_specs=[
            pl.BlockSpec((PROJ_TM, D_MODEL), lambda i, j: (i, 0)),
            pl.BlockSpec((1, D_MODEL), lambda i, j: (0, 0)),
            pl.BlockSpec((D_MODEL, PROJ_TN), lambda i, j: (0, jnp.minimum(j, N_GM_TILES - 1))),
            pl.BlockSpec((PROJ_TN, D_MODEL), lambda i, j: (jnp.maximum(j - N_GM_TILES, 0), 0)),
            pl.BlockSpec((LANES, D_MODEL), lambda i, j: (W_IN_MAIN // LANES, 0)),
            pl.BlockSpec((1, PROJ_TN), lambda i, j: (0, jnp.minimum(j, N_GM_TILES - 1))),
        ],
        out_specs=[
            pl.BlockSpec((PROJ_TM, PROJ_TN), lambda i, j: (i, j)),
            pl.BlockSpec((PROJ_TM, LANES), lambda i, j: (i, 0)),
        ],
        out_shape=[
            jax.ShapeDtypeStruct((SEQ, N_PROJ), F32),
            jax.ShapeDtypeStruct((SEQ, LANES), F32),
        ],
        scratch_shapes=[pltpu.VMEM((PROJ_TM, D_MODEL), BF16)],
        compiler_params=pltpu.CompilerParams(
            dimension_semantics=("arbitrary", "arbitrary"), vmem_limit_bytes=PROJ_VMEM_LIMIT),
        name="proj",
    )(x2, norm_w, w_merge, w_in, w_in, b_merge)


def _rope(x, cos, sin_signed):
    return x * cos + pltpu.roll(x, HEAD_DIM // 2, 1) * sin_signed


def _prep_kernel(q_ref, ks_ref, vs_ref, kw_ref, vw_ref, cos_ref, sin_ref,
                 qo_ref, kso_ref, vso_ref, kwo_ref, vwo_ref):
    i = pl.program_id(0)
    t0 = jnp.maximum(i - 1, 0) * PREP_TM
    cos = cos_ref[...]
    sin = sin_ref[...]
    for hq in range(N_Q_HEADS):
        sl = slice(hq * HEAD_DIM, (hq + 1) * HEAD_DIM)
        qo_ref[hq] = (_rope(q_ref[:, sl], cos, sin) * Q_SCALE).T.astype(BF16)
    blk = (t0 + lax.broadcasted_iota(jnp.int32, (PREP_TM, N_BLK), 0)) // SEL_BLOCK
    lane = lax.broadcasted_iota(jnp.int32, (PREP_TM, N_BLK), 1)
    onehot = jnp.where(blk == lane, 1.0, 0.0).astype(BF16)
    pad_flag = jnp.where(lane == 0, 1.0, 0.0)
    is_pad = i == 0
    ones_rows = jnp.where(lax.broadcasted_iota(jnp.int32, (V_ROWS - HEAD_DIM, PREP_TM), 0) == 0,
                          1.0, 0.0).astype(BF16)
    for h in range(N_KV_HEADS):
        sl = slice(h * HEAD_DIM, (h + 1) * HEAD_DIM)
        kso_ref[h, :, 0:HEAD_DIM] = _rope(ks_ref[:, sl], cos, sin).astype(BF16)
        kso_ref[h, :, HEAD_DIM:2 * HEAD_DIM] = onehot
        vso_ref[h, 0:HEAD_DIM, :] = vs_ref[:, sl].T.astype(BF16)
        vso_ref[h, HEAD_DIM:V_ROWS, :] = ones_rows
        kwo_ref[h, :, 0:HEAD_DIM] = jnp.where(is_pad, 0.0, _rope(kw_ref[:, sl], cos, sin)).astype(BF16)
        kwo_ref[h, :, HEAD_DIM:2 * HEAD_DIM] = jnp.where(is_pad, pad_flag, 0.0).astype(BF16)
        vwo_ref[h, 0:HEAD_DIM, :] = jnp.where(is_pad, 0.0, vw_ref[:, sl].T).astype(BF16)
        vwo_ref[h, HEAD_DIM:V_ROWS, :] = ones_rows


def _prep(proj, cos_tab, sin_tab):
    tm = PREP_TM
    assert tm == WINDOW
    tok = lambda i: jnp.maximum(i - 1, 0)
    kv_spec = lambda col: pl.BlockSpec((tm, KV_WIDTH), lambda i: (tok(i), col // KV_WIDTH))
    return pl.pallas_call(
        _prep_kernel,
        grid=(SEQ // tm + 1,),
        in_specs=[
            pl.BlockSpec((tm, NSA_WIDTH), lambda i: (tok(i), COL_Q // NSA_WIDTH)),
            kv_spec(COL_KS), kv_spec(COL_VS), kv_spec(COL_KW), kv_spec(COL_VW),
            pl.BlockSpec((tm, HEAD_DIM), lambda i: (tok(i), 0)),
            pl.BlockSpec((tm, HEAD_DIM), lambda i: (tok(i), 0)),
        ],
        out_specs=[
            pl.BlockSpec((N_Q_HEADS, HEAD_DIM, tm), lambda i: (0, 0, tok(i))),
            pl.BlockSpec((N_KV_HEADS, tm, 2 * HEAD_DIM), lambda i: (0, tok(i), 0)),
            pl.BlockSpec((N_KV_HEADS, V_ROWS, tm), lambda i: (0, 0, tok(i))),
            pl.BlockSpec((N_KV_HEADS, tm, 2 * HEAD_DIM), lambda i: (0, i, 0)),
            pl.BlockSpec((N_KV_HEADS, V_ROWS, tm), lambda i: (0, 0, i)),
        ],
        out_shape=[
            jax.ShapeDtypeStruct((N_Q_HEADS, HEAD_DIM, SEQ), BF16),
            jax.ShapeDtypeStruct((N_KV_HEADS, SEQ, 2 * HEAD_DIM), BF16),
            jax.ShapeDtypeStruct((N_KV_HEADS, V_ROWS, SEQ), BF16),
            jax.ShapeDtypeStruct((N_KV_HEADS, WINDOW + SEQ, 2 * HEAD_DIM), BF16),
            jax.ShapeDtypeStruct((N_KV_HEADS, V_ROWS, WINDOW + SEQ), BF16),
        ],
        compiler_params=pltpu.CompilerParams(
            dimension_semantics=("arbitrary",), vmem_limit_bytes=VMEM_LIMIT),
        name="prep",
    )(proj, proj, proj, proj, proj, cos_tab, sin_tab)


def _cmp_kernel(x_ref, pe_ref, w1_ref, w2_ref, cos_ref, sin_ref, o_ref, ot_ref, hi_scr):
    half = CMP_BLOCK // 2
    acc_lo = jnp.zeros((N_CMP_PAD, CMP_HIDDEN), F32)
    acc_hi = jnp.zeros((N_CMP_PAD, CMP_HIDDEN), F32)
    for l in range(half):
        xl = x_ref[pl.ds(l, N_CMP_PAD, stride=CMP_STRIDE), :]
        acc_lo += _dot((xl + pe_ref[0, l:l + 1, :]).astype(BF16), w1_ref[0, l])
        acc_hi += _dot((xl + pe_ref[0, half + l:half + l + 1, :]).astype(BF16), w1_ref[0, half + l])
    hi_scr[0:N_CMP_PAD, :] = acc_hi
    hi_scr[N_CMP_PAD:N_CMP_PAD + 8, :] = jnp.zeros((8, CMP_HIDDEN), F32)
    pre = acc_lo + hi_scr[pl.ds(1, N_CMP_PAD), :]
    hdn = pre * jax.nn.sigmoid(pre)
    out = _dot(hdn.astype(BF16), w2_ref[0])
    out = _rope(out, cos_ref[0], sin_ref[0])
    row = lax.broadcasted_iota(jnp.int32, (N_CMP_PAD, HEAD_DIM), 0)
    out = jnp.where(row < N_CMP, out, 0.0)
    o_ref[0, 0] = out.astype(BF16)
    ot_ref[0, 0] = out.T.astype(BF16)


def _compress(proj, pe, w1, w2, cos_c, sin_c):
    return pl.pallas_call(
        _cmp_kernel,
        grid=(2, N_KV_HEADS),
        in_specs=[
            pl.BlockSpec((SEQ, HEAD_DIM), lambda k, h: (0, COL_KC // HEAD_DIM + N_KV_HEADS * k + h)),
            pl.BlockSpec((1, CMP_BLOCK, HEAD_DIM), lambda k, h: (k, 0, 0)),
            pl.BlockSpec((1, CMP_BLOCK, HEAD_DIM, CMP_HIDDEN), lambda k, h: (k, 0, 0, 0)),
            pl.BlockSpec((1, CMP_HIDDEN, HEAD_DIM), lambda k, h: (k, 0, 0)),
            pl.BlockSpec((1, N_CMP_PAD, HEAD_DIM), lambda k, h: (k, 0, 0)),
            pl.BlockSpec((1, N_CMP_PAD, HEAD_DIM), lambda k, h: (k, 0, 0)),
        ],
        out_specs=[
            pl.BlockSpec((1, 1, N_CMP_PAD, HEAD_DIM), lambda k, h: (k, h, 0, 0)),
            pl.BlockSpec((1, 1, HEAD_DIM, N_CMP_PAD), lambda k, h: (k, h, 0, 0)),
        ],
        out_shape=[
            jax.ShapeDtypeStruct((2, N_KV_HEADS, N_CMP_PAD, HEAD_DIM), BF16),
            jax.ShapeDtypeStruct((2, N_KV_HEADS, HEAD_DIM, N_CMP_PAD), BF16),
        ],
        scratch_shapes=[pltpu.VMEM((N_CMP_PAD + 8, CMP_HIDDEN), F32)],
        compiler_params=pltpu.CompilerParams(
            dimension_semantics=("arbitrary", "arbitrary"), vmem_limit_bytes=VMEM_LIMIT),
        name="compress",
    )(proj, pe, w1, w2, cos_c, sin_c)


IMP_PAD = 8
NH = 2
N_STAT = 3


def _attn_kernel(q_ref, kc_ref, vct_ref, ks_ref, vst_ref, kw_ref, vwt_ref, g_ref, gn_ref,
                 o_ref, imp_scr, sel_scr, oc_scr, ow_scr, s_scr, p_scr, acc_scr, stat_scr, g_scr):
    i = pl.program_id(1)
    s0 = i * AQ
    nrow = AR
    heads = range(NH)

    q_t = [jnp.concatenate([q_ref[hh * GROUP + g] for g in range(GROUP)], axis=1) for hh in heads]

    def mask_groups(s, mask):
        return jnp.concatenate(
            [jnp.where(mask, s[:, g * AQ:(g + 1) * AQ], NEG) for g in range(GROUP)], axis=1)

    def compressed(n_chunks):
        rows = CMP_CHUNK * n_chunks
        mrows = min(rows, CMP_MASK_ROWS)
        c0 = rows - mrows
        r_i = lax.broadcasted_iota(jnp.int32, (mrows, AQ), 0)
        t_i = lax.broadcasted_iota(jnp.int32, (mrows, AQ), 1)
        visible = CMP_STRIDE * r_i - t_i <= s0 - (CMP_BLOCK - 1) - CMP_STRIDE * c0
        rest = IMP_PAD + N_CMP_PAD + 8 - (IMP_PAD + rows)
        for hh in heads:
            sc = _dot(kc_ref[0, hh, 0:rows, :], q_t[hh])
            tail = mask_groups(sc[c0:rows], visible)
            sc = tail if c0 == 0 else jnp.concatenate([sc[0:c0], tail], axis=0)
            m_c = jnp.max(sc, axis=0, keepdims=True)
            m_c = jnp.where(m_c > 0.5 * NEG, m_c, 0.0)
            e_c = jnp.exp2(sc - m_c)
            l_c = jnp.sum(e_c, axis=0, keepdims=True)
            p_c = e_c * (1.0 / jnp.where(l_c > 0.0, l_c, 1.0))
            oc_scr[hh] = _dot(vct_ref[0, hh, :, 0:rows], p_c.astype(BF16))
            imp_c = (p_c[:, 0:AQ] + p_c[:, AQ:2 * AQ]) + (p_c[:, 2 * AQ:3 * AQ] + p_c[:, 3 * AQ:4 * AQ])
            for a in range(AQ // LANES):
                imp_scr[hh, a, IMP_PAD:IMP_PAD + rows, :] = imp_c[:, a * LANES:(a + 1) * LANES]
                imp_scr[hh, a, IMP_PAD + rows:IMP_PAD + rows + rest, :] = jnp.zeros((rest, LANES), F32)

    for hh in heads:
        for a in range(AQ // LANES):
            imp_scr[hh, a, 0:IMP_PAD, :] = jnp.zeros((IMP_PAD, LANES), F32)
    n_cmp_chunks = i // (CMP_CHUNK * CMP_STRIDE // AQ) + 1
    for n in range(1, N_CMP_PAD // CMP_CHUNK + 1):
        pl.when(n_cmp_chunks == n)(functools.partial(compressed, n))

    ratio = SEL_BLOCK // CMP_STRIDE

    def importance(hh):
        tap = lambda k: jnp.concatenate(
            [imp_scr[hh, a, pl.ds(IMP_PAD + k, N_BLK, stride=ratio), :] for a in range(AQ // LANES)], axis=1)
        return tap(-1) + 2.0 * (tap(0) + tap(1) + tap(2)) + tap(3)

    jidx = lax.broadcasted_iota(jnp.int32, (N_BLK, AQ), 0).astype(F32)
    tok_l = lax.broadcasted_iota(jnp.int32, (1, AQ), 1)
    jt = (i * (AQ // SEL_BLOCK) + tok_l // SEL_BLOCK).astype(F32)
    miscount = []
    for hh in heads:
        v = jnp.where(jidx == 0.0, REMOVED, importance(hh))
        v = jnp.where(jidx == jt, REMOVED, v)
        v = jnp.where(jidx == jt - 1.0, REMOVED, v)
        v = jnp.where(jidx > jt, NEG, v)
        for _ in range(N_SEL - 3):
            v = jnp.where(v == jnp.max(v, axis=0, keepdims=True), REMOVED, v)
        sel_fast = jnp.where(jidx > jt, 0.0, jnp.where(v == REMOVED, 1.0, 0.0))
        n_picked = jnp.sum(sel_fast, axis=0, keepdims=True)
        miscount.append(jnp.max(jnp.abs(n_picked - jnp.minimum(jt + 1.0, float(N_SEL)))))
        sel_scr[hh] = sel_fast

    flag_pen = jnp.where(lax.broadcasted_iota(jnp.int32, (HEAD_DIM, nrow), 0) == 0, NEG, 0.0).astype(BF16)
    wk0 = pl.multiple_of(s0, AQ)
    r_i = lax.broadcasted_iota(jnp.int32, (AQ, AQ), 0)
    t_i = lax.broadcasted_iota(jnp.int32, (AQ, AQ), 1)
    tri_old = jnp.concatenate([jnp.where(r_i > t_i, 0.0, NEG).astype(BF16)] * GROUP, axis=1)
    tri_causal = jnp.concatenate([jnp.where(r_i <= t_i, 0.0, NEG).astype(BF16)] * GROUP, axis=1)
    for hh in heads:
        q_win = jnp.concatenate([q_t[hh], flag_pen], axis=0)
        sw = _dot(kw_ref[hh, pl.ds(wk0, WIN_KEYS), :], q_win).astype(BF16)
        sw = jnp.concatenate([sw[0:AQ] + tri_old, sw[AQ:WINDOW], sw[WINDOW:WIN_KEYS] + tri_causal], axis=0)
        m_w = jnp.max(sw, axis=0, keepdims=True)
        p_w = jnp.exp2(sw - m_w)
        ow_scr[hh] = _dot(vwt_ref[hh, :, pl.ds(wk0, WIN_KEYS)], p_w)
        s_scr[hh] = _dot(ks_ref[hh, 0:SEL_TK, 0:HEAD_DIM], q_t[hh]).astype(BF16)

    @pl.when(functools.reduce(jnp.maximum, miscount) > 0.5)
    def _():
        for hh in heads:
            w = jnp.where(jidx == 0.0, FORCE, importance(hh))
            w = jnp.where(jidx == jt, FORCE, w)
            w = jnp.where(jidx == jt - 1.0, FORCE, w)
            w = jnp.where(jidx > jt, NEG, w)
            sel = jnp.zeros((N_BLK, AQ), F32)
            for _ in range(N_SEL):
                mx = jnp.max(w, axis=0, keepdims=True)
                cand = jnp.where(w == mx, jidx, float(N_BLK))
                pick = jidx == jnp.min(cand, axis=0, keepdims=True)
                sel = jnp.where(pick, 1.0, sel)
                w = jnp.where(pick, REMOVED, w)
            sel_scr[hh] = jnp.where(jidx > jt, 0.0, sel)

    q_aug = []
    for hh in heads:
        pen = jnp.where(sel_scr[hh] > 0.5, 0.0, NEG).astype(BF16)
        q_aug.append(jnp.concatenate([q_t[hh], jnp.concatenate([pen] * GROUP, axis=1)], axis=0))

    def key_start(kt):
        return pl.multiple_of(kt * SEL_TK, SEL_TK)

    def col_max(s):
        return jnp.max(s, axis=0, keepdims=True).astype(F32)

    def stat(hh, k):
        return stat_scr[hh * N_STAT + k:hh * N_STAT + k + 1, :]

    def set_stat(hh, k, value):
        stat_scr[hh * N_STAT + k:hh * N_STAT + k + 1, :] = value

    def add_pv(hh, kt, p, alpha):
        pv = _dot(vst_ref[hh, :, pl.ds(key_start(kt), SEL_TK)], p)
        acc_scr[hh] = alpha * acc_scr[hh] + pv

    def step(kt):
        k_prev = key_start(jnp.maximum(kt - 1, 0))
        k_next = key_start(kt + 1)
        for g in range(GROUP):
            for hh in heads:
                cs = slice(g * AQ, (g + 1) * AQ)
                row = lambda k: stat_scr[hh * N_STAT + k:hh * N_STAT + k + 1, cs]
                m, m_tile, alpha_prev = row(0), row(1), row(2)
                p_prev = p_scr[hh, :, cs]
                s_cur = s_scr[hh, :, cs]
                s_nxt = _dot(ks_ref[hh, pl.ds(k_next, SEL_TK), :], q_aug[hh][:, cs]).astype(BF16)
                pv = _dot(vst_ref[hh, :, pl.ds(k_prev, SEL_TK)], p_prev)
                acc_scr[hh, :, cs] = alpha_prev * acc_scr[hh, :, cs] + pv
                m_new = jnp.maximum(m, m_tile)
                p_scr[hh, :, cs] = jnp.exp2(s_cur - m_new.astype(BF16))
                s_scr[hh, :, cs] = s_nxt
                stat_scr[hh * N_STAT + 0:hh * N_STAT + 1, cs] = m_new
                stat_scr[hh * N_STAT + 1:hh * N_STAT + 2, cs] = col_max(s_nxt)
                stat_scr[hh * N_STAT + 2:hh * N_STAT + 3, cs] = jnp.exp2(m - m_new)

    n_full = (i * AQ) // SEL_TK
    for hh in heads:
        pen0 = jnp.where(sel_scr[hh, 0:SEL_TK // SEL_BLOCK, :] > 0.5, 0.0, NEG)
        pen0 = jnp.concatenate([jnp.broadcast_to(pen0[b:b + 1, :], (SEL_BLOCK, AQ))
                                for b in range(SEL_TK // SEL_BLOCK)], axis=0).astype(BF16)
        s_first = s_scr[hh] + jnp.concatenate([pen0] * GROUP, axis=1)
        s_scr[hh] = s_first
        p_scr[hh] = jnp.zeros((SEL_TK, nrow), BF16)
        acc_scr[hh] = jnp.zeros((V_ROWS, nrow), F32)
        set_stat(hh, 0, jnp.full((1, nrow), NEG, F32))
        set_stat(hh, 1, col_max(s_first))
        set_stat(hh, 2, jnp.ones((1, nrow), F32))

    @pl.loop(0, n_full // 2)
    def _(j):
        step(2 * j)
        step(2 * j + 1)

    @pl.when(n_full % 2 == 1)
    def _():
        step(n_full - 1)

    off = pl.multiple_of(s0 - n_full * SEL_TK, AQ)
    for hh in heads:
        add_pv(hh, jnp.maximum(n_full - 1, 0), p_scr[hh], stat(hh, 2))
        s_scr[hh, pl.ds(off, AQ), :] = s_scr[hh, pl.ds(off, AQ), :] + tri_causal
        s_diag = s_scr[hh]
        m_s = stat(hh, 0)
        m_fin = jnp.maximum(m_s, col_max(s_diag))
        add_pv(hh, n_full, jnp.exp2(s_diag - m_fin.astype(BF16)), jnp.exp2(m_s - m_fin))

    g_scr[...] = g_ref[...].T
    for hh in heads:
        row0 = ((pl.program_id(0) * NH + hh) * GROUP) * 3
        grow = lambda c: jnp.concatenate(
            [g_scr[pl.ds(row0 + g * 3 + c, 1), :] for g in range(GROUP)], axis=1)
        l_s = acc_scr[hh, HEAD_DIM:HEAD_DIM + 1, :]
        l_w = ow_scr[hh, HEAD_DIM:HEAD_DIM + 1, :]
        out_t = (oc_scr[hh] * grow(0) + acc_scr[hh, 0:HEAD_DIM, :] * (grow(1) / l_s)
                 + ow_scr[hh, 0:HEAD_DIM, :] * (grow(2) / l_w))
        out = jnp.concatenate([out_t[:, g * AQ:(g + 1) * AQ].T for g in range(GROUP)], axis=1)
        cols = slice(hh * GROUP * HEAD_DIM, (hh + 1) * GROUP * HEAD_DIM)
        gn = gn_ref[:, cols]
        o_ref[:, cols] = (out * (gn * jax.nn.sigmoid(gn))).astype(BF16)


def _attention(q_r, kc, vct, ks_aug, vst, kw_aug, vwt, gates, proj):
    head_res = lambda shape: pl.BlockSpec((NH,) + shape, lambda hp, i: (hp, 0, 0), pipeline_mode=pl.Buffered(1))
    width = NH * GROUP * HEAD_DIM
    return pl.pallas_call(
        _attn_kernel,
        grid=(N_KV_HEADS // NH, SEQ // AQ),
        in_specs=[
            pl.BlockSpec((NH * GROUP, HEAD_DIM, AQ), lambda hp, i: (hp, 0, i)),
            pl.BlockSpec((1, NH, N_CMP_PAD, HEAD_DIM), lambda hp, i: (0, hp, 0, 0)),
            pl.BlockSpec((1, NH, HEAD_DIM, N_CMP_PAD), lambda hp, i: (1, hp, 0, 0)),
            head_res((SEQ, 2 * HEAD_DIM)),
            head_res((V_ROWS, SEQ)),
            head_res((WINDOW + SEQ, 2 * HEAD_DIM)),
            head_res((V_ROWS, WINDOW + SEQ)),
            pl.BlockSpec((AQ, LANES), lambda hp, i: (i, 0)),
            pl.BlockSpec((AQ, width), lambda hp, i: (i, COL_GN // width + hp)),
        ],
        out_specs=pl.BlockSpec((AQ, width), lambda hp, i: (i, hp)),
        out_shape=jax.ShapeDtypeStruct((SEQ, NSA_WIDTH), BF16),
        scratch_shapes=[pltpu.VMEM((NH, AQ // LANES, IMP_PAD + N_CMP_PAD + 8, LANES), F32),
                        pltpu.VMEM((NH, N_BLK, AQ), F32),
                        pltpu.VMEM((NH, HEAD_DIM, AR), F32),
                        pltpu.VMEM((NH, V_ROWS, AR), F32),
                        pltpu.VMEM((NH, SEL_TK, AR), BF16),
                        pltpu.VMEM((NH, SEL_TK, AR), BF16),
                        pltpu.VMEM((NH, V_ROWS, AR), F32),
                        pltpu.VMEM((NH * N_STAT, AR), F32),
                        pltpu.VMEM((LANES, AQ), F32)],
        compiler_params=pltpu.CompilerParams(
            dimension_semantics=("arbitrary", "arbitrary"), vmem_limit_bytes=VMEM_LIMIT),
        name="attn",
    )(q_r, kc, vct, ks_aug, vst, kw_aug, vwt, gates, proj)


def _pool_kernel(u_ref, up_ref, gp_ref, mix_ref, sc_ref, wpo_ref, ya_ref, ext_scr):
    i = pl.program_id(0)
    tm = POOL_TM
    ext_scr[0:POOL_HALO, :] = jnp.where(i > 0, up_ref[...], 0.0)
    ext_scr[POOL_HALO:POOL_HALO + tm, :] = u_ref[...]
    t1 = i * tm + lax.broadcasted_iota(jnp.int32, (tm, 1), 0) + 1
    zs = []
    for g, w in enumerate(POOL_WINDOWS):
        cs = slice(g * POOL_GROUP, (g + 1) * POOL_GROUP)
        u = ext_scr[POOL_HALO:POOL_HALO + tm, cs]
        acc = u
        for k in range(1, w):
            acc = acc + ext_scr[POOL_HALO - k:POOL_HALO - k + tm, cs]
        cnt = jnp.minimum(t1, w).astype(F32)
        pooled = acc / cnt - u
        zs.append(_dot(pooled.astype(BF16), mix_ref[g]))
    z = jnp.concatenate(zs, axis=1) * sc_ref[...]
    gp = gp_ref[...]
    y = z * (gp * jax.nn.sigmoid(gp))
    ya_ref[...] = _dot(y.astype(BF16), wpo_ref[...])


def _pool(proj, mix, scale, w_pool_out):
    tm = POOL_TM
    return pl.pallas_call(
        _pool_kernel,
        grid=(SEQ // tm,),
        in_specs=[
            pl.BlockSpec((tm, POOL_WIDTH), lambda i: (i, COL_U // POOL_WIDTH)),
            pl.BlockSpec((POOL_HALO, POOL_WIDTH),
                         lambda i: (jnp.maximum(i * (tm // POOL_HALO) - 1, 0), COL_U // POOL_WIDTH)),
            pl.BlockSpec((tm, POOL_WIDTH), lambda i: (i, COL_GP // POOL_WIDTH)),
            pl.BlockSpec((len(POOL_WINDOWS), POOL_GROUP, POOL_GROUP), lambda i: (0, 0, 0)),
            pl.BlockSpec((1, POOL_WIDTH), lambda i: (0, 0)),
            pl.BlockSpec((POOL_WIDTH, D_MODEL), lambda i: (0, 0)),
        ],
        out_specs=pl.BlockSpec((tm, D_MODEL), lambda i: (i, 0)),
        out_shape=jax.ShapeDtypeStruct((SEQ, D_MODEL), F32),
        scratch_shapes=[pltpu.VMEM((POOL_HALO + tm, POOL_WIDTH), F32)],
        compiler_params=pltpu.CompilerParams(
            dimension_semantics=("arbitrary",), vmem_limit_bytes=VMEM_LIMIT),
        name="pool",
    )(proj, proj, proj, mix, scale, w_pool_out)


def _out_kernel(x_ref, ya_ref, o_ref, gm0_ref, gm1_ref, wn_ref, wo_ref, fw_ref, out_ref):
    y_b = _dot(o_ref[...], wn_ref[...])
    merged = gm0_ref[...] * ya_ref[...] + gm1_ref[...] * y_b
    r = x_ref[...] + _dot(merged.astype(BF16), wo_ref[...])
    ms = jnp.mean(r * r, axis=-1, keepdims=True)
    out_ref[...] = r * lax.rsqrt(ms + EPS) * fw_ref[...]


def _output(x2, y_a, o_gated, proj, w_nsa_out, w_out, final_norm_w):
    tm = OUT_TM
    row = lambda: pl.BlockSpec((tm, D_MODEL), lambda i: (i, 0))
    resident = lambda: pl.BlockSpec((D_MODEL, D_MODEL), lambda i: (0, 0), pipeline_mode=pl.Buffered(1))
    return pl.pallas_call(
        _out_kernel,
        grid=(SEQ // tm,),
        in_specs=[
            row(), row(), row(),
            pl.BlockSpec((tm, D_MODEL), lambda i: (i, 0)),
            pl.BlockSpec((tm, D_MODEL), lambda i: (i, 1)),
            resident(), resident(),
            pl.BlockSpec((1, D_MODEL), lambda i: (0, 0)),
        ],
        out_specs=row(),
        out_shape=jax.ShapeDtypeStruct((SEQ, D_MODEL), F32),
        compiler_params=pltpu.CompilerParams(
            dimension_semantics=("arbitrary",), vmem_limit_bytes=VMEM_LIMIT),
        name="out",
    )(x2, y_a, o_gated, proj, proj, w_nsa_out, w_out, final_norm_w)


def _rope_tables():
    half = HEAD_DIM // 2
    inv = ROPE_THETA ** (-np.arange(half, dtype=np.float64) / half)
    ang = np.arange(SEQ, dtype=np.float64)[:, None] * inv[None, :]
    cos = np.concatenate([np.cos(ang), np.cos(ang)], axis=-1).astype(np.float32)
    sin = np.concatenate([-np.sin(ang), np.sin(ang)], axis=-1).astype(np.float32)
    pad = lambda t: np.pad(t[CMP_BLOCK - 1::CMP_STRIDE], ((0, N_CMP_PAD - N_CMP), (0, 0)))
    cos_c = np.stack([pad(cos), np.ones((N_CMP_PAD, HEAD_DIM), np.float32)])
    sin_c = np.stack([pad(sin), np.zeros((N_CMP_PAD, HEAD_DIM), np.float32)])
    return cos, sin, cos_c, sin_c


def kernel(x, norm_w, w_in, pool_mix, pool_scale, cmp_pe_k, cmp_w1_k, cmp_w2_k, cmp_pe_v, cmp_w1_v,
           cmp_w2_v, w_pool_out, w_nsa_out, w_merge, b_merge, w_out, final_norm_w):
    assert x.shape == (1, SEQ, D_MODEL) and norm_w.shape[0] == 1
    x2 = x.reshape(SEQ, D_MODEL)
    proj, gates = _proj(x2, norm_w, w_merge[0].astype(BF16), jnp.transpose(w_in[0]), b_merge)

    cos_tab, sin_tab, cos_c, sin_c = _rope_tables()
    q_r, ks_aug, vst, kw_aug, vwt = _prep(proj, cos_tab, sin_tab)

    pe = jnp.stack([cmp_pe_k[0], cmp_pe_v[0]])
    w1 = jnp.stack([cmp_w1_k[0], cmp_w1_v[0]]).astype(BF16)
    w2 = jnp.stack([cmp_w2_k[0], cmp_w2_v[0]]).astype(BF16)
    cmp_n, cmp_t = _compress(proj, pe, w1, w2, cos_c, sin_c)

    o_gated = _attention(q_r, cmp_n, cmp_t, ks_aug, vst, kw_aug, vwt, gates, proj)

    y_a = _pool(proj, pool_mix[0].astype(BF16), pool_scale, w_pool_out[0].astype(BF16))
    out = _output(x2, y_a, o_gated, proj, w_nsa_out[0].astype(BF16), w_out[0].astype(BF16),
                  final_norm_w[None, :])
    return out.reshape(1, SEQ, D_MODEL)
```

```python
import functools

import numpy as np
import jax
import jax.numpy as jnp
from jax import lax
from jax.experimental import pallas as pl
from jax.experimental.pallas import tpu as pltpu

F32 = jnp.float32
BF16 = jnp.bfloat16

LANES = 128

D_MODEL = 2048
SEQ = 8192
POOL_WIDTH = 1024
POOL_WINDOWS = (2, 4, 8, 16)
POOL_GROUP = 256
POOL_HALO = 16
N_Q_HEADS = 16
N_KV_HEADS = 4
GROUP = 4
HEAD_DIM = 128
NSA_WIDTH = 2048
KV_WIDTH = 512
CMP_BLOCK = 32
CMP_STRIDE = 16
CMP_HIDDEN = 256
N_CMP = (SEQ - CMP_BLOCK) // CMP_STRIDE + 1
N_CMP_PAD = 512
SEL_BLOCK = 64
N_BLK = SEQ // SEL_BLOCK
N_SEL = 16
WINDOW = 512
ROPE_THETA = 10000.0
EPS = 1e-6
NEG = -1e30
FORCE = 1e6
REMOVED = -3e38
LOG2E = 1.4426950408889634
Q_SCALE = HEAD_DIM ** -0.5 * LOG2E

COL_GM = 0
COL_U = 4096
COL_GP = 5120
COL_Q = 6144
COL_KC = 8192
COL_VC = 8704
COL_KS = 9216
COL_VS = 9728
COL_KW = 10240
COL_VW = 10752
COL_GN = 11264
N_PROJ = 13312
W_IN_MAIN = 9216
N_GATE = 3 * N_Q_HEADS

VMEM_LIMIT = 56 * 1024 * 1024
PROJ_VMEM_LIMIT = 60000 * 1024

PROJ_TM = 1024
PROJ_TN = 1024
N_GM_TILES = 4096 // PROJ_TN
PREP_TM = 512
SEL_TK = 512
AQ = 256
AR = GROUP * AQ
WIN_KEYS = WINDOW + AQ
CMP_CHUNK = 128
CMP_MASK_ROWS = 256
V_ROWS = HEAD_DIM + 16
POOL_TM = 512
OUT_TM = 256


def _dot(a, b):
    return jnp.dot(a, b, preferred_element_type=F32)


def _nt_dot(a, b):
    return lax.dot_general(a, b, (((1,), (1,)), ((), ())), preferred_element_type=F32)


def _proj_kernel(x_ref, nw_ref, wm_ref, wi_ref, wg_ref, b_ref, o_ref, g_ref, h_scr):
    j = pl.program_id(1)

    @pl.when(j == 0)
    def _():
        x = x_ref[...]
        ms = jnp.mean(x * x, axis=-1, keepdims=True)
        hb = (x * lax.rsqrt(ms + EPS) * nw_ref[...]).astype(BF16)
        h_scr[...] = hb
        row = lax.broadcasted_iota(jnp.int32, (LANES, D_MODEL), 0)
        wg = jnp.where(row < N_GATE, wg_ref[...], 0.0).astype(BF16)
        g_ref[...] = jax.nn.sigmoid(_nt_dot(hb, wg))

    @pl.when(j < N_GM_TILES)
    def _():
        o_ref[...] = jax.nn.sigmoid(_dot(h_scr[...], wm_ref[...]) + b_ref[...])

    @pl.when(j >= N_GM_TILES)
    def _():
        o_ref[...] = _nt_dot(h_scr[...], wi_ref[...].astype(BF16))


def _proj(x2, norm_w, w_merge, w_in, b_merge):
    grid = (SEQ // PROJ_TM, N_PROJ // PROJ_TN)
    return pl.pallas_call(
        _proj_kernel,
        grid=grid,
        in_specs=[
            pl.BlockSpec((PROJ_TM, D_MODEL), lambda i, j: (i, 0)),
            pl.BlockSpec((1, D_MODEL), lambda i, j: (0, 0)),
            pl.BlockSpec((D_MODEL, PROJ_TN), lambda i, j: (0, jnp.minimum(j, N_GM_TILES - 1))),
            pl.BlockSpec((PROJ_TN, D_MODEL), lambda i, j: (jnp.maximum(j - N_GM_TILES, 0), 0)),
            pl.BlockSpec((LANES, D_MODEL), lambda i, j: (W_IN_MAIN // LANES, 0)),
            pl.BlockSpec((1, PROJ_TN), lambda i, j: (0, jnp.minimum(j, N_GM_TILES - 1))),
        ],
        out_specs=[
            pl.BlockSpec((PROJ_TM, PROJ_TN), lambda i, j: (i, j)),
            pl.BlockSpec((PROJ_TM, LANES), lambda i, j: (i, 0)),
        ],
        out_shape=[
            jax.ShapeDtypeStruct((SEQ, N_PROJ), F32),
            jax.ShapeDtypeStruct((SEQ, LANES), F32),
        ],
        scratch_shapes=[pltpu.VMEM((PROJ_TM, D_MODEL), BF16)],
        compiler_params=pltpu.CompilerParams(
            dimension_semantics=("arbitrary", "arbitrary"), vmem_limit_bytes=PROJ_VMEM_LIMIT),
        name="proj",
    )(x2, norm_w, w_merge, w_in, w_in, b_merge)


def _rope(x, cos, sin_signed):
    return x * cos + pltpu.roll(x, HEAD_DIM // 2, 1) * sin_signed


def _prep_kernel(q_ref, ks_ref, vs_ref, kw_ref, vw_ref, cos_ref, sin_ref,
                 qo_ref, kso_ref, vso_ref, kwo_ref, vwo_ref):
    i = pl.program_id(0)
    t0 = jnp.maximum(i - 1, 0) * PREP_TM
    cos = cos_ref[...]
    sin = sin_ref[...]
    for hq in range(N_Q_HEADS):
        sl = slice(hq * HEAD_DIM, (hq + 1) * HEAD_DIM)
        qo_ref[hq] = (_rope(q_ref[:, sl], cos, sin) * Q_SCALE).T.astype(BF16)
    blk = (t0 + lax.broadcasted_iota(jnp.int32, (PREP_TM, N_BLK), 0)) // SEL_BLOCK
    lane = lax.broadcasted_iota(jnp.int32, (PREP_TM, N_BLK), 1)
    onehot = jnp.where(blk == lane, 1.0, 0.0).astype(BF16)
    pad_flag = jnp.where(lane == 0, 1.0, 0.0)
    is_pad = i == 0
    ones_rows = jnp.where(lax.broadcasted_iota(jnp.int32, (V_ROWS - HEAD_DIM, PREP_TM), 0) == 0,
                          1.0, 0.0).astype(BF16)
    for h in range(N_KV_HEADS):
        sl = slice(h * HEAD_DIM, (h + 1) * HEAD_DIM)
        kso_ref[h, :, 0:HEAD_DIM] = _rope(ks_ref[:, sl], cos, sin).astype(BF16)
        kso_ref[h, :, HEAD_DIM:2 * HEAD_DIM] = onehot
        vso_ref[h, 0:HEAD_DIM, :] = vs_ref[:, sl].T.astype(BF16)
        vso_ref[h, HEAD_DIM:V_ROWS, :] = ones_rows
        kwo_ref[h, :, 0:HEAD_DIM] = jnp.where(is_pad, 0.0, _rope(kw_ref[:, sl], cos, sin)).astype(BF16)
        kwo_ref[h, :, HEAD_DIM:2 * HEAD_DIM] = jnp.where(is_pad, pad_flag, 0.0).astype(BF16)
        vwo_ref[h, 0:HEAD_DIM, :] = jnp.where(is_pad, 0.0, vw_ref[:, sl].T).astype(BF16)
        vwo_ref[h, HEAD_DIM:V_ROWS, :] = ones_rows


def _prep(proj, cos_tab, sin_tab):
    tm = PREP_TM
    assert tm == WINDOW
    tok = lambda i: jnp.maximum(i - 1, 0)
    kv_spec = lambda col: pl.BlockSpec((tm, KV_WIDTH), lambda i: (tok(i), col // KV_WIDTH))
    return pl.pallas_call(
        _prep_kernel,
        grid=(SEQ // tm + 1,),
        in_specs=[
            pl.BlockSpec((tm, NSA_WIDTH), lambda i: (tok(i), COL_Q // NSA_WIDTH)),
            kv_spec(COL_KS), kv_spec(COL_VS), kv_spec(COL_KW), kv_spec(COL_VW),
            pl.BlockSpec((tm, HEAD_DIM), lambda i: (tok(i), 0)),
            pl.BlockSpec((tm, HEAD_DIM), lambda i: (tok(i), 0)),
        ],
        out_specs=[
            pl.BlockSpec((N_Q_HEADS, HEAD_DIM, tm), lambda i: (0, 0, tok(i))),
            pl.BlockSpec((N_KV_HEADS, tm, 2 * HEAD_DIM), lambda i: (0, tok(i), 0)),
            pl.BlockSpec((N_KV_HEADS, V_ROWS, tm), lambda i: (0, 0, tok(i))),
            pl.BlockSpec((N_KV_HEADS, tm, 2 * HEAD_DIM), lambda i: (0, i, 0)),
            pl.BlockSpec((N_KV_HEADS, V_ROWS, tm), lambda i: (0, 0, i)),
        ],
        out_shape=[
            jax.ShapeDtypeStruct((N_Q_HEADS, HEAD_DIM, SEQ), BF16),
            jax.ShapeDtypeStruct((N_KV_HEADS, SEQ, 2 * HEAD_DIM), BF16),
            jax.ShapeDtypeStruct((N_KV_HEADS, V_ROWS, SEQ), BF16),
            jax.ShapeDtypeStruct((N_KV_HEADS, WINDOW + SEQ, 2 * HEAD_DIM), BF16),
            jax.ShapeDtypeStruct((N_KV_HEADS, V_ROWS, WINDOW + SEQ), BF16),
        ],
        compiler_params=pltpu.CompilerParams(
            dimension_semantics=("arbitrary",), vmem_limit_bytes=VMEM_LIMIT),
        name="prep",
    )(proj, proj, proj, proj, proj, cos_tab, sin_tab)


def _cmp_kernel(x_ref, pe_ref, w1_ref, w2_ref, cos_ref, sin_ref, o_ref, ot_ref, x_scr, hi_scr):
    half = CMP_BLOCK // 2
    row = lax.broadcasted_iota(jnp.int32, (N_CMP_PAD, HEAD_DIM), 0)
    for h in range(N_KV_HEADS):
        x_scr[...] = x_ref[:, h * HEAD_DIM:(h + 1) * HEAD_DIM]
        acc_lo = jnp.zeros((N_CMP_PAD, CMP_HIDDEN), F32)
        acc_hi = jnp.zeros((N_CMP_PAD, CMP_HIDDEN), F32)
        for l in range(half):
            xl = x_scr[pl.ds(l, N_CMP_PAD, stride=CMP_STRIDE), :]
            acc_lo += _dot((xl + pe_ref[0, l:l + 1, :]).astype(BF16), w1_ref[0, l])
            acc_hi += _dot((xl + pe_ref[0, half + l:half + l + 1, :]).astype(BF16), w1_ref[0, half + l])
        hi_scr[0:N_CMP_PAD, :] = acc_hi
        hi_scr[N_CMP_PAD:N_CMP_PAD + 8, :] = jnp.zeros((8, CMP_HIDDEN), F32)
        pre = acc_lo + hi_scr[pl.ds(1, N_CMP_PAD), :]
        hdn = pre * jax.nn.sigmoid(pre)
        out = _dot(hdn.astype(BF16), w2_ref[0])
        out = _rope(out, cos_ref[0], sin_ref[0])
        out = jnp.where(row < N_CMP, out, 0.0)
        o_ref[0, h] = out.astype(BF16)
        ot_ref[0, h] = out.T.astype(BF16)


def _compress(proj, pe, w1, w2, cos_c, sin_c):
    return pl.pallas_call(
        _cmp_kernel,
        grid=(2,),
        in_specs=[
            pl.BlockSpec((SEQ, KV_WIDTH), lambda k: (0, COL_KC // KV_WIDTH + k)),
            pl.BlockSpec((1, CMP_BLOCK, HEAD_DIM), lambda k: (k, 0, 0)),
            pl.BlockSpec((1, CMP_BLOCK, HEAD_DIM, CMP_HIDDEN), lambda k: (k, 0, 0, 0)),
            pl.BlockSpec((1, CMP_HIDDEN, HEAD_DIM), lambda k: (k, 0, 0)),
            pl.BlockSpec((1, N_CMP_PAD, HEAD_DIM), lambda k: (k, 0, 0)),
            pl.BlockSpec((1, N_CMP_PAD, HEAD_DIM), lambda k: (k, 0, 0)),
        ],
        out_specs=[
            pl.BlockSpec((1, N_KV_HEADS, N_CMP_PAD, HEAD_DIM), lambda k: (k, 0, 0, 0)),
            pl.BlockSpec((1, N_KV_HEADS, HEAD_DIM, N_CMP_PAD), lambda k: (k, 0, 0, 0)),
        ],
        out_shape=[
            jax.ShapeDtypeStruct((2, N_KV_HEADS, N_CMP_PAD, HEAD_DIM), BF16),
            jax.ShapeDtypeStruct((2, N_KV_HEADS, HEAD_DIM, N_CMP_PAD), BF16),
        ],
        scratch_shapes=[pltpu.VMEM((SEQ, HEAD_DIM), F32),
                        pltpu.VMEM((N_CMP_PAD + 8, CMP_HIDDEN), F32)],
        compiler_params=pltpu.CompilerParams(
            dimension_semantics=("arbitrary",), vmem_limit_bytes=VMEM_LIMIT),
        name="compress",
    )(proj, pe, w1, w2, cos_c, sin_c)


IMP_PAD = 8
NH = 2
N_STAT = 3


def _attn_kernel(q_ref, kc_ref, vct_ref, ks_ref, vst_ref, kw_ref, vwt_ref, g_ref, gn_ref,
                 o_ref, imp_scr, sel_scr, oc_scr, ow_scr, s_scr, p_scr, acc_scr, stat_scr, g_scr):
    i = pl.program_id(1)
    s0 = i * AQ
    nrow = AR
    heads = range(NH)

    q_t = [jnp.concatenate([q_ref[hh * GROUP + g] for g in range(GROUP)], axis=1) for hh in heads]

    def mask_groups(s, mask):
        return jnp.concatenate(
            [jnp.where(mask, s[:, g * AQ:(g + 1) * AQ], NEG) for g in range(GROUP)], axis=1)

    def compressed(n_chunks):
        rows = CMP_CHUNK * n_chunks
        mrows = min(rows, CMP_MASK_ROWS)
        c0 = rows - mrows
        r_i = lax.broadcasted_iota(jnp.int32, (mrows, AQ), 0)
        t_i = lax.broadcasted_iota(jnp.int32, (mrows, AQ), 1)
        visible = CMP_STRIDE * r_i - t_i <= s0 - (CMP_BLOCK - 1) - CMP_STRIDE * c0
        rest = IMP_PAD + N_CMP_PAD + 8 - (IMP_PAD + rows)
        for hh in heads:
            sc = _dot(kc_ref[0, hh, 0:rows, :], q_t[hh])
            tail = mask_groups(sc[c0:rows], visible)
            sc = tail if c0 == 0 else jnp.concatenate([sc[0:c0], tail], axis=0)
            m_c = jnp.max(sc, axis=0, keepdims=True)
            m_c = jnp.where(m_c > 0.5 * NEG, m_c, 0.0)
            e_c = jnp.exp2(sc - m_c)
            l_c = jnp.sum(e_c, axis=0, keepdims=True)
            p_c = e_c * (1.0 / jnp.where(l_c > 0.0, l_c, 1.0))
            oc_scr[hh] = _dot(vct_ref[0, hh, :, 0:rows], p_c.astype(BF16))
            imp_c = (p_c[:, 0:AQ] + p_c[:, AQ:2 * AQ]) + (p_c[:, 2 * AQ:3 * AQ] + p_c[:, 3 * AQ:4 * AQ])
            for a in range(AQ // LANES):
                imp_scr[hh, a, IMP_PAD:IMP_PAD + rows, :] = imp_c[:, a * LANES:(a + 1) * LANES]
                imp_scr[hh, a, IMP_PAD + rows:IMP_PAD + rows + rest, :] = jnp.zeros((rest, LANES), F32)

    for hh in heads:
        for a in range(AQ // LANES):
            imp_scr[hh, a, 0:IMP_PAD, :] = jnp.zeros((IMP_PAD, LANES), F32)
    n_cmp_chunks = i // (CMP_CHUNK * CMP_STRIDE // AQ) + 1
    for n in range(1, N_CMP_PAD // CMP_CHUNK + 1):
        pl.when(n_cmp_chunks == n)(functools.partial(compressed, n))

    ratio = SEL_BLOCK // CMP_STRIDE

    def importance(hh):
        tap = lambda k: jnp.concatenate(
            [imp_scr[hh, a, pl.ds(IMP_PAD + k, N_BLK, stride=ratio), :] for a in range(AQ // LANES)], axis=1)
        return tap(-1) + 2.0 * (tap(0) + tap(1) + tap(2)) + tap(3)

    jidx = lax.broadcasted_iota(jnp.int32, (N_BLK, AQ), 0).astype(F32)
    tok_l = lax.broadcasted_iota(jnp.int32, (1, AQ), 1)
    jt = (i * (AQ // SEL_BLOCK) + tok_l // SEL_BLOCK).astype(F32)
    miscount = []
    for hh in heads:
        v = jnp.where(jidx == 0.0, REMOVED, importance(hh))
        v = jnp.where(jidx == jt, REMOVED, v)
        v = jnp.where(jidx == jt - 1.0, REMOVED, v)
        v = jnp.where(jidx > jt, NEG, v)
        for _ in range(N_SEL - 3):
            v = jnp.where(v == jnp.max(v, axis=0, keepdims=True), REMOVED, v)
        sel_fast = jnp.where(jidx > jt, 0.0, jnp.where(v == REMOVED, 1.0, 0.0))
        n_picked = jnp.sum(sel_fast, axis=0, keepdims=True)
        miscount.append(jnp.max(jnp.abs(n_picked - jnp.minimum(jt + 1.0, float(N_SEL)))))
        sel_scr[hh] = sel_fast

    flag_pen = jnp.where(lax.broadcasted_iota(jnp.int32, (HEAD_DIM, nrow), 0) == 0, NEG, 0.0).astype(BF16)
    wk0 = pl.multiple_of(s0, AQ)
    r_i = lax.broadcasted_iota(jnp.int32, (AQ, AQ), 0)
    t_i = lax.broadcasted_iota(jnp.int32, (AQ, AQ), 1)
    tri_old = jnp.concatenate([jnp.where(r_i > t_i, 0.0, NEG).astype(BF16)] * GROUP, axis=1)
    tri_causal = jnp.concatenate([jnp.where(r_i <= t_i, 0.0, NEG).astype(BF16)] * GROUP, axis=1)
    for hh in heads:
        q_win = jnp.concatenate([q_t[hh], flag_pen], axis=0)
        sw = _dot(kw_ref[hh, pl.ds(wk0, WIN_KEYS), :], q_win).astype(BF16)
        sw = jnp.concatenate([sw[0:AQ] + tri_old, sw[AQ:WINDOW], sw[WINDOW:WIN_KEYS] + tri_causal], axis=0)
        m_w = jnp.max(sw, axis=0, keepdims=True)
        p_w = jnp.exp2(sw - m_w)
        ow_scr[hh] = _dot(vwt_ref[hh, :, pl.ds(wk0, WIN_KEYS)], p_w)
        s_scr[hh] = _dot(ks_ref[hh, 0:SEL_TK, 0:HEAD_DIM], q_t[hh]).astype(BF16)

    @pl.when(functools.reduce(jnp.maximum, miscount) > 0.5)
    def _():
        for hh in heads:
            w = jnp.where(jidx == 0.0, FORCE, importance(hh))
            w = jnp.where(jidx == jt, FORCE, w)
            w = jnp.where(jidx == jt - 1.0, FORCE, w)
            w = jnp.where(jidx > jt, NEG, w)
            sel = jnp.zeros((N_BLK, AQ), F32)
            for _ in range(N_SEL):
                mx = jnp.max(w, axis=0, keepdims=True)
                cand = jnp.where(w == mx, jidx, float(N_BLK))
                pick = jidx == jnp.min(cand, axis=0, keepdims=True)
                sel = jnp.where(pick, 1.0, sel)
                w = jnp.where(pick, REMOVED, w)
            sel_scr[hh] = jnp.where(jidx > jt, 0.0, sel)

    q_aug = []
    for hh in heads:
        pen = jnp.where(sel_scr[hh] > 0.5, 0.0, NEG).astype(BF16)
        q_aug.append(jnp.concatenate([q_t[hh], jnp.concatenate([pen] * GROUP, axis=1)], axis=0))

    def key_start(kt):
        return pl.multiple_of(kt * SEL_TK, SEL_TK)

    def col_max(s):
        return jnp.max(s, axis=0, keepdims=True).astype(F32)

    def stat(hh, k):
        return stat_scr[hh * N_STAT + k:hh * N_STAT + k + 1, :]

    def set_stat(hh, k, value):
        stat_scr[hh * N_STAT + k:hh * N_STAT + k + 1, :] = value

    def add_pv(hh, kt, p, alpha):
        pv = _dot(vst_ref[hh, :, pl.ds(key_start(kt), SEL_TK)], p)
        acc_scr[hh] = alpha * acc_scr[hh] + pv

    def step(kt):
        k_prev = key_start(jnp.maximum(kt - 1, 0))
        k_next = key_start(kt + 1)
        for hh in heads:
            for g in range(GROUP):
                cs = slice(g * AQ, (g + 1) * AQ)
                row = lambda k: stat_scr[hh * N_STAT + k:hh * N_STAT + k + 1, cs]
                m, m_tile, alpha_prev = row(0), row(1), row(2)
                p_prev = p_scr[hh, :, cs]
                s_cur = s_scr[hh, :, cs]
                s_nxt = _dot(ks_ref[hh, pl.ds(k_next, SEL_TK), :], q_aug[hh][:, cs]).astype(BF16)
                pv = _dot(vst_ref[hh, :, pl.ds(k_prev, SEL_TK)], p_prev)
                acc_scr[hh, :, cs] = alpha_prev * acc_scr[hh, :, cs] + pv
                m_new = jnp.maximum(m, m_tile)
                p_scr[hh, :, cs] = jnp.exp2(s_cur - m_new.astype(BF16))
                s_scr[hh, :, cs] = s_nxt
                stat_scr[hh * N_STAT + 0:hh * N_STAT + 1, cs] = m_new
                stat_scr[hh * N_STAT + 1:hh * N_STAT + 2, cs] = col_max(s_nxt)
                stat_scr[hh * N_STAT + 2:hh * N_STAT + 3, cs] = jnp.exp2(m - m_new)

    n_full = (i * AQ) // SEL_TK
    for hh in heads:
        pen0 = jnp.where(sel_scr[hh, 0:SEL_TK // SEL_BLOCK, :] > 0.5, 0.0, NEG)
        pen0 = jnp.concatenate([jnp.broadcast_to(pen0[b:b + 1, :], (SEL_BLOCK, AQ))
                                for b in range(SEL_TK // SEL_BLOCK)], axis=0).astype(BF16)
        s_first = s_scr[hh] + jnp.concatenate([pen0] * GROUP, axis=1)
        s_scr[hh] = s_first
        p_scr[hh] = jnp.zeros((SEL_TK, nrow), BF16)
        acc_scr[hh] = jnp.zeros((V_ROWS, nrow), F32)
        set_stat(hh, 0, jnp.full((1, nrow), NEG, F32))
        set_stat(hh, 1, col_max(s_first))
        set_stat(hh, 2, jnp.ones((1, nrow), F32))

    @pl.loop(0, n_full // 2)
    def _(j):
        step(2 * j)
        step(2 * j + 1)

    @pl.when(n_full % 2 == 1)
    def _():
        step(n_full - 1)

    off = pl.multiple_of(s0 - n_full * SEL_TK, AQ)
    for hh in heads:
        add_pv(hh, jnp.maximum(n_full - 1, 0), p_scr[hh], stat(hh, 2))
        s_scr[hh, pl.ds(off, AQ), :] = s_scr[hh, pl.ds(off, AQ), :] + tri_causal
        s_diag = s_scr[hh]
        m_s = stat(hh, 0)
        m_fin = jnp.maximum(m_s, col_max(s_diag))
        add_pv(hh, n_full, jnp.exp2(s_diag - m_fin.astype(BF16)), jnp.exp2(m_s - m_fin))

    g_scr[...] = g_ref[...].T
    for hh in heads:
        row0 = ((pl.program_id(0) * NH + hh) * GROUP) * 3
        grow = lambda c: jnp.concatenate(
            [g_scr[pl.ds(row0 + g * 3 + c, 1), :] for g in range(GROUP)], axis=1)
        l_s = acc_scr[hh, HEAD_DIM:HEAD_DIM + 1, :]
        l_w = ow_scr[hh, HEAD_DIM:HEAD_DIM + 1, :]
        out_t = (oc_scr[hh] * grow(0) + acc_scr[hh, 0:HEAD_DIM, :] * (grow(1) / l_s)
                 + ow_scr[hh, 0:HEAD_DIM, :] * (grow(2) / l_w))
        out = jnp.concatenate([out_t[:, g * AQ:(g + 1) * AQ].T for g in range(GROUP)], axis=1)
        cols = slice(hh * GROUP * HEAD_DIM, (hh + 1) * GROUP * HEAD_DIM)
        gn = gn_ref[:, cols]
        o_ref[:, cols] = (out * (gn * jax.nn.sigmoid(gn))).astype(BF16)


def _attention(q_r, kc, vct, ks_aug, vst, kw_aug, vwt, gates, proj):
    head_res = lambda shape: pl.BlockSpec((NH,) + shape, lambda hp, i: (hp, 0, 0), pipeline_mode=pl.Buffered(1))
    width = NH * GROUP * HEAD_DIM
    return pl.pallas_call(
        _attn_kernel,
        grid=(N_KV_HEADS // NH, SEQ // AQ),
        in_specs=[
            pl.BlockSpec((NH * GROUP, HEAD_DIM, AQ), lambda hp, i: (hp, 0, i)),
            pl.BlockSpec((1, NH, N_CMP_PAD, HEAD_DIM), lambda hp, i: (0, hp, 0, 0)),
            pl.BlockSpec((1, NH, HEAD_DIM, N_CMP_PAD), lambda hp, i: (1, hp, 0, 0)),
            head_res((SEQ, 2 * HEAD_DIM)),
            head_res((V_ROWS, SEQ)),
            head_res((WINDOW + SEQ, 2 * HEAD_DIM)),
            head_res((V_ROWS, WINDOW + SEQ)),
            pl.BlockSpec((AQ, LANES), lambda hp, i: (i, 0)),
            pl.BlockSpec((AQ, width), lambda hp, i: (i, COL_GN // width + hp)),
        ],
        out_specs=pl.BlockSpec((AQ, width), lambda hp, i: (i, hp)),
        out_shape=jax.ShapeDtypeStruct((SEQ, NSA_WIDTH), BF16),
        scratch_shapes=[pltpu.VMEM((NH, AQ // LANES, IMP_PAD + N_CMP_PAD + 8, LANES), F32),
                        pltpu.VMEM((NH, N_BLK, AQ), F32),
                        pltpu.VMEM((NH, HEAD_DIM, AR), F32),
                        pltpu.VMEM((NH, V_ROWS, AR), F32),
                        pltpu.VMEM((NH, SEL_TK, AR), BF16),
                        pltpu.VMEM((NH, SEL_TK, AR), BF16),
                        pltpu.VMEM((NH, V_ROWS, AR), F32),
                        pltpu.VMEM((NH * N_STAT, AR), F32),
                        pltpu.VMEM((LANES, AQ), F32)],
        compiler_params=pltpu.CompilerParams(
            dimension_semantics=("arbitrary", "arbitrary"), vmem_limit_bytes=VMEM_LIMIT),
        name="attn",
    )(q_r, kc, vct, ks_aug, vst, kw_aug, vwt, gates, proj)


def _pool_kernel(u_ref, up_ref, gp_ref, mix_ref, sc_ref, wpo_ref, ya_ref, ext_scr):
    i = pl.program_id(0)
    tm = POOL_TM
    ext_scr[0:POOL_HALO, :] = jnp.where(i > 0, up_ref[...], 0.0)
    ext_scr[POOL_HALO:POOL_HALO + tm, :] = u_ref[...]
    t1 = i * tm + lax.broadcasted_iota(jnp.int32, (tm, 1), 0) + 1
    zs = []
    for g, w in enumerate(POOL_WINDOWS):
        cs = slice(g * POOL_GROUP, (g + 1) * POOL_GROUP)
        u = ext_scr[POOL_HALO:POOL_HALO + tm, cs]
        acc = u
        for k in range(1, w):
            acc = acc + ext_scr[POOL_HALO - k:POOL_HALO - k + tm, cs]
        cnt = jnp.minimum(t1, w).astype(F32)
        pooled = acc / cnt - u
        zs.append(_dot(pooled.astype(BF16), mix_ref[g]))
    z = jnp.concatenate(zs, axis=1) * sc_ref[...]
    gp = gp_ref[...]
    y = z * (gp * jax.nn.sigmoid(gp))
    ya_ref[...] = _dot(y.astype(BF16), wpo_ref[...])


def _pool(proj, mix, scale, w_pool_out):
    tm = POOL_TM
    return pl.pallas_call(
        _pool_kernel,
        grid=(SEQ // tm,),
        in_specs=[
            pl.BlockSpec((tm, POOL_WIDTH), lambda i: (i, COL_U // POOL_WIDTH)),
            pl.BlockSpec((POOL_HALO, POOL_WIDTH),
                         lambda i: (jnp.maximum(i * (tm // POOL_HALO) - 1, 0), COL_U // POOL_WIDTH)),
            pl.BlockSpec((tm, POOL_WIDTH), lambda i: (i, COL_GP // POOL_WIDTH)),
            pl.BlockSpec((len(POOL_WINDOWS), POOL_GROUP, POOL_GROUP), lambda i: (0, 0, 0)),
            pl.BlockSpec((1, POOL_WIDTH), lambda i: (0, 0)),
            pl.BlockSpec((POOL_WIDTH, D_MODEL), lambda i: (0, 0)),
        ],
        out_specs=pl.BlockSpec((tm, D_MODEL), lambda i: (i, 0)),
        out_shape=jax.ShapeDtypeStruct((SEQ, D_MODEL), F32),
        scratch_shapes=[pltpu.VMEM((POOL_HALO + tm, POOL_WIDTH), F32)],
        compiler_params=pltpu.CompilerParams(
            dimension_semantics=("arbitrary",), vmem_limit_bytes=VMEM_LIMIT),
        name="pool",
    )(proj, proj, proj, mix, scale, w_pool_out)


def _out_kernel(x_ref, ya_ref, o_ref, gm0_ref, gm1_ref, wn_ref, wo_ref, fw_ref, out_ref):
    y_b = _dot(o_ref[...], wn_ref[...])
    merged = gm0_ref[...] * ya_ref[...] + gm1_ref[...] * y_b
    r = x_ref[...] + _dot(merged.astype(BF16), wo_ref[...])
    ms = jnp.mean(r * r, axis=-1, keepdims=True)
    out_ref[...] = r * lax.rsqrt(ms + EPS) * fw_ref[...]


def _output(x2, y_a, o_gated, proj, w_nsa_out, w_out, final_norm_w):
    tm = OUT_TM
    row = lambda: pl.BlockSpec((tm, D_MODEL), lambda i: (i, 0))
    resident = lambda: pl.BlockSpec((D_MODEL, D_MODEL), lambda i: (0, 0), pipeline_mode=pl.Buffered(1))
    return pl.pallas_call(
        _out_kernel,
        grid=(SEQ // tm,),
        in_specs=[
            row(), row(), row(),
            pl.BlockSpec((tm, D_MODEL), lambda i: (i, 0)),
            pl.BlockSpec((tm, D_MODEL), lambda i: (i, 1)),
            resident(), resident(),
            pl.BlockSpec((1, D_MODEL), lambda i: (0, 0)),
        ],
        out_specs=row(),
        out_shape=jax.ShapeDtypeStruct((SEQ, D_MODEL), F32),
        compiler_params=pltpu.CompilerParams(
            dimension_semantics=("arbitrary",), vmem_limit_bytes=VMEM_LIMIT),
        name="out",
    )(x2, y_a, o_gated, proj, proj, w_nsa_out, w_out, final_norm_w)


def _rope_tables():
    half = HEAD_DIM // 2
    inv = ROPE_THETA ** (-np.arange(half, dtype=np.float64) / half)
    ang = np.arange(SEQ, dtype=np.float64)[:, None] * inv[None, :]
    cos = np.concatenate([np.cos(ang), np.cos(ang)], axis=-1).astype(np.float32)
    sin = np.concatenate([-np.sin(ang), np.sin(ang)], axis=-1).astype(np.float32)
    pad = lambda t: np.pad(t[CMP_BLOCK - 1::CMP_STRIDE], ((0, N_CMP_PAD - N_CMP), (0, 0)))
    cos_c = np.stack([pad(cos), np.ones((N_CMP_PAD, HEAD_DIM), np.float32)])
    sin_c = np.stack([pad(sin), np.zeros((N_CMP_PAD, HEAD_DIM), np.float32)])
    return cos, sin, cos_c, sin_c


def kernel(x, norm_w, w_in, pool_mix, pool_scale, cmp_pe_k, cmp_w1_k, cmp_w2_k, cmp_pe_v, cmp_w1_v,
           cmp_w2_v, w_pool_out, w_nsa_out, w_merge, b_merge, w_out, final_norm_w):
    assert x.shape == (1, SEQ, D_MODEL) and norm_w.shape[0] == 1
    x2 = x.reshape(SEQ, D_MODEL)
    proj, gates = _proj(x2, norm_w, w_merge[0].astype(BF16), jnp.transpose(w_in[0]), b_merge)

    cos_tab, sin_tab, cos_c, sin_c = _rope_tables()
    q_r, ks_aug, vst, kw_aug, vwt = _prep(proj, cos_tab, sin_tab)

    pe = jnp.stack([cmp_pe_k[0], cmp_pe_v[0]])
    w1 = jnp.stack([cmp_w1_k[0], cmp_w1_v[0]]).astype(BF16)
    w2 = jnp.stack([cmp_w2_k[0], cmp_w2_v[0]]).astype(BF16)
    cmp_n, cmp_t = _compress(proj, pe, w1, w2, cos_c, sin_c)

    o_gated = _attention(q_r, cmp_n, cmp_t, ks_aug, vst, kw_aug, vwt, gates, proj)

    y_a = _pool(proj, pool_mix[0].astype(BF16), pool_scale, w_pool_out[0].astype(BF16))
    out = _output(x2, y_a, o_gated, proj, w_nsa_out[0].astype(BF16), w_out[0].astype(BF16),
                  final_norm_w[None, :])
    return out.reshape(1, SEQ, D_MODEL)
```
